```python
import math
import jax, jax.numpy as jnp
from jax import lax
import numpy as np

D_MODEL = 1024
BATCH = 4
SEQ = 8192
DEPTH = 2
DEC_BATCH = 32
DEC_SEQ = 1
PAST_LEN = 16384
PAGE_SIZE = 128

D_MIX = D_MODEL
POOL_WINDOWS = (2, 4, 8, 16)
N_POOL_GROUPS = len(POOL_WINDOWS)
D_POOL = D_MIX // 4
POOL_GROUP = D_POOL // N_POOL_GROUPS
POOL_BUF = max(POOL_WINDOWS) - 1

HEAD_DIM = 64
N_ATT_HEADS = (D_MIX // 4) // HEAD_DIM
D_ATT = N_ATT_HEADS * HEAD_DIM
DILATED = ((128, 1), (512, 4), (2048, 16))
ATT_WIN = max(w for w, _ in DILATED)
ATT_BLOCK = 128
ROPE_THETA = 10000.0

D_SSM = D_MIX - D_POOL - D_ATT
SSM_HEAD_DIM = 64
N_SSM_HEADS = D_SSM // SSM_HEAD_DIM
SSM_STATE = 128
SSM_GROUPS = 2
CONV_WIDTH = 4
SSM_CHUNK = 128
D_CONV = D_SSM + 2 * SSM_GROUPS * SSM_STATE
D_IN_PROJ = D_POOL + 3 * D_ATT + D_SSM + D_CONV + N_SSM_HEADS

D_FF = ((8 * D_MODEL // 3 + 127) // 128) * 128
RMS_EPS = 1e-6

kernel_name = 'hybrid_pool_dilated_ssd_decoder_step'


def rms_norm(x, g):
    xf = x.astype(jnp.float32)
    y = xf * lax.rsqrt(jnp.mean(xf * xf, -1, keepdims=True) + RMS_EPS)
    return (y * g.astype(jnp.float32)).astype(x.dtype)


def swiglu(x, w_gate, w_up, w_down):
    return jnp.matmul(jax.nn.silu(jnp.matmul(x, w_gate)) * jnp.matmul(x, w_up), w_down)


def rope(x, pos):
    half = x.shape[-1] // 2
    inv = ROPE_THETA ** (-jnp.arange(half, dtype=jnp.float32) / half)
    ang = pos.astype(jnp.float32)[:, None] * inv[None]
    cos = jnp.cos(ang)[None, :, None, :]
    sin = jnp.sin(ang)[None, :, None, :]
    x1, x2 = x[..., :half], x[..., half:]
    return jnp.concatenate([x1 * cos - x2 * sin, x2 * cos + x1 * sin], -1)


def split_proj(proj):
    cuts = np.cumsum([D_POOL, D_ATT, D_ATT, D_ATT, D_SSM, D_CONV]).tolist()
    return jnp.split(proj, cuts, axis=-1)


def pool_mix(u, pos, pool_w, pool_scale):
    n, L, _ = u.shape
    pad = max(POOL_WINDOWS)
    cs = jnp.pad(jnp.cumsum(u, axis=1), ((0, 0), (pad, 0), (0, 0)))
    means = []
    for g, w in enumerate(POOL_WINDOWS):
        c = cs[:, :, g * POOL_GROUP:(g + 1) * POOL_GROUP]
        s = c[:, pad:pad + L] - c[:, pad - w:pad - w + L]
        cnt = jnp.minimum(w, pos + 1).astype(jnp.float32)
        means.append(s / cnt[None, :, None])
    d = jnp.stack(means, 2) - u.reshape(n, L, N_POOL_GROUPS, POOL_GROUP)
    y = jnp.einsum('nlgc,gcd->nlgd', d, pool_w).reshape(n, L, D_POOL)
    return y * pool_scale


def dilated_branch_prompt(q, k, v, dil, n_back):
    b, S, h, e = q.shape
    L = S // dil
    nb = -(-L // ATT_BLOCK)
    Lp = nb * ATT_BLOCK

    def to_res(t):
        return jnp.pad(t.reshape(b, L, dil, h, e), ((0, 0), (0, Lp - L), (0, 0), (0, 0), (0, 0)))

    def band(t):
        tp = jnp.pad(t, ((0, 0), (ATT_BLOCK, 0), (0, 0), (0, 0), (0, 0)))
        prev = tp[:, :Lp].reshape(b, nb, ATT_BLOCK, dil, h, e)
        cur = t.reshape(b, nb, ATT_BLOCK, dil, h, e)
        return jnp.concatenate([prev, cur], 2)

    qb = to_res(q).reshape(b, nb, ATT_BLOCK, dil, h, e)
    kb, vb = band(to_res(k)), band(to_res(v))
    s = jnp.einsum('bnqrhe,bnkrhe->bnrhqk', qb, kb) / math.sqrt(e)
    i = jnp.arange(ATT_BLOCK)[:, None]
    j = jnp.arange(2 * ATT_BLOCK)[None]
    dist = ATT_BLOCK + i - j
    blk = jnp.arange(nb)[:, None, None]
    valid = (dist >= 0) & (dist <= n_back) & ((blk > 0) | (j >= ATT_BLOCK))
    s = jnp.where(valid[None, :, None, None], s, -jnp.inf)
    m = jnp.max(s, -1, keepdims=True)
    p = jnp.exp(s - m)
    l = jnp.sum(p, -1, keepdims=True)
    o = jnp.einsum('bnrhqk,bnkrhe->bnqrhe', p / l, vb)
    lse = jnp.transpose((m + jnp.log(l))[..., 0], (0, 1, 4, 2, 3))
    o = o.reshape(b, Lp, dil, h, e)[:, :L].reshape(b, S, h, e)
    lse = lse.reshape(b, Lp, dil, h)[:, :L].reshape(b, S, h)
    return o, lse


def dilated_branch_sample(q, kc, vc, dil, n_back, lb):
    T, e = q.shape[1], q.shape[-1]
    idx = lb + jnp.arange(T)[:, None] - dil * jnp.arange(n_back + 1)[None]
    valid = idx >= 0
    idx = jnp.maximum(idx, 0)
    kg, vg = kc[:, idx], vc[:, idx]
    s = jnp.einsum('nthe,ntjhe->nthj', q, kg) / math.sqrt(e)
    s = jnp.where(valid[None, :, None, :], s, -jnp.inf)
    m = jnp.max(s, -1, keepdims=True)
    p = jnp.exp(s - m)
    l = jnp.sum(p, -1, keepdims=True)
    o = jnp.einsum('nthj,ntjhe->nthe', p / l, vg)
    return o, (m + jnp.log(l))[..., 0]


def combine_by_denominator(outs, lses):
    wts = jax.nn.softmax(jnp.stack(lses, 0), axis=0)
    return jnp.einsum('gnth,gnthe->nthe', wts, jnp.stack(outs, 0))


def causal_conv(u, w, bias):
    L = u.shape[1] - (CONV_WIDTH - 1)
    return sum(u[:, t:t + L] * w[t] for t in range(CONV_WIDTH)) + bias


def split_ssm(u):
    n, L, _ = u.shape
    gn = SSM_GROUPS * SSM_STATE
    xs = u[..., :D_SSM].reshape(n, L, N_SSM_HEADS, SSM_HEAD_DIM)
    bm = u[..., D_SSM:D_SSM + gn].reshape(n, L, SSM_GROUPS, SSM_STATE)
    cm = u[..., D_SSM + gn:].reshape(n, L, SSM_GROUPS, SSM_STATE)
    return xs, bm, cm


def ssd_chunked(x, dt, a, bm, cm):
    b, S, H, P = x.shape
    G, N = bm.shape[2], bm.shape[3]
    J = H // G
    Q = SSM_CHUNK
    nc = S // Q
    xc = x.reshape(b, nc, Q, G, J, P)
    dtc = dt.reshape(b, nc, Q, G, J)
    bc = bm.reshape(b, nc, Q, G, N)
    cc = cm.reshape(b, nc, Q, G, N)
    acs = jnp.cumsum(dtc * a.reshape(G, J), axis=2)
    diff = acs[:, :, :, None] - acs[:, :, None]
    causal = jnp.tril(jnp.ones((Q, Q), bool))
    decay = jnp.exp(jnp.where(causal[:, :, None, None], diff, -jnp.inf))
    cb = jnp.einsum('bclgn,bcsgn->bclsg', cc, bc)
    mat = cb[..., None] * decay * dtc[:, :, None]
    y_diag = jnp.einsum('bclsgj,bcsgjp->bclgjp', mat, xc)
    decay_end = jnp.exp(acs[:, :, -1:] - acs)
    states = jnp.einsum('bclgn,bclgj,bclgjp->bcgjpn', bc, decay_end * dtc, xc)
    chunk_decay = jnp.exp(acs[:, :, -1])

    def step(h, inp):
        st, dec = inp
        return dec[..., None, None] * h + st, h

    h0 = jnp.zeros((b, G, J, P, N), x.dtype)
    h_last, h_prev = lax.scan(step, h0, (jnp.moveaxis(states, 1, 0), jnp.moveaxis(chunk_decay, 1, 0)))
    h_prev = jnp.moveaxis(h_prev, 0, 1)
    y_off = jnp.einsum('bclgn,bcgjpn,bclgj->bclgjp', cc, h_prev, jnp.exp(acs))
    return (y_diag + y_off).reshape(b, S, H, P), h_last.reshape(b, H, P, N)


def ssd_recurrent(x, dt, a, bm, cm, h0):
    J = x.shape[2] // bm.shape[2]
    bh = jnp.repeat(bm, J, axis=2)
    ch = jnp.repeat(cm, J, axis=2)

    def step(h, inp):
        xt, dtt, bt, ct = inp
        h = jnp.exp(dtt * a)[..., None, None] * h + (dtt[..., None] * xt)[..., None] * bt[:, :, None, :]
        return h, jnp.einsum('nhpk,nhk->nhp', h, ct)

    h_last, ys = lax.scan(step, h0, tuple(jnp.moveaxis(t, 1, 0) for t in (x, dt, bh, ch)))
    return jnp.moveaxis(ys, 0, 1), h_last


def ssm_gate_norm(y, xs, z, d_skip, ssm_norm):
    n, L = xs.shape[:2]
    y = (y + d_skip[:, None] * xs).reshape(n, L, D_SSM) * jax.nn.silu(z)
    yg = y.reshape(n, L, SSM_GROUPS, D_SSM // SSM_GROUPS)
    yg = yg * lax.rsqrt(jnp.mean(yg * yg, -1, keepdims=True) + RMS_EPS)
    return yg.reshape(n, L, D_SSM) * ssm_norm


def mixer_prompt(hn, w_in, pool_w, pool_scale, conv_w, conv_b, dt_bias, a_log, d_skip, ssm_norm, w_out):
    n, L, _ = hn.shape
    xa, q, k, v, z, xbc, dtr = split_proj(jnp.matmul(hn, w_in).astype(jnp.float32))
    pos = jnp.arange(L)
    ya = pool_mix(xa, pos, pool_w, pool_scale)
    q = rope(q.reshape(n, L, N_ATT_HEADS, HEAD_DIM), pos)
    k = rope(k.reshape(n, L, N_ATT_HEADS, HEAD_DIM), pos)
    v = v.reshape(n, L, N_ATT_HEADS, HEAD_DIM)
    outs, lses = [], []
    for win, dil in DILATED:
        o, lse = dilated_branch_prompt(q, k, v, dil, win // dil)
        outs.append(o)
        lses.append(lse)
    yb = combine_by_denominator(outs, lses).reshape(n, L, D_ATT)
    u = jax.nn.silu(causal_conv(jnp.pad(xbc, ((0, 0), (CONV_WIDTH - 1, 0), (0, 0))), conv_w, conv_b))
    xs, bm, cm = split_ssm(u)
    dt = jax.nn.softplus(dtr + dt_bias)
    y, h_last = ssd_chunked(xs, dt, -jnp.exp(a_log.astype(jnp.float32)), bm, cm)
    yc = ssm_gate_norm(y, xs, z, d_skip, ssm_norm)
    out = jnp.matmul(jnp.concatenate([ya, yb, yc], -1).astype(hn.dtype), w_out)
    wb = min(ATT_WIN, L)
    return out, (xa[:, -POOL_BUF:], k[:, -wb:], v[:, -wb:], xbc[:, -(CONV_WIDTH - 1):], h_last)


def mixer_sample(hn, c_pool, c_k, c_v, s_conv, s_ssm, w_in, pool_w, pool_scale, conv_w, conv_b,
                 dt_bias, a_log, d_skip, ssm_norm, w_out):
    n, T, _ = hn.shape
    f32 = jnp.float32
    xa, q, k, v, z, xbc, dtr = split_proj(jnp.matmul(hn, w_in).astype(f32))
    pos = PAST_LEN + jnp.arange(T)
    xa_cat = jnp.concatenate([c_pool.astype(f32), xa], 1)
    pos_cat = PAST_LEN - POOL_BUF + jnp.arange(POOL_BUF + T)
    ya = pool_mix(xa_cat, pos_cat, pool_w, pool_scale)[:, POOL_BUF:]
    q = rope(q.reshape(n, T, N_ATT_HEADS, HEAD_DIM), pos)
    k = rope(k.reshape(n, T, N_ATT_HEADS, HEAD_DIM), pos)
    v = v.reshape(n, T, N_ATT_HEADS, HEAD_DIM)
    lb = c_k.shape[1]
    kc = jnp.concatenate([c_k.astype(f32), k], 1)
    vc = jnp.concatenate([c_v.astype(f32), v], 1)
    outs, lses = [], []
    for win, dil in DILATED:
        o, lse = dilated_branch_sample(q, kc, vc, dil, win // dil, lb)
        outs.append(o)
        lses.append(lse)
    yb = combine_by_denominator(outs, lses).reshape(n, T, D_ATT)
    xbc_cat = jnp.concatenate([s_conv.astype(f32), xbc], 1)
    u = jax.nn.silu(causal_conv(xbc_cat, conv_w, conv_b))
    xs, bm, cm = split_ssm(u)
    dt = jax.nn.softplus(dtr + dt_bias)
    y, h_last = ssd_recurrent(xs, dt, -jnp.exp(a_log.astype(f32)), bm, cm, s_ssm.astype(f32))
    yc = ssm_gate_norm(y, xs, z, d_skip, ssm_norm)
    out = jnp.matmul(jnp.concatenate([ya, yb, yc], -1).astype(hn.dtype), w_out)
    return out, (xa_cat[:, -POOL_BUF:], kc[:, -lb:], vc[:, -lb:], xbc_cat[:, -(CONV_WIDTH - 1):], h_last)


def setup_inputs(seed: int = 0) -> dict:
    key = jax.random.key(seed)
    ks = iter(jax.random.split(key, 32))
    f32 = jnp.float32

    def nrm(shape, scale=1.0):
        return scale * jax.random.normal(next(ks), shape, f32)

    wb = min(ATT_WIN, PAST_LEN)
    inp = {}
    inp['x_prompt'] = nrm((BATCH, SEQ, D_MODEL))
    inp['x_sample'] = nrm((DEC_BATCH, DEC_SEQ, D_MODEL))
    inp['cache_pool'] = nrm((DEPTH, DEC_BATCH, POOL_BUF, D_POOL))
    inp['cache_k'] = nrm((DEPTH, DEC_BATCH, wb, N_ATT_HEADS, HEAD_DIM))
    inp['cache_v'] = nrm((DEPTH, DEC_BATCH, wb, N_ATT_HEADS, HEAD_DIM))
    inp['state_conv'] = nrm((DEPTH, DEC_BATCH, CONV_WIDTH - 1, D_CONV))
    inp['state_ssm'] = nrm((DEPTH, DEC_BATCH, N_SSM_HEADS, SSM_HEAD_DIM, SSM_STATE), 0.1)
    inp['ffn1_norm'] = 1.0 + nrm((DEPTH, D_MODEL), 0.02)
    inp['ffn1_w_gate'] = nrm((DEPTH, D_MODEL, D_FF), D_MODEL ** -0.5)
    inp['ffn1_w_up'] = nrm((DEPTH, D_MODEL, D_FF), D_MODEL ** -0.5)
    inp['ffn1_w_down'] = nrm((DEPTH, D_FF, D_MODEL), D_FF ** -0.5)
    inp['mix_norm'] = 1.0 + nrm((DEPTH, D_MODEL), 0.02)
    inp['w_in'] = nrm((DEPTH, D_MODEL, D_IN_PROJ), D_MODEL ** -0.5)
    inp['pool_w'] = nrm((DEPTH, N_POOL_GROUPS, POOL_GROUP, POOL_GROUP), POOL_GROUP ** -0.5)
    inp['pool_scale'] = 1.0 + nrm((DEPTH, D_POOL), 0.02)
    inp['conv_w'] = nrm((DEPTH, CONV_WIDTH, D_CONV), CONV_WIDTH ** -0.5)
    inp['conv_b'] = nrm((DEPTH, D_CONV), 0.02)
    dt0 = jnp.exp(jax.random.uniform(next(ks), (DEPTH, N_SSM_HEADS), f32, math.log(1e-3), math.log(1e-1)))
    inp['dt_bias'] = dt0 + jnp.log(-jnp.expm1(-dt0))
    inp['a_log'] = jnp.log(jax.random.uniform(next(ks), (DEPTH, N_SSM_HEADS), f32, 1.0, 16.0))
    inp['d_skip'] = 1.0 + nrm((DEPTH, N_SSM_HEADS), 0.1)
    inp['ssm_norm'] = 1.0 + nrm((DEPTH, D_SSM), 0.02)
    inp['w_out'] = nrm((DEPTH, D_MIX, D_MODEL), D_MIX ** -0.5)
    inp['ffn2_norm'] = 1.0 + nrm((DEPTH, D_MODEL), 0.02)
    inp['ffn2_w_gate'] = nrm((DEPTH, D_MODEL, D_FF), D_MODEL ** -0.5)
    inp['ffn2_w_up'] = nrm((DEPTH, D_MODEL, D_FF), D_MODEL ** -0.5)
    inp['ffn2_w_down'] = nrm((DEPTH, D_FF, D_MODEL), D_FF ** -0.5)
    inp['final_norm'] = 1.0 + nrm((D_MODEL,), 0.02)
    return inp


def reference(x_prompt, x_sample, cache_pool, cache_k, cache_v, state_conv, state_ssm,
              ffn1_norm, ffn1_w_gate, ffn1_w_up, ffn1_w_down, mix_norm, w_in, pool_w, pool_scale,
              conv_w, conv_b, dt_bias, a_log, d_skip, ssm_norm, w_out,
              ffn2_norm, ffn2_w_gate, ffn2_w_up, ffn2_w_down, final_norm):
    yp, ys = x_prompt, x_sample
    st_p = [[] for _ in range(5)]
    st_s = [[] for _ in range(5)]
    for i in range(DEPTH):
        mw = (w_in[i], pool_w[i], pool_scale[i], conv_w[i], conv_b[i], dt_bias[i], a_log[i],
              d_skip[i], ssm_norm[i], w_out[i])
        f1 = (ffn1_w_gate[i], ffn1_w_up[i], ffn1_w_down[i])
        f2 = (ffn2_w_gate[i], ffn2_w_up[i], ffn2_w_down[i])
        yp = yp + 0.5 * swiglu(rms_norm(yp, ffn1_norm[i]), *f1)
        ys = ys + 0.5 * swiglu(rms_norm(ys, ffn1_norm[i]), *f1)
        mo_p, new_p = mixer_prompt(rms_norm(yp, mix_norm[i]), *mw)
        mo_s, new_s = mixer_sample(rms_norm(ys, mix_norm[i]), cache_pool[i], cache_k[i], cache_v[i],
                                   state_conv[i], state_ssm[i], *mw)
        yp = yp + mo_p.astype(yp.dtype)
        ys = ys + mo_s.astype(ys.dtype)
        yp = yp + 0.5 * swiglu(rms_norm(yp, ffn2_norm[i]), *f2)
        ys = ys + 0.5 * swiglu(rms_norm(ys, ffn2_norm[i]), *f2)
        for lst, val in zip(st_p, new_p):
            lst.append(val)
        for lst, val in zip(st_s, new_s):
            lst.append(val)
    y_prompt = rms_norm(yp, final_norm)
    y_sample = rms_norm(ys, final_norm)
    pool_p = jnp.stack(st_p[0], 0).astype(cache_pool.dtype)
    pool_s = jnp.stack(st_s[0], 0).astype(cache_pool.dtype)
    k_p = jnp.stack(st_p[1], 0).astype(cache_k.dtype)
    k_s = jnp.stack(st_s[1], 0).astype(cache_k.dtype)
    v_p = jnp.stack(st_p[2], 0).astype(cache_v.dtype)
    v_s = jnp.stack(st_s[2], 0).astype(cache_v.dtype)
    conv_p = jnp.stack(st_p[3], 0).astype(state_conv.dtype)
    conv_s = jnp.stack(st_s[3], 0).astype(state_conv.dtype)
    ssm_p = jnp.stack(st_p[4], 0).astype(state_ssm.dtype)
    ssm_s = jnp.stack(st_s[4], 0).astype(state_ssm.dtype)
    return (y_prompt, y_sample, pool_p, pool_s, k_p, k_s, v_p, v_s, conv_p, conv_s, ssm_p, ssm_s)
```

```python
import functools
import math

import jax
import jax.numpy as jnp
from jax import lax
from jax.experimental import pallas as pl
from jax.experimental.pallas import tpu as pltpu

f32 = jnp.float32
bf16 = jnp.bfloat16

D_MODEL = 1024
PAST_LEN = 16384
POOL_WINDOWS = (2, 4, 8, 16)
D_POOL = 256
POOL_GROUP = D_POOL // len(POOL_WINDOWS)
POOL_BUF = max(POOL_WINDOWS) - 1
HEAD_DIM = 64
N_ATT_HEADS = 4
D_ATT = N_ATT_HEADS * HEAD_DIM
DILATIONS = (16, 4, 1)
N_BACK = 128
ATT_WIN = 2048
ROPE_THETA = 10000.0
D_SSM = 512
SSM_HEAD_DIM = 64
N_SSM_HEADS = 8
SSM_STATE = 128
SSM_GROUPS = 2
CONV_WIDTH = 4
SSM_CHUNK = 128
D_CONV = D_SSM + 2 * SSM_GROUPS * SSM_STATE
D_IN_PROJ = D_POOL + 3 * D_ATT + D_SSM + D_CONV + N_SSM_HEADS
D_FF = 2816
RMS_EPS = 1e-6

LANES = 128
SUBLANES = 8
VMEM_LIMIT = 56 * 1024 * 1024

N_PROJ = ((D_IN_PROJ + LANES - 1) // LANES) * LANES
OFF_Q = D_POOL
OFF_K = OFF_Q + D_ATT
OFF_V = OFF_K + D_ATT
OFF_Z = OFF_V + D_ATT
OFF_XBC = OFF_Z + D_SSM
OFF_DT = OFF_XBC + D_CONV

ROW_TILE = 512
ATT_TILE = 2048
POOL_HALO = 16
CONV_HALO = 8
HIGHEST = lax.Precision.HIGHEST


def _params(n_axes):
    return pltpu.CompilerParams(dimension_semantics=("arbitrary",) * n_axes, vmem_limit_bytes=VMEM_LIMIT)


def _const_spec(shape):
    nd = len(shape)
    return pl.BlockSpec(shape, lambda *_: (0,) * nd, pipeline_mode=pl.Buffered(1))


def _rms(x, g):
    return x * lax.rsqrt(jnp.mean(x * x, -1, keepdims=True) + RMS_EPS) * g


def _silu(x):
    return x * jax.nn.sigmoid(x)


def _softplus(x):
    return jnp.maximum(x, 0.0) + jnp.log1p(jnp.exp(-jnp.abs(x)))


def _rope_slab(x, cos, sin_signed):
    lane = lax.broadcasted_iota(jnp.int32, x.shape, 1)
    first_half = (lane % HEAD_DIM) < (HEAD_DIM // 2)
    partner = jnp.where(first_half, pltpu.roll(x, LANES - HEAD_DIM // 2, 1), pltpu.roll(x, HEAD_DIM // 2, 1))
    return x * cos + partner * sin_signed


def _pool_select(s2, s4, s8, s16):
    lane = lax.broadcasted_iota(jnp.int32, s2.shape, 1)
    return jnp.where(lane < POOL_GROUP, s2, jnp.where(lane < 2 * POOL_GROUP, s4, jnp.where(lane < 3 * POOL_GROUP, s8, s16)))


def _pool_window(shape):
    lane = lax.broadcasted_iota(jnp.int32, shape, 1)
    return jnp.where(lane < POOL_GROUP, POOL_WINDOWS[0],
                     jnp.where(lane < 2 * POOL_GROUP, POOL_WINDOWS[1],
                               jnp.where(lane < 3 * POOL_GROUP, POOL_WINDOWS[2], POOL_WINDOWS[3])))


def _ffn_kernel(has_mix, has_final, *refs):
    refs = list(refs)
    x_ref = refs.pop(0)
    if has_mix:
        ya_ref, yb_ref, yc_ref, wo_ref = refs[:4]
        refs = refs[4:]
    g_ref, wg_ref, wu_ref, wd_ref = refs[:4]
    refs = refs[4:]
    if has_final:
        fg_ref = refs.pop(0)
    o_ref = refs.pop(0)

    x = x_ref[...]
    if has_mix:
        cat = jnp.concatenate([ya_ref[...], yb_ref[0], yb_ref[1], yc_ref[...]], axis=1)
        x = x + jnp.dot(cat, wo_ref[...], preferred_element_type=f32)
    xn = _rms(x, g_ref[...]).astype(bf16)
    gate = jnp.dot(xn, wg_ref[...], preferred_element_type=f32)
    up = jnp.dot(xn, wu_ref[...], preferred_element_type=f32)
    h = (_silu(gate) * up).astype(bf16)
    y = x + 0.5 * jnp.dot(h, wd_ref[...], preferred_element_type=f32)
    if has_final:
        y = _rms(y, fg_ref[...])
    o_ref[...] = y


def _ffn(x, g, wg, wu, wd, mix=None, final_g=None):
    m = x.shape[0]
    tm = min(ROW_TILE, m)
    assert m % tm == 0
    row = lambda w: pl.BlockSpec((tm, w), lambda i: (i, 0))
    args, specs = [x], [row(D_MODEL)]
    if mix is not None:
        ya, yb, yc, wo = mix
        args += [ya, yb, yc, wo]
        specs += [row(D_POOL), pl.BlockSpec((2, tm, LANES), lambda i: (0, i, 0)), row(D_SSM), _const_spec(wo.shape)]
    args += [g, wg, wu, wd]
    specs += [_const_spec(g.shape), _const_spec(wg.shape), _const_spec(wu.shape), _const_spec(wd.shape)]
    if final_g is not None:
        args.append(final_g)
        specs.append(_const_spec(final_g.shape))
    return pl.pallas_call(
        functools.partial(_ffn_kernel, mix is not None, final_g is not None),
        grid=(m // tm,),
        in_specs=specs,
        out_specs=row(D_MODEL),
        out_shape=jax.ShapeDtypeStruct((m, D_MODEL), f32),
        compiler_params=_params(1),
        name="ffn",
    )(*args)


def _inproj_kernel(tiles_per_seq, x_ref, g_ref, w_ref, pbd_ref, pscale_ref, cw_ref, cb_ref, dtb_ref, cos_ref, sin_ref,
                   ya_ref, q_ref, k_ref, v_ref, z_ref, u_ref, dt_ref, pst_ref, cst_ref, xa_buf, xbc_buf):
    tm = x_ref.shape[0]
    t = pl.program_id(0) % tiles_per_seq

    xn = _rms(x_ref[...], g_ref[...]).astype(bf16)
    proj = jnp.dot(xn, w_ref[...], preferred_element_type=f32)

    @pl.when(t == 0)
    def _():
        xa_buf[0:POOL_HALO, :] = jnp.zeros((POOL_HALO, D_POOL), f32)
        xbc_buf[0:CONV_HALO, :] = jnp.zeros((CONV_HALO, D_CONV), f32)

    xa = proj[:, 0:D_POOL]
    xa_buf[POOL_HALO:POOL_HALO + tm, :] = xa
    back = lambda i: xa_buf[POOL_HALO - i:POOL_HALO - i + tm, :]
    s2 = xa + back(1)
    s4 = s2 + back(2) + back(3)
    s8 = s4 + back(4) + back(5) + back(6) + back(7)
    s16 = s8
    for i in range(8, 16):
        s16 = s16 + back(i)
    pos = t * tm + lax.broadcasted_iota(jnp.int32, (tm, D_POOL), 0)
    cnt = jnp.minimum(_pool_window((tm, D_POOL)), pos + 1).astype(f32)
    dmean = _pool_select(s2, s4, s8, s16) / cnt - xa
    ya = jnp.dot(dmean.astype(bf16), pbd_ref[...], preferred_element_type=f32) * pscale_ref[...]
    ya_ref[...] = ya.astype(ya_ref.dtype)
    pst_ref[0] = xa[tm - POOL_HALO:tm, :]
    xa_buf[0:POOL_HALO, :] = xa[tm - POOL_HALO:tm, :]

    cos = cos_ref[...]
    sin = sin_ref[...]
    for s in range(2):
        q_ref[s] = _rope_slab(proj[:, OFF_Q + s * LANES:OFF_Q + (s + 1) * LANES], cos, sin)
        k_ref[s] = _rope_slab(proj[:, OFF_K + s * LANES:OFF_K + (s + 1) * LANES], cos, sin)
        v_ref[s] = proj[:, OFF_V + s * LANES:OFF_V + (s + 1) * LANES]

    z_ref[...] = proj[:, OFF_Z:OFF_Z + D_SSM]

    xbc = proj[:, OFF_XBC:OFF_XBC + D_CONV]
    xbc_buf[CONV_HALO:CONV_HALO + tm, :] = xbc
    conv = cb_ref[...] + xbc * cw_ref[CONV_WIDTH - 1:CONV_WIDTH, :]
    for j in range(1, CONV_WIDTH):
        conv = conv + xbc_buf[CONV_HALO - j:CONV_HALO - j + tm, :] * cw_ref[CONV_WIDTH - 1 - j:CONV_WIDTH - j, :]
    u_ref[...] = _silu(conv)
    cst_ref[0] = xbc[tm - CONV_HALO:tm, :]
    xbc_buf[0:CONV_HALO, :] = xbc[tm - CONV_HALO:tm, :]

    dt_ref[...] = _softplus(proj[:, OFF_DT:OFF_DT + LANES] + dtb_ref[...])


def _inproj_prompt(x, batch, seq, g, w, pbd, pscale, cw, cb, dtb, cos, sin):
    m = x.shape[0]
    tm = ROW_TILE
    tps = seq // tm
    row = lambda wd: pl.BlockSpec((tm, wd), lambda i: (i, 0))
    slab = pl.BlockSpec((2, tm, LANES), lambda i: (0, i, 0))
    tab = pl.BlockSpec((tm, LANES), lambda i: (i % tps, 0))
    out_shape = (
        jax.ShapeDtypeStruct((m, D_POOL), bf16),
        jax.ShapeDtypeStruct((2, m, LANES), f32),
        jax.ShapeDtypeStruct((2, m, LANES), f32),
        jax.ShapeDtypeStruct((2, m, LANES), f32),
        jax.ShapeDtypeStruct((m, D_SSM), f32),
        jax.ShapeDtypeStruct((m, D_CONV), f32),
        jax.ShapeDtypeStruct((m, LANES), f32),
        jax.ShapeDtypeStruct((batch, POOL_HALO, D_POOL), f32),
        jax.ShapeDtypeStruct((batch, CONV_HALO, D_CONV), f32),
    )
    out_specs = (
        row(D_POOL), slab, slab, slab, row(D_SSM), row(D_CONV), row(LANES),
        pl.BlockSpec((1, POOL_HALO, D_POOL), lambda i: (i // tps, 0, 0)),
        pl.BlockSpec((1, CONV_HALO, D_CONV), lambda i: (i // tps, 0, 0)),
    )
    in_specs = [row(D_MODEL), _const_spec(g.shape), _const_spec(w.shape), _const_spec(pbd.shape), _const_spec(pscale.shape),
                _const_spec(cw.shape), _const_spec(cb.shape), _const_spec(dtb.shape), tab, tab]
    return pl.pallas_call(
        functools.partial(_inproj_kernel, tps),
        grid=(m // tm,),
        in_specs=in_specs,
        out_specs=out_specs,
        out_shape=out_shape,
        scratch_shapes=[pltpu.VMEM((POOL_HALO + tm, D_POOL), f32), pltpu.VMEM((CONV_HALO + tm, D_CONV), f32)],
        compiler_params=_params(1),
        name="inproj",
    )(x, g, w, pbd, pscale, cw, cb, dtb, cos, sin)


def _attn_kernel(q_ref, kp_ref, kc_ref, vp_ref, vc_ref, o_ref, kbuf, vbuf, acc, mrun, lrun):
    t = pl.program_id(2)
    blk = N_BACK
    kbuf[0:ATT_TILE, :] = kp_ref[0, 0]
    kbuf[ATT_TILE:2 * ATT_TILE, :] = kc_ref[0, 0]
    vbuf[0:ATT_TILE, :] = vp_ref[0, 0]
    vbuf[ATT_TILE:2 * ATT_TILE, :] = vc_ref[0, 0]

    qi = lax.broadcasted_iota(jnp.int32, (blk, 2 * blk), 0)
    kj = lax.broadcasted_iota(jnp.int32, (blk, 2 * blk), 1)
    dist = blk + qi - kj
    band = (dist >= 0) & (dist <= N_BACK)
    bias_full = jnp.where(band, 0.0, -jnp.inf).astype(f32)
    bias_cur = jnp.where(band & (kj >= blk), 0.0, -jnp.inf).astype(f32)
    head0 = lax.broadcasted_iota(jnp.int32, (blk, LANES), 1) < HEAD_DIM
    head0_k = lax.broadcasted_iota(jnp.int32, (2 * blk, LANES), 1) < HEAD_DIM
    scale = 1.0 / math.sqrt(HEAD_DIM)

    for d in DILATIONS:
        first = d == DILATIONS[0]
        last = d == DILATIONS[-1]
        span = blk * d
        n_units = ATT_TILE // blk

        def unit(u, carry, d=d, first=first, last=last, span=span):
            sb = u // d
            r = u % d
            qstart = sb * span + r
            kstart = ATT_TILE + qstart - span
            if d > 1:
                qrows = pl.ds(qstart, blk, stride=d)
                krows = pl.ds(kstart, 2 * blk, stride=d)
            else:
                qrows = pl.ds(pl.multiple_of(qstart, blk), blk)
                krows = pl.ds(pl.multiple_of(kstart, blk), 2 * blk)
            qb = (q_ref[0, 0, qrows, :] * scale).astype(bf16)
            kb = kbuf[krows, :].astype(bf16)
            vb = vbuf[krows, :].astype(bf16)
            prev_ok = jnp.logical_or(t > 0, sb > 0)
            bias = jnp.where(prev_ok, bias_full, bias_cur)
            ps, ms, ls = [], [], []
            for hmask in (head0, jnp.logical_not(head0)):
                qh = jnp.where(hmask, qb, jnp.zeros_like(qb))
                s = lax.dot_general(qh, kb, (((1,), (1,)), ((), ())), preferred_element_type=f32) + bias
                mh = jnp.max(s, axis=1, keepdims=True)
                p = jnp.exp(s - mh)
                ps.append(p.astype(bf16))
                ms.append(mh)
                ls.append(jnp.sum(p, axis=1, keepdims=True))
            pcat = jnp.concatenate(ps, axis=1)
            vcat = jnp.concatenate([jnp.where(head0_k, vb, jnp.zeros_like(vb)),
                                    jnp.where(head0_k, jnp.zeros_like(vb), vb)], axis=0)
            pv = jnp.dot(pcat, vcat, preferred_element_type=f32)
            m_e = jnp.where(head0, ms[0], ms[1])
            l_e = jnp.where(head0, ls[0], ls[1])
            if first:
                acc[qrows, :] = pv
                mrun[qrows, :] = m_e
                lrun[qrows, :] = l_e
            else:
                m_old = mrun[qrows, :]
                m_new = jnp.maximum(m_old, m_e)
                a = jnp.exp(m_old - m_new)
                b = jnp.exp(m_e - m_new)
                acc_new = acc[qrows, :] * a + pv * b
                l_new = lrun[qrows, :] * a + l_e * b
                if last:
                    o_ref[0, 0, qrows, :] = (acc_new / l_new).astype(o_ref.dtype)
                else:
                    acc[qrows, :] = acc_new
                    lrun[qrows, :] = l_new
                    mrun[qrows, :] = m_new
            return carry

        lax.fori_loop(0, n_units, unit, 0)


def _attn_prompt(q, k, v, batch, seq):
    nt = seq // ATT_TILE
    q4 = q.reshape(2, batch, seq, LANES)
    k4 = k.reshape(2, batch, seq, LANES)
    v4 = v.reshape(2, batch, seq, LANES)
    cur = pl.BlockSpec((1, 1, ATT_TILE, LANES), lambda b, s, t: (s, b, t, 0))
    prev = pl.BlockSpec((1, 1, ATT_TILE, LANES), lambda b, s, t: (s, b, jnp.maximum(t - 1, 0), 0))
    out = pl.pallas_call(
        _attn_kernel,
        grid=(batch, 2, nt),
        in_specs=[cur, prev, cur, prev, cur],
        out_specs=cur,
        out_shape=jax.ShapeDtypeStruct((2, batch, seq, LANES), bf16),
        scratch_shapes=[pltpu.VMEM((2 * ATT_TILE, LANES), f32), pltpu.VMEM((2 * ATT_TILE, LANES), f32),
                        pltpu.VMEM((ATT_TILE, LANES), f32), pltpu.VMEM((ATT_TILE, LANES), f32),
                        pltpu.VMEM((ATT_TILE, LANES), f32)],
        compiler_params=_params(3),
        name="attn",
    )(q4, k4, k4, v4, v4)
    return out.reshape(2, batch * seq, LANES)


def _ssd_chunk(xs, bm, cm, dtp, z, a_pad, a_exp, e_mat, dskip, norm, st_ref):
    q = SSM_CHUNK
    gl = D_SSM // SSM_GROUPS
    li = lax.broadcasted_iota(jnp.int32, (q, q), 0)
    si = lax.broadcasted_iota(jnp.int32, (q, q), 1)
    causal = li >= si
    tril = causal.astype(f32)
    triu = (li <= si).astype(f32)

    dt_exp = jnp.dot(dtp, e_mat, precision=HIGHEST, preferred_element_type=f32)
    acs_exp = jnp.dot(tril, dt_exp * a_exp, precision=HIGHEST, preferred_element_type=f32)
    dt_t = dtp.T
    acs_row = jnp.dot((dtp * a_pad).T[0:SUBLANES], triu, precision=HIGHEST, preferred_element_type=f32)
    acs_last = acs_exp[q - 1:q, :]
    w_exp = jnp.exp(acs_last - acs_exp) * dt_exp
    eacs = jnp.exp(acs_exp)
    chunk_decay = jnp.exp(acs_last)

    xs_b = xs.astype(bf16)
    lane = lax.broadcasted_iota(jnp.int32, (q, LANES), 1)
    lo = lane < SSM_HEAD_DIM
    y_parts = []
    for g in range(SSM_GROUPS):
        bg = bm[:, g * SSM_STATE:(g + 1) * SSM_STATE].astype(bf16)
        cg = cm[:, g * SSM_STATE:(g + 1) * SSM_STATE].astype(bf16)
        cb = lax.dot_general(cg, bg, (((1,), (1,)), ((), ())), preferred_element_type=f32)
        heads_per_group = N_SSM_HEADS // SSM_GROUPS
        for pair in range(heads_per_group // 2):
            mats = []
            for hh in range(2):
                h = g * heads_per_group + pair * 2 + hh
                col = acs_exp[:, h * SSM_HEAD_DIM:h * SSM_HEAD_DIM + 1]
                diff = jnp.where(causal, col - acs_row[h:h + 1, :], -jnp.inf)
                mats.append((cb * jnp.exp(diff) * dt_t[h:h + 1, :]).astype(bf16))
            c0 = (g * heads_per_group + pair * 2) * SSM_HEAD_DIM
            xp = xs_b[:, c0:c0 + LANES]
            rhs = jnp.concatenate([jnp.where(lo, xp, jnp.zeros_like(xp)), jnp.where(lo, jnp.zeros_like(xp), xp)], axis=0)
            y_parts.append(jnp.dot(jnp.concatenate(mats, axis=1), rhs, preferred_element_type=f32))
    y = jnp.concatenate(y_parts, axis=1)

    offs = []
    for g in range(SSM_GROUPS):
        cg = cm[:, g * SSM_STATE:(g + 1) * SSM_STATE].astype(bf16)
        bg_t = bm[:, g * SSM_STATE:(g + 1) * SSM_STATE].T.astype(bf16)
        st_g = st_ref[:, g * gl:(g + 1) * gl]
        offs.append(jnp.dot(cg, st_g.astype(bf16), preferred_element_type=f32))
        xw = (xs[:, g * gl:(g + 1) * gl] * w_exp[:, g * gl:(g + 1) * gl]).astype(bf16)
        st_new = jnp.dot(bg_t, xw, preferred_element_type=f32)
        st_ref[:, g * gl:(g + 1) * gl] = st_g * chunk_decay[:, g * gl:(g + 1) * gl] + st_new
    y = y + jnp.concatenate(offs, axis=1) * eacs

    y = (y + dskip * xs) * _silu(z)
    outs = []
    for g in range(SSM_GROUPS):
        yg = y[:, g * gl:(g + 1) * gl]
        outs.append(yg * lax.rsqrt(jnp.mean(yg * yg, -1, keepdims=True) + RMS_EPS))
    return jnp.concatenate(outs, axis=1) * norm


def _ssd_kernel(u_ref, dt_ref, z_ref, apad_ref, aexp_ref, e_ref, dskip_ref, norm_ref, y_ref, state_ref, st):
    @pl.when(pl.program_id(1) == 0)
    def _():
        st[...] = jnp.zeros_like(st)

    for c in range(u_ref.shape[0] // SSM_CHUNK):
        rows = slice(c * SSM_CHUNK, (c + 1) * SSM_CHUNK)
        u = u_ref[rows, :]
        y = _ssd_chunk(u[:, 0:D_SSM], u[:, D_SSM:D_SSM + SSM_GROUPS * SSM_STATE], u[:, D_SSM + SSM_GROUPS * SSM_STATE:],
                       dt_ref[rows, :], z_ref[rows, :], apad_ref[...], aexp_ref[...], e_ref[...], dskip_ref[...],
                       norm_ref[...], st)
        y_ref[rows, :] = y.astype(y_ref.dtype)
    state_ref[0] = st[...]


def _ssd_prompt(u, dt, z, batch, seq, a_pad, a_exp, e_mat, dskip, norm):
    tq = ROW_TILE
    tps = seq // tq
    row = lambda wd: pl.BlockSpec((tq, wd), lambda b, t: (b * tps + t, 0))
    consts = [a_pad, a_exp, e_mat, dskip, norm]
    return pl.pallas_call(
        _ssd_kernel,
        grid=(batch, tps),
        in_specs=[row(D_CONV), row(LANES), row(D_SSM)] + [_const_spec(c.shape) for c in consts],
        out_specs=(row(D_SSM), pl.BlockSpec((1, SSM_STATE, D_SSM), lambda b, t: (b, 0, 0))),
        out_shape=(jax.ShapeDtypeStruct((batch * seq, D_SSM), bf16), jax.ShapeDtypeStruct((batch, SSM_STATE, D_SSM), f32)),
        scratch_shapes=[pltpu.VMEM((SSM_STATE, D_SSM), f32)],
        compiler_params=_params(2),
        name="ssd",
    )(u, dt, z, *consts)


def _sample_pre_kernel(x_ref, g_ref, w_ref, cpool_ref, pbd_ref, pscale_ref, cconv_ref, cw_ref, cb_ref, dtb_ref,
                       cos_ref, sin_ref, ya_ref, q_ref, k_ref, v_ref, z_ref, u_ref, dt_ref, pnew_ref, cnew_ref):
    xn = _rms(x_ref[...], g_ref[...]).astype(bf16)
    proj = jnp.dot(xn, w_ref[...], preferred_element_type=f32)

    xa = proj[:, 0:D_POOL]
    back = lambda i: cpool_ref[POOL_BUF - i]
    s2 = xa + back(1)
    s4 = s2 + back(2) + back(3)
    s8 = s4 + back(4) + back(5) + back(6) + back(7)
    s16 = s8
    for i in range(8, 16):
        s16 = s16 + back(i)
    cnt = jnp.minimum(_pool_window(xa.shape), PAST_LEN + 1).astype(f32)
    dmean = _pool_select(s2, s4, s8, s16) / cnt - xa
    ya = jnp.dot(dmean.astype(bf16), pbd_ref[...], preferred_element_type=f32) * pscale_ref[...]
    ya_ref[...] = ya.astype(ya_ref.dtype)
    for i in range(POOL_BUF - 1):
        pnew_ref[i] = cpool_ref[i + 1]
    pnew_ref[POOL_BUF - 1] = xa

    cos = cos_ref[...]
    sin = sin_ref[...]
    for s in range(2):
        sl = slice(s * LANES, (s + 1) * LANES)
        q_ref[:, sl] = _rope_slab(proj[:, OFF_Q + s * LANES:OFF_Q + (s + 1) * LANES], cos, sin)
        k_ref[:, sl] = _rope_slab(proj[:, OFF_K + s * LANES:OFF_K + (s + 1) * LANES], cos, sin)
    v_ref[...] = proj[:, OFF_V:OFF_V + D_ATT]
    z_ref[...] = proj[:, OFF_Z:OFF_Z + D_SSM]

    xbc = proj[:, OFF_XBC:OFF_XBC + D_CONV]
    conv = cb_ref[...] + xbc * cw_ref[CONV_WIDTH - 1:CONV_WIDTH, :]
    for j in range(1, CONV_WIDTH):
        conv = conv + cconv_ref[CONV_WIDTH - 1 - j] * cw_ref[CONV_WIDTH - 1 - j:CONV_WIDTH - j, :]
    u_ref[...] = _silu(conv)
    for j in range(CONV_WIDTH - 2):
        cnew_ref[j] = cconv_ref[j + 1]
    cnew_ref[CONV_WIDTH - 2] = xbc

    dt_ref[...] = _softplus(proj[:, OFF_DT:OFF_DT + LANES] + dtb_ref[...])


def _sample_pre(x, g, w, cpool, pbd, pscale, cconv, cw, cb, dtb, cos, sin):
    n = x.shape[0]
    args = [x, g, w, cpool, pbd, pscale, cconv, cw, cb, dtb, cos, sin]
    out_shape = (
        jax.ShapeDtypeStruct((n, D_POOL), bf16),
        jax.ShapeDtypeStruct((n, D_ATT), f32),
        jax.ShapeDtypeStruct((n, D_ATT), f32),
        jax.ShapeDtypeStruct((n, D_ATT), f32),
        jax.ShapeDtypeStruct((n, D_SSM), f32),
        jax.ShapeDtypeStruct((n, D_CONV), f32),
        jax.ShapeDtypeStruct((n, LANES), f32),
        jax.ShapeDtypeStruct((POOL_BUF, n, D_POOL), f32),
        jax.ShapeDtypeStruct((CONV_WIDTH - 1, n, D_CONV), f32),
    )
    full = lambda s: pl.BlockSpec(s.shape, lambda i, nd=len(s.shape): (0,) * nd)
    return pl.pallas_call(
        _sample_pre_kernel,
        grid=(1,),
        in_specs=[full(a) for a in args],
        out_specs=tuple(full(s) for s in out_shape),
        out_shape=out_shape,
        compiler_params=_params(1),
        name="sample_pre",
    )(*args)


def _sample_ssd_kernel(u_ref, dt_ref, z_ref, st_ref, aexp_ref, e_ref, dskip_ref, norm_ref, y_ref, stnew_ref):
    n = u_ref.shape[0]
    gl = D_SSM // SSM_GROUPS
    u = u_ref[...]
    xs = u[:, 0:D_SSM]
    dt_exp = jnp.dot(dt_ref[...], e_ref[...], precision=HIGHEST, preferred_element_type=f32)
    pad = jnp.zeros((LANES - n, D_SSM), f32)
    dec_t = jnp.concatenate([jnp.exp(dt_exp * aexp_ref[...]), pad], axis=0).T
    dtx_t = jnp.concatenate([dt_exp * xs, pad], axis=0).T
    lane = lax.broadcasted_iota(jnp.int32, (D_SSM, LANES), 1)
    y_t = jnp.zeros((D_SSM, LANES), f32)
    for i in range(n):
        dec = dec_t[:, i:i + 1]
        dtx = dtx_t[:, i:i + 1]
        ycols = []
        for g in range(SSM_GROUPS):
            rows = slice(g * gl, (g + 1) * gl)
            b_row = u[i:i + 1, D_SSM + g * SSM_STATE:D_SSM + (g + 1) * SSM_STATE]
            c_row = u[i:i + 1, D_SSM + (SSM_GROUPS + g) * SSM_STATE:D_SSM + (SSM_GROUPS + g + 1) * SSM_STATE]
            h_new = dec[rows] * st_ref[i, rows, :] + dtx[rows] * b_row
            stnew_ref[i, rows, :] = h_new
            ycols.append(jnp.sum(h_new * c_row, axis=1, keepdims=True))
        y_t = jnp.where(lane == i, jnp.concatenate(ycols, axis=0), y_t)
    y = y_t.T[0:n]
    y = (y + dskip_ref[...] * xs) * _silu(z_ref[...])
    outs = []
    for g in range(SSM_GROUPS):
        yg = y[:, g * gl:(g + 1) * gl]
        outs.append(yg * lax.rsqrt(jnp.mean(yg * yg, -1, keepdims=True) + RMS_EPS))
    y_ref[...] = (jnp.concatenate(outs, axis=1) * norm_ref[...]).astype(y_ref.dtype)


def _sample_ssd(u, dt, z, state, a_exp, e_mat, dskip, norm):
    n = u.shape[0]
    args = [u, dt, z, state, a_exp, e_mat, dskip, norm]
    out_shape = (jax.ShapeDtypeStruct((n, D_SSM), bf16), jax.ShapeDtypeStruct(state.shape, f32))
    full = lambda s: pl.BlockSpec(s.shape, lambda i, nd=len(s.shape): (0,) * nd)
    return pl.pallas_call(
        _sample_ssd_kernel,
        grid=(1,),
        in_specs=[full(a) for a in args],
        out_specs=tuple(full(s) for s in out_shape),
        out_shape=out_shape,
        compiler_params=_params(1),
        name="sample_ssd",
    )(*args)


def _sample_attn_kernel(has_prev, q_ref, kn_ref, vn_ref, kc_ref, vc_ref, *rest):
    if has_prev:
        rest = rest[2:]
    y_ref, ks_ref, vs_ref = rest
    nh = N_ATT_HEADS
    rows = kc_ref.shape[2]
    wb = rows // nh
    kc = kc_ref[0, 0]
    vc = vc_ref[0, 0]
    q4 = q_ref[0]
    kn = kn_ref[0]
    vn = vn_ref[0]
    scale = 1.0 / math.sqrt(HEAD_DIM)

    s_t = lax.dot_general(q4.astype(bf16), kc.astype(bf16), (((1,), (1,)), ((), ())), preferred_element_type=f32) * scale
    s_new = jnp.sum(q4 * kn, axis=1, keepdims=True) * scale
    col = lax.broadcasted_iota(jnp.int32, (nh, rows), 1)
    head = lax.broadcasted_iota(jnp.int32, (nh, rows), 0)
    dist = wb - col // nh
    own = (col % nh) == head
    ms, ps, ls, es = [], [], [], []
    for d in DILATIONS:
        valid = own & ((dist % d) == 0) & (dist <= N_BACK * d)
        s = jnp.where(valid, s_t, -jnp.inf)
        m = jnp.maximum(jnp.max(s, axis=1, keepdims=True), s_new)
        p = jnp.exp(s - m)
        e_new = jnp.exp(s_new - m)
        ms.append(m)
        ps.append(p)
        es.append(e_new)
        ls.append(jnp.sum(p, axis=1, keepdims=True) + e_new)
    m_all = jnp.maximum(jnp.maximum(ms[0], ms[1]), ms[2])
    p_all = jnp.zeros_like(ps[0])
    w_new = jnp.zeros_like(s_new)
    l_all = jnp.zeros_like(s_new)
    for m, p, e_new, l in zip(ms, ps, es, ls):
        c = jnp.exp(m - m_all)
        p_all = p_all + c * p
        w_new = w_new + c * e_new
        l_all = l_all + c * l
    o = jnp.dot(p_all.astype(bf16), vc.astype(bf16), preferred_element_type=f32) + w_new * vn
    y_ref[0] = o / l_all
    ks_ref[0, 0] = jnp.concatenate([kc[nh:], kn], axis=0)
    vs_ref[0, 0] = jnp.concatenate([vc[nh:], vn], axis=0)


def _sample_attn(layer, q, kn, vn, cache_k, cache_v, prev):
    depth, n, rows, e = cache_k.shape
    nh = N_ATT_HEADS
    small = pl.BlockSpec((1, nh, e), lambda i: (i, 0, 0))
    big = pl.BlockSpec((1, 1, rows, e), lambda i: (layer, i, 0, 0))
    in_specs = [small, small, small, big, big]
    args = [q, kn, vn, cache_k, cache_v]
    aliases = {}
    if prev is not None:
        in_specs += [pl.BlockSpec(memory_space=pl.ANY)] * 2
        args += list(prev)
        aliases = {5: 1, 6: 2}
    return pl.pallas_call(
        functools.partial(_sample_attn_kernel, prev is not None),
        grid=(n,),
        in_specs=in_specs,
        out_specs=(small, big, big),
        out_shape=(jax.ShapeDtypeStruct((n, nh, e), f32), jax.ShapeDtypeStruct(cache_k.shape, f32),
                   jax.ShapeDtypeStruct(cache_v.shape, f32)),
        input_output_aliases=aliases,
        compiler_params=_params(1),
        name="sample_attn",
    )(*args)


def _rope_tables(pos):
    half = HEAD_DIM // 2
    inv = ROPE_THETA ** (-jnp.arange(half, dtype=f32) / half)
    ang = pos.astype(f32)[:, None] * inv[None]
    cos = jnp.tile(jnp.cos(ang), (1, LANES // half))
    sin = jnp.sin(ang)
    sin_signed = jnp.tile(jnp.concatenate([-sin, sin], axis=1), (1, LANES // HEAD_DIM))
    return cos, sin_signed


def kernel(x_prompt, x_sample, cache_pool, cache_k, cache_v, state_conv, state_ssm, ffn1_norm, ffn1_w_gate, ffn1_w_up,
           ffn1_w_down, mix_norm, w_in, pool_w, pool_scale, conv_w, conv_b, dt_bias, a_log, d_skip, ssm_norm, w_out,
           ffn2_norm, ffn2_w_gate, ffn2_w_up, ffn2_w_down, final_norm):
    batch, seq, _ = x_prompt.shape
    n_dec = x_sample.shape[0]
    depth = w_in.shape[0]
    wb = cache_k.shape[2]
    assert seq % ATT_TILE == 0 and seq % ROW_TILE == 0 and x_sample.shape[1] == 1 and wb == ATT_WIN

    yp = x_prompt.reshape(batch * seq, D_MODEL)
    ys = x_sample.reshape(n_dec, D_MODEL)
    cos_p, sin_p = _rope_tables(jnp.arange(seq))
    cos_s, sin_s = _rope_tables(jnp.full((1,), PAST_LEN))
    e_mat = (jnp.arange(LANES)[:, None] == (jnp.arange(D_SSM)[None, :] // SSM_HEAD_DIM)).astype(f32)
    ck_flat = cache_k.reshape(depth, n_dec, wb * N_ATT_HEADS, HEAD_DIM)
    cv_flat = cache_v.reshape(depth, n_dec, wb * N_ATT_HEADS, HEAD_DIM)
    row2 = lambda a: a.reshape(1, -1)

    st_p = [[] for _ in range(5)]
    st_s = [[] for _ in range(3)]
    kv_s = None
    for i in range(depth):
        last = i == depth - 1
        f1 = (row2(ffn1_norm[i]), ffn1_w_gate[i].astype(bf16), ffn1_w_up[i].astype(bf16), ffn1_w_down[i].astype(bf16))
        f2 = (row2(ffn2_norm[i]), ffn2_w_gate[i].astype(bf16), ffn2_w_up[i].astype(bf16), ffn2_w_down[i].astype(bf16))
        w_pad = jnp.pad(w_in[i], ((0, 0), (0, N_PROJ - D_IN_PROJ))).astype(bf16)
        pbd = jax.scipy.linalg.block_diag(*[pool_w[i, g] for g in range(len(POOL_WINDOWS))]).astype(bf16)
        pscale = row2(pool_scale[i])
        cb = row2(conv_b[i])
        dtb = row2(jnp.pad(dt_bias[i], (0, LANES - N_SSM_HEADS)))
        a_neg = -jnp.exp(a_log[i].astype(f32))
        a_pad = row2(jnp.pad(a_neg, (0, LANES - N_SSM_HEADS)))
        a_exp = row2(jnp.repeat(a_neg, SSM_HEAD_DIM))
        dskip = row2(jnp.repeat(d_skip[i], SSM_HEAD_DIM))
        norm = row2(ssm_norm[i])
        wo = w_out[i].astype(bf16)
        g_mix = row2(mix_norm[i])
        fin = row2(final_norm) if last else None

        yp = _ffn(yp, *f1)
        ya, q, k, v, z, u, dt, pst, cst = _inproj_prompt(yp, batch, seq, g_mix, w_pad, pbd, pscale, conv_w[i], cb, dtb,
                                                         cos_p, sin_p)
        yb = _attn_prompt(q, k, v, batch, seq)
        yc, sst = _ssd_prompt(u, dt, z, batch, seq, a_pad, a_exp, e_mat, dskip, norm)
        yp = _ffn(yp, *f2, mix=(ya, yb, yc, wo), final_g=fin)

        def window(t):
            t = t.reshape(2, batch, seq, LANES)[:, :, seq - wb:]
            return jnp.transpose(t, (1, 2, 0, 3)).reshape(batch, wb, N_ATT_HEADS, HEAD_DIM)

        st_p[0].append(pst[:, POOL_HALO - POOL_BUF:])
        st_p[1].append(window(k))
        st_p[2].append(window(v))
        st_p[3].append(cst[:, CONV_HALO - (CONV_WIDTH - 1):])
        st_p[4].append(jnp.transpose(sst.reshape(batch, SSM_STATE, N_SSM_HEADS, SSM_HEAD_DIM), (0, 2, 3, 1)))

        ys = _ffn(ys, *f1)
        cpool = jnp.transpose(cache_pool[i], (1, 0, 2))
        cconv = jnp.transpose(state_conv[i], (1, 0, 2))
        ya_s, q_s, k_s, v_s, z_s, u_s, dt_s, pnew, cnew = _sample_pre(ys, g_mix, w_pad, cpool, pbd, pscale, cconv,
                                                                    conv_w[i], cb, dtb, cos_s, sin_s)
        heads = lambda t: t.reshape(n_dec, N_ATT_HEADS, HEAD_DIM)
        yb_s, ks_buf, vs_buf = _sample_attn(i, heads(q_s), heads(k_s), heads(v_s), ck_flat, cv_flat, kv_s)
        kv_s = (ks_buf, vs_buf)
        yc_s, ssm_new = _sample_ssd(u_s, dt_s, z_s, state_ssm[i].reshape(n_dec, D_SSM, SSM_STATE), a_exp, e_mat, dskip, norm)
        yb_slab = jnp.transpose(yb_s.reshape(n_dec, 2, LANES), (1, 0, 2)).astype(bf16)
        ys = _ffn(ys, *f2, mix=(ya_s, yb_slab, yc_s, wo), final_g=fin)
        st_s[0].append(jnp.transpose(pnew, (1, 0, 2)))
        st_s[1].append(jnp.transpose(cnew, (1, 0, 2)))
        st_s[2].append(ssm_new.reshape(n_dec, N_SSM_HEADS, SSM_HEAD_DIM, SSM_STATE))

    y_prompt = yp.reshape(batch, seq, D_MODEL)
    y_sample = ys.reshape(n_dec, 1, D_MODEL)
    pool_p, k_p, v_p, conv_p, ssm_p = (jnp.stack(s, 0) for s in st_p)
    pool_s, conv_s, ssm_s = (jnp.stack(s, 0) for s in st_s)
    k_s = kv_s[0].reshape(cache_k.shape)
    v_s = kv_s[1].reshape(cache_v.shape)
    return (y_prompt, y_sample, pool_p, pool_s, k_p, k_s, v_p, v_s, conv_p, conv_s, ssm_p, ssm_s)
```

```python
import functools
import math

import jax
import jax.numpy as jnp
from jax import lax
from jax.experimental import pallas as pl
from jax.experimental.pallas import tpu as pltpu

f32 = jnp.float32
bf16 = jnp.bfloat16

D_MODEL = 1024
PAST_LEN = 16384
POOL_WINDOWS = (2, 4, 8, 16)
D_POOL = 256
POOL_GROUP = D_POOL // len(POOL_WINDOWS)
POOL_BUF = max(POOL_WINDOWS) - 1
HEAD_DIM = 64
N_ATT_HEADS = 4
D_ATT = N_ATT_HEADS * HEAD_DIM
DILATIONS = (16, 4, 1)
N_BACK = 128
ATT_WIN = 2048
ROPE_THETA = 10000.0
D_SSM = 512
SSM_HEAD_DIM = 64
N_SSM_HEADS = 8
SSM_STATE = 128
SSM_GROUPS = 2
CONV_WIDTH = 4
SSM_CHUNK = 128
D_CONV = D_SSM + 2 * SSM_GROUPS * SSM_STATE
D_IN_PROJ = D_POOL + 3 * D_ATT + D_SSM + D_CONV + N_SSM_HEADS
D_FF = 2816
RMS_EPS = 1e-6

LANES = 128
SUBLANES = 8
VMEM_LIMIT = 56 * 1024 * 1024

OFF_DT = D_POOL
OFF_Q = OFF_DT + LANES
OFF_K = OFF_Q + D_ATT
OFF_V = OFF_K + D_ATT
OFF_Z = OFF_V + D_ATT
OFF_XBC = OFF_Z + D_SSM
N_PROJ = OFF_XBC + D_CONV

ROW_TILE = 512
ATT_TILE = 2048
ATT_UNROLL = 4
POOL_HALO = 16
CONV_HALO = 8


def _params(n_axes):
    return pltpu.CompilerParams(dimension_semantics=("arbitrary",) * n_axes, vmem_limit_bytes=VMEM_LIMIT)


def _const_spec(shape):
    nd = len(shape)
    return pl.BlockSpec(shape, lambda *_: (0,) * nd, pipeline_mode=pl.Buffered(1))


def _layer_spec(arr, layer):
    nd = arr.ndim - 1
    return pl.BlockSpec((None,) + arr.shape[1:], lambda *_: (layer,) + (0,) * nd, pipeline_mode=pl.Buffered(1))


def _rms(x, g):
    return x * lax.rsqrt(jnp.mean(x * x, -1, keepdims=True) + RMS_EPS) * g


def _dot_f32_by_01(x, m01, lhs_is_01=False):
    hi = x.astype(bf16)
    rest = x - hi.astype(f32)
    mid = rest.astype(bf16)
    lo = (rest - mid.astype(f32)).astype(bf16)
    out = None
    for part in (hi, mid, lo):
        if lhs_is_01:
            t = jnp.dot(m01, part, preferred_element_type=f32)
        else:
            t = jnp.dot(part, m01, preferred_element_type=f32)
        out = t if out is None else out + t
    return out


def _silu(x):
    return x * jax.nn.sigmoid(x)


def _softplus(x):
    return jnp.maximum(x, 0.0) + jnp.log1p(jnp.exp(-jnp.abs(x)))


def _rope_slab(x, cos, sin_signed):
    lane = lax.broadcasted_iota(jnp.int32, x.shape, 1)
    first_half = (lane % HEAD_DIM) < (HEAD_DIM // 2)
    partner = jnp.where(first_half, pltpu.roll(x, LANES - HEAD_DIM // 2, 1), pltpu.roll(x, HEAD_DIM // 2, 1))
    return x * cos + partner * sin_signed


def _pool_select(s2, s4, s8, s16):
    lane = lax.broadcasted_iota(jnp.int32, s2.shape, 1)
    return jnp.where(lane < POOL_GROUP, s2, jnp.where(lane < 2 * POOL_GROUP, s4, jnp.where(lane < 3 * POOL_GROUP, s8, s16)))


def _pool_window(shape):
    lane = lax.broadcasted_iota(jnp.int32, shape, 1)
    return jnp.where(lane < POOL_GROUP, POOL_WINDOWS[0],
                     jnp.where(lane < 2 * POOL_GROUP, POOL_WINDOWS[1],
                               jnp.where(lane < 3 * POOL_GROUP, POOL_WINDOWS[2], POOL_WINDOWS[3])))


def _ffn_kernel(has_mix, has_final, *refs):
    refs = list(refs)
    x_ref = refs.pop(0)
    if has_mix:
        ya_ref, yb_ref, yc_ref, wo_ref = refs[:4]
        refs = refs[4:]
    g_ref, wg_ref, wu_ref, wd_ref = refs[:4]
    refs = refs[4:]
    if has_final:
        fg_ref = refs.pop(0)
    o_ref = refs.pop(0)

    x = x_ref[...]
    if has_mix:
        cat = jnp.concatenate([ya_ref[...], yb_ref[0], yb_ref[1], yc_ref[...]], axis=1)
        x = x + jnp.dot(cat, wo_ref[...], preferred_element_type=f32)
    xn = _rms(x, g_ref[...]).astype(bf16)
    gate = jnp.dot(xn, wg_ref[...], preferred_element_type=f32)
    up = jnp.dot(xn, wu_ref[...], preferred_element_type=f32)
    h = (_silu(gate) * up).astype(bf16)
    y = x + 0.5 * jnp.dot(h, wd_ref[...], preferred_element_type=f32)
    if has_final:
        y = _rms(y, fg_ref[...])
    o_ref[...] = y


def _ffn(x, layer, g, wg, wu, wd, mix=None, final_g=None):
    m = x.shape[0]
    tm = min(ROW_TILE, m)
    assert m % tm == 0
    row = lambda w: pl.BlockSpec((tm, w), lambda i: (i, 0))
    args, specs = [x], [row(D_MODEL)]
    if mix is not None:
        ya, yb, yc, wo = mix
        args += [ya, yb, yc, wo]
        specs += [row(D_POOL), pl.BlockSpec((2, tm, LANES), lambda i: (0, i, 0)), row(D_SSM), _layer_spec(wo, layer)]
    args += [g, wg, wu, wd]
    specs += [_layer_spec(g, layer), _layer_spec(wg, layer), _layer_spec(wu, layer), _layer_spec(wd, layer)]
    if final_g is not None:
        args.append(final_g)
        specs.append(_const_spec(final_g.shape))
    return pl.pallas_call(
        functools.partial(_ffn_kernel, mix is not None, final_g is not None),
        grid=(m // tm,),
        in_specs=specs,
        out_specs=row(D_MODEL),
        out_shape=jax.ShapeDtypeStruct((m, D_MODEL), f32),
        compiler_params=_params(1),
        name="ffn",
    )(*args)


def _inproj_kernel(tiles_per_seq, x_ref, g_ref, w_ref, pbd_ref, pscale_ref, cw_ref, cb_ref, dtb_ref, cos_ref, sin_ref,
                   ya_ref, q_ref, k_ref, v_ref, z_ref, u_ref, dt_ref, pst_ref, cst_ref, xa_buf, xbc_buf):
    tm = x_ref.shape[0]
    t = pl.program_id(0) % tiles_per_seq

    xn = _rms(x_ref[...], g_ref[...]).astype(bf16)
    proj = lambda a, b: jnp.dot(xn, w_ref[:, a:b], preferred_element_type=f32)

    @pl.when(t == 0)
    def _():
        xa_buf[0:POOL_HALO, :] = jnp.zeros((POOL_HALO, D_POOL), f32)
        xbc_buf[0:CONV_HALO, :] = jnp.zeros((CONV_HALO, D_CONV), f32)

    xbc = proj(OFF_XBC, OFF_XBC + D_CONV)
    xbc_buf[CONV_HALO:CONV_HALO + tm, :] = xbc
    conv = cb_ref[...] + xbc * cw_ref[CONV_WIDTH - 1:CONV_WIDTH, :]
    for j in range(1, CONV_WIDTH):
        conv = conv + xbc_buf[CONV_HALO - j:CONV_HALO - j + tm, :] * cw_ref[CONV_WIDTH - 1 - j:CONV_WIDTH - j, :]
    u_ref[...] = _silu(conv)
    cst_ref[0] = xbc[tm - CONV_HALO:tm, :]
    xbc_buf[0:CONV_HALO, :] = xbc[tm - CONV_HALO:tm, :]

    pa = proj(0, OFF_Q)
    xa = pa[:, 0:D_POOL]
    dt_ref[...] = _softplus(pa[:, OFF_DT:OFF_DT + LANES] + dtb_ref[...])
    xa_buf[POOL_HALO:POOL_HALO + tm, :] = xa
    back = lambda i: xa_buf[POOL_HALO - i:POOL_HALO - i + tm, :]
    s2 = xa + back(1)
    s4 = s2 + back(2) + back(3)
    s8 = s4 + back(4) + back(5) + back(6) + back(7)
    s16 = s8
    for i in range(8, 16):
        s16 = s16 + back(i)
    pos = t * tm + lax.broadcasted_iota(jnp.int32, (tm, D_POOL), 0)
    cnt = jnp.minimum(_pool_window((tm, D_POOL)), pos + 1).astype(f32)
    dmean = _pool_select(s2, s4, s8, s16) / cnt - xa
    ya = jnp.dot(dmean.astype(bf16), pbd_ref[...], preferred_element_type=f32) * pscale_ref[...]
    ya_ref[...] = ya.astype(ya_ref.dtype)
    pst_ref[0] = xa[tm - POOL_HALO:tm, :]
    xa_buf[0:POOL_HALO, :] = xa[tm - POOL_HALO:tm, :]

    qkv = proj(OFF_Q, OFF_Z)
    cos = cos_ref[...]
    sin = sin_ref[...]
    for s in range(2):
        q_ref[s] = _rope_slab(qkv[:, s * LANES:(s + 1) * LANES], cos, sin)
        k_ref[s] = _rope_slab(qkv[:, D_ATT + s * LANES:D_ATT + (s + 1) * LANES], cos, sin)
        v_ref[s] = qkv[:, 2 * D_ATT + s * LANES:2 * D_ATT + (s + 1) * LANES]

    z_ref[...] = proj(OFF_Z, OFF_XBC)


def _inproj_prompt(x, layer, batch, seq, g, w, pbd, pscale, cw, cb, dtb, cos, sin):
    m = x.shape[0]
    tm = ROW_TILE
    tps = seq // tm
    row = lambda wd: pl.BlockSpec((tm, wd), lambda i: (i, 0))
    slab = pl.BlockSpec((2, tm, LANES), lambda i: (0, i, 0))
    tab = pl.BlockSpec((tm, LANES), lambda i: (i % tps, 0))
    out_shape = (
        jax.ShapeDtypeStruct((m, D_POOL), bf16),
        jax.ShapeDtypeStruct((2, m, LANES), f32),
        jax.ShapeDtypeStruct((2, m, LANES), f32),
        jax.ShapeDtypeStruct((2, m, LANES), f32),
        jax.ShapeDtypeStruct((m, D_SSM), f32),
        jax.ShapeDtypeStruct((m, D_CONV), f32),
        jax.ShapeDtypeStruct((m, LANES), f32),
        jax.ShapeDtypeStruct((batch, POOL_HALO, D_POOL), f32),
        jax.ShapeDtypeStruct((batch, CONV_HALO, D_CONV), f32),
    )
    out_specs = (
        row(D_POOL), slab, slab, slab, row(D_SSM), row(D_CONV), row(LANES),
        pl.BlockSpec((1, POOL_HALO, D_POOL), lambda i: (i // tps, 0, 0)),
        pl.BlockSpec((1, CONV_HALO, D_CONV), lambda i: (i // tps, 0, 0)),
    )
    in_specs = [row(D_MODEL), _layer_spec(g, layer), _layer_spec(w, layer), _const_spec(pbd.shape), _const_spec(pscale.shape),
                _const_spec(cw.shape), _const_spec(cb.shape), _const_spec(dtb.shape), tab, tab]
    return pl.pallas_call(
        functools.partial(_inproj_kernel, tps),
        grid=(m // tm,),
        in_specs=in_specs,
        out_specs=out_specs,
        out_shape=out_shape,
        scratch_shapes=[pltpu.VMEM((POOL_HALO + tm, D_POOL), f32), pltpu.VMEM((CONV_HALO + tm, D_CONV), f32)],
        compiler_params=_params(1),
        name="inproj",
    )(x, g, w, pbd, pscale, cw, cb, dtb, cos, sin)


def _attn_kernel(q_ref, kp_ref, kc_ref, vp_ref, vc_ref, o_ref, kbuf, vbuf, acc, mrun, lrun):
    t = pl.program_id(2)
    blk = N_BACK
    kbuf[0:ATT_TILE, :] = kp_ref[0, 0]
    kbuf[ATT_TILE:2 * ATT_TILE, :] = kc_ref[0, 0]
    vbuf[0:ATT_TILE, :] = vp_ref[0, 0]
    vbuf[ATT_TILE:2 * ATT_TILE, :] = vc_ref[0, 0]

    qi = lax.broadcasted_iota(jnp.int32, (2 * blk, 2 * blk), 0) % blk
    kj = lax.broadcasted_iota(jnp.int32, (2 * blk, 2 * blk), 1)
    dist = blk + qi - kj
    band = (dist >= 0) & (dist <= N_BACK)
    bias_full = jnp.where(band, 0.0, -jnp.inf).astype(f32)
    bias_cur = jnp.where(band & (kj >= blk), 0.0, -jnp.inf).astype(f32)
    head0 = lax.broadcasted_iota(jnp.int32, (blk, LANES), 1) < HEAD_DIM
    head0_k = lax.broadcasted_iota(jnp.int32, (2 * blk, LANES), 1) < HEAD_DIM
    scale = 1.0 / math.sqrt(HEAD_DIM)

    for d in DILATIONS:
        first = d == DILATIONS[0]
        last = d == DILATIONS[-1]
        span = blk * d
        n_units = ATT_TILE // blk

        def group(gi, carry, d=d, first=first, last=last, span=span):
            rows, scores, vcats = [], [], []
            for j in range(ATT_UNROLL):
                u = gi * ATT_UNROLL + j
                sb = u // d
                r = u % d
                qstart = sb * span + r
                kstart = ATT_TILE + qstart - span
                if d > 1:
                    qrows = pl.ds(qstart, blk, stride=d)
                    krows = pl.ds(kstart, 2 * blk, stride=d)
                else:
                    qrows = pl.ds(pl.multiple_of(qstart, blk), blk)
                    krows = pl.ds(pl.multiple_of(kstart, blk), 2 * blk)
                rows.append(qrows)
                qb = (q_ref[0, 0, qrows, :] * scale).astype(bf16)
                kb = kbuf[krows, :].astype(bf16)
                vb = vbuf[krows, :].astype(bf16)
                bias = jnp.where(jnp.logical_or(t > 0, sb > 0), bias_full, bias_cur)
                qcat = jnp.concatenate([jnp.where(head0, qb, jnp.zeros_like(qb)), jnp.where(head0, jnp.zeros_like(qb), qb)],
                                       axis=0)
                scores.append(lax.dot_general(qcat, kb, (((1,), (1,)), ((), ())), preferred_element_type=f32) + bias)
                vcats.append(jnp.concatenate([jnp.where(head0_k, vb, jnp.zeros_like(vb)),
                                              jnp.where(head0_k, jnp.zeros_like(vb), vb)], axis=0))
            stats = []
            for j in range(ATT_UNROLL):
                m2 = jnp.max(scores[j], axis=1, keepdims=True)
                p = jnp.exp(scores[j] - m2)
                l2 = jnp.sum(p, axis=1, keepdims=True)
                pb = p.astype(bf16)
                pcat = jnp.concatenate([pb[0:blk], pb[blk:2 * blk]], axis=1)
                stats.append((pcat, jnp.where(head0, m2[0:blk], m2[blk:2 * blk]), jnp.where(head0, l2[0:blk], l2[blk:2 * blk])))
            pvs = [jnp.dot(stats[j][0], vcats[j], preferred_element_type=f32) for j in range(ATT_UNROLL)]
            for j in range(ATT_UNROLL):
                qrows, pv, (_, m_e, l_e) = rows[j], pvs[j], stats[j]
                if first:
                    acc[qrows, :] = pv
                    mrun[qrows, :] = m_e
                    lrun[qrows, :] = l_e
                else:
                    m_old = mrun[qrows, :]
                    m_new = jnp.maximum(m_old, m_e)
                    a = jnp.exp(m_old - m_new)
                    b = jnp.exp(m_e - m_new)
                    acc_new = acc[qrows, :] * a + pv * b
                    l_new = lrun[qrows, :] * a + l_e * b
                    if last:
                        o_ref[0, 0, qrows, :] = (acc_new / l_new).astype(o_ref.dtype)
                    else:
                        acc[qrows, :] = acc_new
                        lrun[qrows, :] = l_new
                        mrun[qrows, :] = m_new
            return carry

        lax.fori_loop(0, n_units // ATT_UNROLL, group, 0)


def _attn_prompt(q, k, v, batch, seq):
    nt = seq // ATT_TILE
    q4 = q.reshape(2, batch, seq, LANES)
    k4 = k.reshape(2, batch, seq, LANES)
    v4 = v.reshape(2, batch, seq, LANES)
    cur = pl.BlockSpec((1, 1, ATT_TILE, LANES), lambda b, s, t: (s, b, t, 0))
    prev = pl.BlockSpec((1, 1, ATT_TILE, LANES), lambda b, s, t: (s, b, jnp.maximum(t - 1, 0), 0))
    out = pl.pallas_call(
        _attn_kernel,
        grid=(batch, 2, nt),
        in_specs=[cur, prev, cur, prev, cur],
        out_specs=cur,
        out_shape=jax.ShapeDtypeStruct((2, batch, seq, LANES), bf16),
        scratch_shapes=[pltpu.VMEM((2 * ATT_TILE, LANES), f32), pltpu.VMEM((2 * ATT_TILE, LANES), f32),
                        pltpu.VMEM((ATT_TILE, LANES), f32), pltpu.VMEM((ATT_TILE, LANES), f32),
                        pltpu.VMEM((ATT_TILE, LANES), f32)],
        compiler_params=_params(3),
        name="attn",
    )(q4, k4, k4, v4, v4)
    return out.reshape(2, batch * seq, LANES)


def _ssd_chunk(xs, bm, cm, dtp, z, a_pad, e_mat, dskip, norm, st_ref):
    q = SSM_CHUNK
    gl = D_SSM // SSM_GROUPS
    li = lax.broadcasted_iota(jnp.int32, (q, q), 0)
    si = lax.broadcasted_iota(jnp.int32, (q, q), 1)
    causal = li >= si
    tril = causal.astype(bf16)
    triu = (li <= si).astype(bf16)

    dta = dtp * a_pad
    cum = _dot_f32_by_01(dta, tril, lhs_is_01=True)
    both = _dot_f32_by_01(jnp.concatenate([dtp, cum], axis=0), e_mat)
    dt_exp = both[0:q]
    acs_exp = both[q:2 * q]
    dt_t = dtp.T
    acs_row = _dot_f32_by_01(dta.T[0:SUBLANES], triu)
    acs_last = acs_exp[q - 1:q, :]
    w_exp = jnp.exp(acs_last - acs_exp) * dt_exp
    eacs = jnp.exp(acs_exp)
    chunk_decay = jnp.exp(acs_last)

    xs_b = xs.astype(bf16)
    lane = lax.broadcasted_iota(jnp.int32, (q, LANES), 1)
    lo = lane < SSM_HEAD_DIM
    y_parts = []
    for g in range(SSM_GROUPS):
        bg = bm[:, g * SSM_STATE:(g + 1) * SSM_STATE].astype(bf16)
        cg = cm[:, g * SSM_STATE:(g + 1) * SSM_STATE].astype(bf16)
        cb = lax.dot_general(cg, bg, (((1,), (1,)), ((), ())), preferred_element_type=f32)
        heads_per_group = N_SSM_HEADS // SSM_GROUPS
        for pair in range(heads_per_group // 2):
            mats = []
            for hh in range(2):
                h = g * heads_per_group + pair * 2 + hh
                col = acs_exp[:, h * SSM_HEAD_DIM:h * SSM_HEAD_DIM + 1]
                diff = jnp.where(causal, col - acs_row[h:h + 1, :], -jnp.inf)
                mats.append((cb * jnp.exp(diff) * dt_t[h:h + 1, :]).astype(bf16))
            c0 = (g * heads_per_group + pair * 2) * SSM_HEAD_DIM
            xp = xs_b[:, c0:c0 + LANES]
            rhs = jnp.concatenate([jnp.where(lo, xp, jnp.zeros_like(xp)), jnp.where(lo, jnp.zeros_like(xp), xp)], axis=0)
            y_parts.append(jnp.dot(jnp.concatenate(mats, axis=1), rhs, preferred_element_type=f32))
    y = jnp.concatenate(y_parts, axis=1)

    offs = []
    for g in range(SSM_GROUPS):
        cg = cm[:, g * SSM_STATE:(g + 1) * SSM_STATE].astype(bf16)
        bg_t = bm[:, g * SSM_STATE:(g + 1) * SSM_STATE].T.astype(bf16)
        st_g = st_ref[:, g * gl:(g + 1) * gl]
        offs.append(jnp.dot(cg, st_g.astype(bf16), preferred_element_type=f32))
        xw = (xs[:, g * gl:(g + 1) * gl] * w_exp[:, g * gl:(g + 1) * gl]).astype(bf16)
        st_new = jnp.dot(bg_t, xw, preferred_element_type=f32)
        st_ref[:, g * gl:(g + 1) * gl] = st_g * chunk_decay[:, g * gl:(g + 1) * gl] + st_new
    y = y + jnp.concatenate(offs, axis=1) * eacs

    y = (y + dskip * xs) * _silu(z)
    outs = []
    for g in range(SSM_GROUPS):
        yg = y[:, g * gl:(g + 1) * gl]
        outs.append(yg * lax.rsqrt(jnp.mean(yg * yg, -1, keepdims=True) + RMS_EPS))
    return jnp.concatenate(outs, axis=1) * norm


def _ssd_kernel(u_ref, dt_ref, z_ref, apad_ref, e_ref, dskip_ref, norm_ref, y_ref, state_ref, st):
    @pl.when(pl.program_id(1) == 0)
    def _():
        st[...] = jnp.zeros_like(st)

    for c in range(u_ref.shape[0] // SSM_CHUNK):
        rows = slice(c * SSM_CHUNK, (c + 1) * SSM_CHUNK)
        u = u_ref[rows, :]
        y = _ssd_chunk(u[:, 0:D_SSM], u[:, D_SSM:D_SSM + SSM_GROUPS * SSM_STATE], u[:, D_SSM + SSM_GROUPS * SSM_STATE:],
                       dt_ref[rows, :], z_ref[rows, :], apad_ref[...], e_ref[...], dskip_ref[...], norm_ref[...], st)
        y_ref[rows, :] = y.astype(y_ref.dtype)
    state_ref[0] = st[...]


def _ssd_prompt(u, dt, z, batch, seq, a_pad, e_mat, dskip, norm):
    tq = ROW_TILE
    tps = seq // tq
    row = lambda wd: pl.BlockSpec((tq, wd), lambda b, t: (b * tps + t, 0))
    consts = [a_pad, e_mat, dskip, norm]
    return pl.pallas_call(
        _ssd_kernel,
        grid=(batch, tps),
        in_specs=[row(D_CONV), row(LANES), row(D_SSM)] + [_const_spec(c.shape) for c in consts],
        out_specs=(row(D_SSM), pl.BlockSpec((1, SSM_STATE, D_SSM), lambda b, t: (b, 0, 0))),
        out_shape=(jax.ShapeDtypeStruct((batch * seq, D_SSM), bf16), jax.ShapeDtypeStruct((batch, SSM_STATE, D_SSM), f32)),
        scratch_shapes=[pltpu.VMEM((SSM_STATE, D_SSM), f32)],
        compiler_params=_params(2),
        name="ssd",
    )(u, dt, z, *consts)


def _sample_pre_kernel(x_ref, g_ref, w_ref, cpool_ref, pbd_ref, pscale_ref, cconv_ref, cw_ref, cb_ref, dtb_ref,
                       cos_ref, sin_ref, ya_ref, q_ref, k_ref, v_ref, z_ref, u_ref, dt_ref, pnew_ref, cnew_ref):
    xn = _rms(x_ref[...], g_ref[...]).astype(bf16)
    proj = jnp.dot(xn, w_ref[...], preferred_element_type=f32)

    xa = proj[:, 0:D_POOL]
    back = lambda i: cpool_ref[POOL_BUF - i]
    s2 = xa + back(1)
    s4 = s2 + back(2) + back(3)
    s8 = s4 + back(4) + back(5) + back(6) + back(7)
    s16 = s8
    for i in range(8, 16):
        s16 = s16 + back(i)
    cnt = jnp.minimum(_pool_window(xa.shape), PAST_LEN + 1).astype(f32)
    dmean = _pool_select(s2, s4, s8, s16) / cnt - xa
    ya = jnp.dot(dmean.astype(bf16), pbd_ref[...], preferred_element_type=f32) * pscale_ref[...]
    ya_ref[...] = ya.astype(ya_ref.dtype)
    for i in range(POOL_BUF - 1):
        pnew_ref[i] = cpool_ref[i + 1]
    pnew_ref[POOL_BUF - 1] = xa

    cos = cos_ref[...]
    sin = sin_ref[...]
    for s in range(2):
        sl = slice(s * LANES, (s + 1) * LANES)
        q_ref[:, sl] = _rope_slab(proj[:, OFF_Q + s * LANES:OFF_Q + (s + 1) * LANES], cos, sin)
        k_ref[:, sl] = _rope_slab(proj[:, OFF_K + s * LANES:OFF_K + (s + 1) * LANES], cos, sin)
    v_ref[...] = proj[:, OFF_V:OFF_V + D_ATT]
    z_ref[...] = proj[:, OFF_Z:OFF_Z + D_SSM]

    xbc = proj[:, OFF_XBC:OFF_XBC + D_CONV]
    conv = cb_ref[...] + xbc * cw_ref[CONV_WIDTH - 1:CONV_WIDTH, :]
    for j in range(1, CONV_WIDTH):
        conv = conv + cconv_ref[CONV_WIDTH - 1 - j] * cw_ref[CONV_WIDTH - 1 - j:CONV_WIDTH - j, :]
    u_ref[...] = _silu(conv)
    for j in range(CONV_WIDTH - 2):
        cnew_ref[j] = cconv_ref[j + 1]
    cnew_ref[CONV_WIDTH - 2] = xbc

    dt_ref[...] = _softplus(proj[:, OFF_DT:OFF_DT + LANES] + dtb_ref[...])


def _sample_pre(x, layer, g, w, cpool, pbd, pscale, cconv, cw, cb, dtb, cos, sin):
    n = x.shape[0]
    args = [x, g, w, cpool, pbd, pscale, cconv, cw, cb, dtb, cos, sin]
    out_shape = (
        jax.ShapeDtypeStruct((n, D_POOL), bf16),
        jax.ShapeDtypeStruct((n, D_ATT), f32),
        jax.ShapeDtypeStruct((n, D_ATT), f32),
        jax.ShapeDtypeStruct((n, D_ATT), f32),
        jax.ShapeDtypeStruct((n, D_SSM), f32),
        jax.ShapeDtypeStruct((n, D_CONV), f32),
        jax.ShapeDtypeStruct((n, LANES), f32),
        jax.ShapeDtypeStruct((POOL_BUF, n, D_POOL), f32),
        jax.ShapeDtypeStruct((CONV_WIDTH - 1, n, D_CONV), f32),
    )
    full = lambda s: pl.BlockSpec(s.shape, lambda i, nd=len(s.shape): (0,) * nd)
    in_specs = [full(a) for a in args]
    in_specs[1] = _layer_spec(g, layer)
    in_specs[2] = _layer_spec(w, layer)
    return pl.pallas_call(
        _sample_pre_kernel,
        grid=(1,),
        in_specs=in_specs,
        out_specs=tuple(full(s) for s in out_shape),
        out_shape=out_shape,
        compiler_params=_params(1),
        name="sample_pre",
    )(*args)


def _sample_ssd_kernel(u_ref, dt_ref, z_ref, st_ref, aexp_ref, e_ref, dskip_ref, norm_ref, y_ref, stnew_ref):
    n = u_ref.shape[0]
    gl = D_SSM // SSM_GROUPS
    u = u_ref[...]
    xs = u[:, 0:D_SSM]
    dt_exp = _dot_f32_by_01(dt_ref[...], e_ref[...])
    pad = jnp.zeros((LANES - n, D_SSM), f32)
    dec_t = jnp.concatenate([jnp.exp(dt_exp * aexp_ref[...]), pad], axis=0).T
    dtx_t = jnp.concatenate([dt_exp * xs, pad], axis=0).T
    lane = lax.broadcasted_iota(jnp.int32, (D_SSM, LANES), 1)
    y_t = jnp.zeros((D_SSM, LANES), f32)
    for i in range(n):
        dec = dec_t[:, i:i + 1]
        dtx = dtx_t[:, i:i + 1]
        ycols = []
        for g in range(SSM_GROUPS):
            rows = slice(g * gl, (g + 1) * gl)
            b_row = u[i:i + 1, D_SSM + g * SSM_STATE:D_SSM + (g + 1) * SSM_STATE]
            c_row = u[i:i + 1, D_SSM + (SSM_GROUPS + g) * SSM_STATE:D_SSM + (SSM_GROUPS + g + 1) * SSM_STATE]
            h_new = dec[rows] * st_ref[i, rows, :] + dtx[rows] * b_row
            stnew_ref[i, rows, :] = h_new
            ycols.append(jnp.sum(h_new * c_row, axis=1, keepdims=True))
        y_t = jnp.where(lane == i, jnp.concatenate(ycols, axis=0), y_t)
    y = y_t.T[0:n]
    y = (y + dskip_ref[...] * xs) * _silu(z_ref[...])
    outs = []
    for g in range(SSM_GROUPS):
        yg = y[:, g * gl:(g + 1) * gl]
        outs.append(yg * lax.rsqrt(jnp.mean(yg * yg, -1, keepdims=True) + RMS_EPS))
    y_ref[...] = (jnp.concatenate(outs, axis=1) * norm_ref[...]).astype(y_ref.dtype)


def _sample_ssd(u, dt, z, state, a_exp, e_mat, dskip, norm):
    n = u.shape[0]
    args = [u, dt, z, state, a_exp, e_mat, dskip, norm]
    out_shape = (jax.ShapeDtypeStruct((n, D_SSM), bf16), jax.ShapeDtypeStruct(state.shape, f32))
    full = lambda s: pl.BlockSpec(s.shape, lambda i, nd=len(s.shape): (0,) * nd)
    return pl.pallas_call(
        _sample_ssd_kernel,
        grid=(1,),
        in_specs=[full(a) for a in args],
        out_specs=tuple(full(s) for s in out_shape),
        out_shape=out_shape,
        compiler_params=_params(1),
        name="sample_ssd",
    )(*args)


def _sample_attn_kernel(has_prev, q_ref, kn_ref, vn_ref, knc_ref, vnc_ref, kc_ref, vc_ref, *rest):
    if has_prev:
        rest = rest[2:]
    y_ref, ks_ref, vs_ref = rest
    wb = kc_ref.shape[3]
    kc = kc_ref[0, 0]
    vc = vc_ref[0, 0]
    q = q_ref[0]
    kn = kn_ref[0]
    vn = vn_ref[0]
    scale = 1.0 / math.sqrt(HEAD_DIM)

    own = (lax.broadcasted_iota(jnp.int32, (SUBLANES, D_ATT), 1) // HEAD_DIM) == lax.broadcasted_iota(
        jnp.int32, (SUBLANES, D_ATT), 0)
    qbd = jnp.where(own, jnp.broadcast_to(q, (SUBLANES, D_ATT)), 0.0)
    s_all = jnp.dot(qbd.astype(bf16), kc.astype(bf16), preferred_element_type=f32) * scale
    s_new = jnp.sum(jnp.where(own, jnp.broadcast_to(q * kn, (SUBLANES, D_ATT)), 0.0), axis=1, keepdims=True) * scale
    dist = wb - lax.broadcasted_iota(jnp.int32, (SUBLANES, wb), 1)
    ms, ps, ls, es = [], [], [], []
    for d in DILATIONS:
        valid = ((dist % d) == 0) & (dist <= N_BACK * d)
        s = jnp.where(valid, s_all, -jnp.inf)
        m = jnp.maximum(jnp.max(s, axis=1, keepdims=True), s_new)
        p = jnp.exp(s - m)
        e_new = jnp.exp(s_new - m)
        ms.append(m)
        ps.append(p)
        es.append(e_new)
        ls.append(jnp.sum(p, axis=1, keepdims=True) + e_new)
    m_all = jnp.maximum(jnp.maximum(ms[0], ms[1]), ms[2])
    p_all = jnp.zeros_like(ps[0])
    w_new = jnp.zeros_like(s_new)
    l_all = jnp.zeros_like(s_new)
    for m, p, e_new, l in zip(ms, ps, es, ls):
        c = jnp.exp(m - m_all)
        p_all = p_all + c * p
        w_new = w_new + c * e_new
        l_all = l_all + c * l
    o_full = lax.dot_general(p_all.astype(bf16), vc.astype(bf16), (((1,), (1,)), ((), ())), preferred_element_type=f32)
    to_row = lambda t: jnp.sum(jnp.where(own, t, 0.0), axis=0, keepdims=True)
    o_row = to_row(o_full) + to_row(jnp.broadcast_to(w_new, (SUBLANES, D_ATT))) * vn
    y_ref[0] = o_row / to_row(jnp.broadcast_to(l_all, (SUBLANES, D_ATT)))

    newest = lax.broadcasted_iota(jnp.int32, (D_ATT, wb), 1) == wb - 1
    ks_ref[0, 0] = jnp.where(newest, knc_ref[0], pltpu.roll(kc, wb - 1, 1))
    vs_ref[0, 0] = jnp.where(newest, vnc_ref[0], pltpu.roll(vc, wb - 1, 1))


def _sample_attn(layer, q, kn, vn, cache_k, cache_v, prev):
    depth, n, da, wb = cache_k.shape
    rowv = pl.BlockSpec((1, 1, da), lambda i: (i, 0, 0))
    colv = pl.BlockSpec((1, da, 1), lambda i: (i, 0, 0))
    big = pl.BlockSpec((1, 1, da, wb), lambda i: (layer, i, 0, 0))
    in_specs = [rowv, rowv, rowv, colv, colv, big, big]
    args = [q.reshape(n, 1, da), kn.reshape(n, 1, da), vn.reshape(n, 1, da), kn.reshape(n, da, 1), vn.reshape(n, da, 1),
            cache_k, cache_v]
    aliases = {}
    if prev is not None:
        in_specs += [pl.BlockSpec(memory_space=pl.ANY)] * 2
        aliases = {len(args): 1, len(args) + 1: 2}
        args += list(prev)
    return pl.pallas_call(
        functools.partial(_sample_attn_kernel, prev is not None),
        grid=(n,),
        in_specs=in_specs,
        out_specs=(rowv, big, big),
        out_shape=(jax.ShapeDtypeStruct((n, 1, da), f32), jax.ShapeDtypeStruct(cache_k.shape, f32),
                   jax.ShapeDtypeStruct(cache_v.shape, f32)),
        input_output_aliases=aliases,
        compiler_params=_params(1),
        name="sample_attn",
    )(*args)


def _rope_tables(pos):
    half = HEAD_DIM // 2
    inv = ROPE_THETA ** (-jnp.arange(half, dtype=f32) / half)
    ang = pos.astype(f32)[:, None] * inv[None]
    cos = jnp.tile(jnp.cos(ang), (1, LANES // half))
    sin = jnp.sin(ang)
    sin_signed = jnp.tile(jnp.concatenate([-sin, sin], axis=1), (1, LANES // HEAD_DIM))
    return cos, sin_signed


def kernel(x_prompt, x_sample, cache_pool, cache_k, cache_v, state_conv, state_ssm, ffn1_norm, ffn1_w_gate, ffn1_w_up,
           ffn1_w_down, mix_norm, w_in, pool_w, pool_scale, conv_w, conv_b, dt_bias, a_log, d_skip, ssm_norm, w_out,
           ffn2_norm, ffn2_w_gate, ffn2_w_up, ffn2_w_down, final_norm):
    batch, seq, _ = x_prompt.shape
    n_dec = x_sample.shape[0]
    depth = w_in.shape[0]
    wb = cache_k.shape[2]
    assert seq % ATT_TILE == 0 and seq % ROW_TILE == 0 and x_sample.shape[1] == 1 and wb == ATT_WIN

    yp = x_prompt.reshape(batch * seq, D_MODEL)
    ys = x_sample.reshape(n_dec, D_MODEL)
    cos_p, sin_p = _rope_tables(jnp.arange(seq))
    cos_s, sin_s = _rope_tables(jnp.full((1,), PAST_LEN))
    e_mat = (jnp.arange(LANES)[:, None] == (jnp.arange(D_SSM)[None, :] // SSM_HEAD_DIM)).astype(bf16)
    ck_t = jnp.transpose(cache_k, (0, 1, 3, 4, 2)).reshape(depth, n_dec, D_ATT, wb)
    cv_t = jnp.transpose(cache_v, (0, 1, 3, 4, 2)).reshape(depth, n_dec, D_ATT, wb)
    row2 = lambda a: a.reshape(1, -1)
    row3 = lambda a: a.reshape(depth, 1, -1)

    f1 = (row3(ffn1_norm), ffn1_w_gate.astype(bf16), ffn1_w_up.astype(bf16), ffn1_w_down.astype(bf16))
    f2 = (row3(ffn2_norm), ffn2_w_gate.astype(bf16), ffn2_w_up.astype(bf16), ffn2_w_down.astype(bf16))
    w_all = jnp.concatenate([w_in[:, :, 0:D_POOL],
                             jnp.pad(w_in[:, :, D_IN_PROJ - N_SSM_HEADS:], ((0, 0), (0, 0), (0, LANES - N_SSM_HEADS))),
                             w_in[:, :, D_POOL:D_IN_PROJ - N_SSM_HEADS]], axis=2).astype(bf16)
    wo_all = w_out.astype(bf16)
    g_mix = row3(mix_norm)

    st_p = [[] for _ in range(5)]
    st_s = [[] for _ in range(3)]
    kv_s = None
    for i in range(depth):
        last = i == depth - 1
        pbd = jax.scipy.linalg.block_diag(*[pool_w[i, g] for g in range(len(POOL_WINDOWS))]).astype(bf16)
        pscale = row2(pool_scale[i])
        cb = row2(conv_b[i])
        dtb = row2(jnp.pad(dt_bias[i], (0, LANES - N_SSM_HEADS)))
        a_neg = -jnp.exp(a_log[i].astype(f32))
        a_pad = row2(jnp.pad(a_neg, (0, LANES - N_SSM_HEADS)))
        a_exp = row2(jnp.repeat(a_neg, SSM_HEAD_DIM))
        dskip = row2(jnp.repeat(d_skip[i], SSM_HEAD_DIM))
        norm = row2(ssm_norm[i])
        fin = row2(final_norm) if last else None

        yp = _ffn(yp, i, *f1)
        ya, q, k, v, z, u, dt, pst, cst = _inproj_prompt(yp, i, batch, seq, g_mix, w_all, pbd, pscale, conv_w[i], cb, dtb,
                                                         cos_p, sin_p)
        yb = _attn_prompt(q, k, v, batch, seq)
        yc, sst = _ssd_prompt(u, dt, z, batch, seq, a_pad, e_mat, dskip, norm)
        yp = _ffn(yp, i, *f2, mix=(ya, yb, yc, wo_all), final_g=fin)

        def window(t):
            t = t.reshape(2, batch, seq, LANES)[:, :, seq - wb:]
            return jnp.transpose(t, (1, 2, 0, 3)).reshape(batch, wb, N_ATT_HEADS, HEAD_DIM)

        st_p[0].append(pst[:, POOL_HALO - POOL_BUF:])
        st_p[1].append(window(k))
        st_p[2].append(window(v))
        st_p[3].append(cst[:, CONV_HALO - (CONV_WIDTH - 1):])
        st_p[4].append(jnp.transpose(sst.reshape(batch, SSM_STATE, N_SSM_HEADS, SSM_HEAD_DIM), (0, 2, 3, 1)))

        ys = _ffn(ys, i, *f1)
        cpool = jnp.transpose(cache_pool[i], (1, 0, 2))
        cconv = jnp.transpose(state_conv[i], (1, 0, 2))
        ya_s, q_s, k_s, v_s, z_s, u_s, dt_s, pnew, cnew = _sample_pre(ys, i, g_mix, w_all, cpool, pbd, pscale, cconv,
                                                                    conv_w[i], cb, dtb, cos_s, sin_s)
        yb_s, ks_buf, vs_buf = _sample_attn(i, q_s, k_s, v_s, ck_t, cv_t, kv_s)
        kv_s = (ks_buf, vs_buf)
        yc_s, ssm_new = _sample_ssd(u_s, dt_s, z_s, state_ssm[i].reshape(n_dec, D_SSM, SSM_STATE), a_exp, e_mat, dskip, norm)
        yb_slab = jnp.transpose(yb_s.reshape(n_dec, 2, LANES), (1, 0, 2)).astype(bf16)
        ys = _ffn(ys, i, *f2, mix=(ya_s, yb_slab, yc_s, wo_all), final_g=fin)
        st_s[0].append(jnp.transpose(pnew, (1, 0, 2)))
        st_s[1].append(jnp.transpose(cnew, (1, 0, 2)))
        st_s[2].append(ssm_new.reshape(n_dec, N_SSM_HEADS, SSM_HEAD_DIM, SSM_STATE))

    y_prompt = yp.reshape(batch, seq, D_MODEL)
    y_sample = ys.reshape(n_dec, 1, D_MODEL)
    pool_p, k_p, v_p, conv_p, ssm_p = (jnp.stack(s, 0) for s in st_p)
    pool_s, conv_s, ssm_s = (jnp.stack(s, 0) for s in st_s)
    from_t = lambda t: jnp.transpose(t.reshape(depth, n_dec, N_ATT_HEADS, HEAD_DIM, wb), (0, 1, 4, 2, 3))
    return (y_prompt, y_sample, pool_p, pool_s, k_p, from_t(kv_s[0]), v_p, from_t(kv_s[1]), conv_p, conv_s, ssm_p, ssm_s)
```

```python
import functools
import math

import jax
import jax.numpy as jnp
from jax import lax
from jax.experimental import pallas as pl
from jax.experimental.pallas import tpu as pltpu

f32 = jnp.float32
bf16 = jnp.bfloat16

D_MODEL = 1024
PAST_LEN = 16384
POOL_WINDOWS = (2, 4, 8, 16)
D_POOL = 256
POOL_GROUP = D_POOL // len(POOL_WINDOWS)
POOL_BUF = max(POOL_WINDOWS) - 1
HEAD_DIM = 64
N_ATT_HEADS = 4
D_ATT = N_ATT_HEADS * HEAD_DIM
DILATIONS = (16, 4, 1)
N_BACK = 128
ATT_WIN = 2048
ROPE_THETA = 10000.0
D_SSM = 512
SSM_HEAD_DIM = 64
N_SSM_HEADS = 8
SSM_STATE = 128
SSM_GROUPS = 2
CONV_WIDTH = 4
SSM_CHUNK = 128
D_CONV = D_SSM + 2 * SSM_GROUPS * SSM_STATE
D_IN_PROJ = D_POOL + 3 * D_ATT + D_SSM + D_CONV + N_SSM_HEADS
D_FF = 2816
RMS_EPS = 1e-6

LANES = 128
SUBLANES = 8
VMEM_LIMIT = 56 * 1024 * 1024

OFF_DT = D_POOL
OFF_Q = OFF_DT + LANES
OFF_K = OFF_Q + D_ATT
OFF_V = OFF_K + D_ATT
OFF_Z = OFF_V + D_ATT
OFF_XBC = OFF_Z + D_SSM
N_PROJ = OFF_XBC + D_CONV

ROW_TILE = 512
ATT_TILE = 2048
ATT_UNROLL = 4
POOL_HALO = 16
CONV_HALO = 8


def _params(n_axes, flags=None):
    return pltpu.CompilerParams(dimension_semantics=("arbitrary",) * n_axes, vmem_limit_bytes=VMEM_LIMIT, flags=flags)


def _const_spec(shape):
    nd = len(shape)
    return pl.BlockSpec(shape, lambda *_: (0,) * nd, pipeline_mode=pl.Buffered(1))


def _layer_spec(arr, layer):
    nd = arr.ndim - 1
    return pl.BlockSpec((None,) + arr.shape[1:], lambda *_: (layer,) + (0,) * nd, pipeline_mode=pl.Buffered(1))


def _rms(x, g):
    return x * lax.rsqrt(jnp.mean(x * x, -1, keepdims=True) + RMS_EPS) * g


def _dot_f32_by_01(x, m01, lhs_is_01=False):
    hi = x.astype(bf16)
    rest = x - hi.astype(f32)
    mid = rest.astype(bf16)
    lo = (rest - mid.astype(f32)).astype(bf16)
    out = None
    for part in (hi, mid, lo):
        if lhs_is_01:
            t = jnp.dot(m01, part, preferred_element_type=f32)
        else:
            t = jnp.dot(part, m01, preferred_element_type=f32)
        out = t if out is None else out + t
    return out


def _silu(x):
    return x * jax.nn.sigmoid(x)


def _softplus(x):
    return jnp.maximum(x, 0.0) + jnp.log1p(jnp.exp(-jnp.abs(x)))


def _rope_slab(x, cos, sin_signed):
    lane = lax.broadcasted_iota(jnp.int32, x.shape, 1)
    first_half = (lane % HEAD_DIM) < (HEAD_DIM // 2)
    partner = jnp.where(first_half, pltpu.roll(x, LANES - HEAD_DIM // 2, 1), pltpu.roll(x, HEAD_DIM // 2, 1))
    return x * cos + partner * sin_signed


def _pool_select(s2, s4, s8, s16):
    lane = lax.broadcasted_iota(jnp.int32, s2.shape, 1)
    return jnp.where(lane < POOL_GROUP, s2, jnp.where(lane < 2 * POOL_GROUP, s4, jnp.where(lane < 3 * POOL_GROUP, s8, s16)))


def _pool_window(shape):
    lane = lax.broadcasted_iota(jnp.int32, shape, 1)
    return jnp.where(lane < POOL_GROUP, POOL_WINDOWS[0],
                     jnp.where(lane < 2 * POOL_GROUP, POOL_WINDOWS[1],
                               jnp.where(lane < 3 * POOL_GROUP, POOL_WINDOWS[2], POOL_WINDOWS[3])))


def _ffn_kernel(has_mix, has_final, *refs):
    refs = list(refs)
    x_ref = refs.pop(0)
    if has_mix:
        ya_ref, yb_ref, yc_ref, wo_ref = refs[:4]
        refs = refs[4:]
    g_ref, wg_ref, wu_ref, wd_ref = refs[:4]
    refs = refs[4:]
    if has_final:
        fg_ref = refs.pop(0)
    o_ref = refs.pop(0)

    x = x_ref[...]
    if has_mix:
        cat = jnp.concatenate([ya_ref[...], yb_ref[0], yb_ref[1], yc_ref[...]], axis=1)
        x = x + jnp.dot(cat, wo_ref[...], preferred_element_type=f32)
    xn = _rms(x, g_ref[...]).astype(bf16)
    gate = jnp.dot(xn, wg_ref[...], preferred_element_type=f32)
    up = jnp.dot(xn, wu_ref[...], preferred_element_type=f32)
    h = (_silu(gate) * up).astype(bf16)
    y = x + 0.5 * jnp.dot(h, wd_ref[...], preferred_element_type=f32)
    if has_final:
        y = _rms(y, fg_ref[...])
    o_ref[...] = y


def _ffn(x, layer, g, wg, wu, wd, mix=None, final_g=None):
    m = x.shape[0]
    tm = min(ROW_TILE, m)
    assert m % tm == 0
    row = lambda w: pl.BlockSpec((tm, w), lambda i: (i, 0))
    args, specs = [x], [row(D_MODEL)]
    if mix is not None:
        ya, yb, yc, wo = mix
        args += [ya, yb, yc, wo]
        specs += [row(D_POOL), pl.BlockSpec((2, tm, LANES), lambda i: (0, i, 0)), row(D_SSM), _layer_spec(wo, layer)]
    args += [g, wg, wu, wd]
    specs += [_layer_spec(g, layer), _layer_spec(wg, layer), _layer_spec(wu, layer), _layer_spec(wd, layer)]
    if final_g is not None:
        args.append(final_g)
        specs.append(_const_spec(final_g.shape))
    return pl.pallas_call(
        functools.partial(_ffn_kernel, mix is not None, final_g is not None),
        grid=(m // tm,),
        in_specs=specs,
        out_specs=row(D_MODEL),
        out_shape=jax.ShapeDtypeStruct((m, D_MODEL), f32),
        compiler_params=_params(1),
        name="ffn",
    )(*args)


def _inproj_kernel(tiles_per_seq, x_ref, g_ref, w_ref, pbd_ref, pscale_ref, cw_ref, cb_ref, dtb_ref, cos_ref, sin_ref,
                   ya_ref, q_ref, k_ref, v_ref, z_ref, u_ref, dt_ref, pst_ref, cst_ref, xa_buf, xbc_buf):
    tm = x_ref.shape[0]
    t = pl.program_id(0) % tiles_per_seq

    xn = _rms(x_ref[...], g_ref[...]).astype(bf16)
    proj = lambda a, b: jnp.dot(xn, w_ref[:, a:b], preferred_element_type=f32)

    @pl.when(t == 0)
    def _():
        xa_buf[0:POOL_HALO, :] = jnp.zeros((POOL_HALO, D_POOL), f32)
        xbc_buf[0:CONV_HALO, :] = jnp.zeros((CONV_HALO, D_CONV), f32)

    xbc = proj(OFF_XBC, OFF_XBC + D_CONV)
    pa = proj(0, OFF_Q)
    xbc_buf[CONV_HALO:CONV_HALO + tm, :] = xbc
    conv = cb_ref[...] + xbc * cw_ref[CONV_WIDTH - 1:CONV_WIDTH, :]
    for j in range(1, CONV_WIDTH):
        conv = conv + xbc_buf[CONV_HALO - j:CONV_HALO - j + tm, :] * cw_ref[CONV_WIDTH - 1 - j:CONV_WIDTH - j, :]
    u_ref[...] = _silu(conv)
    cst_ref[0] = xbc[tm - CONV_HALO:tm, :]
    xbc_buf[0:CONV_HALO, :] = xbc[tm - CONV_HALO:tm, :]

    qkv = proj(OFF_Q, OFF_Z)
    xa = pa[:, 0:D_POOL]
    dt_ref[...] = _softplus(pa[:, OFF_DT:OFF_DT + LANES] + dtb_ref[...])
    xa_buf[POOL_HALO:POOL_HALO + tm, :] = xa
    ext = xa_buf[...]
    s2 = ext + pltpu.roll(ext, 1, 0)
    s4 = s2 + pltpu.roll(s2, 2, 0)
    s8 = s4 + pltpu.roll(s4, 4, 0)
    s16 = s8 + pltpu.roll(s8, 8, 0)
    sums = _pool_select(s2, s4, s8, s16)[POOL_HALO:POOL_HALO + tm, :]
    pos = t * tm + lax.broadcasted_iota(jnp.int32, (tm, D_POOL), 0)
    cnt = jnp.minimum(_pool_window((tm, D_POOL)), pos + 1).astype(f32)
    dmean = sums / cnt - xa
    ya = jnp.dot(dmean.astype(bf16), pbd_ref[...], preferred_element_type=f32) * pscale_ref[...]
    ya_ref[...] = ya.astype(ya_ref.dtype)
    pst_ref[0] = xa[tm - POOL_HALO:tm, :]
    xa_buf[0:POOL_HALO, :] = xa[tm - POOL_HALO:tm, :]

    zz = proj(OFF_Z, OFF_XBC)
    cos = cos_ref[...]
    sin = sin_ref[...]
    for s in range(2):
        q_ref[s] = _rope_slab(qkv[:, s * LANES:(s + 1) * LANES], cos, sin)
        k_ref[s] = _rope_slab(qkv[:, D_ATT + s * LANES:D_ATT + (s + 1) * LANES], cos, sin)
        v_ref[s] = qkv[:, 2 * D_ATT + s * LANES:2 * D_ATT + (s + 1) * LANES]

    z_ref[...] = zz


def _inproj_prompt(x, layer, batch, seq, g, w, pbd, pscale, cw, cb, dtb, cos, sin):
    m = x.shape[0]
    tm = ROW_TILE
    tps = seq // tm
    row = lambda wd: pl.BlockSpec((tm, wd), lambda i: (i, 0))
    slab = pl.BlockSpec((2, tm, LANES), lambda i: (0, i, 0))
    tab = pl.BlockSpec((tm, LANES), lambda i: (i % tps, 0))
    out_shape = (
        jax.ShapeDtypeStruct((m, D_POOL), bf16),
        jax.ShapeDtypeStruct((2, m, LANES), f32),
        jax.ShapeDtypeStruct((2, m, LANES), f32),
        jax.ShapeDtypeStruct((2, m, LANES), f32),
        jax.ShapeDtypeStruct((m, D_SSM), f32),
        jax.ShapeDtypeStruct((m, D_CONV), f32),
        jax.ShapeDtypeStruct((m, LANES), f32),
        jax.ShapeDtypeStruct((batch, POOL_HALO, D_POOL), f32),
        jax.ShapeDtypeStruct((batch, CONV_HALO, D_CONV), f32),
    )
    out_specs = (
        row(D_POOL), slab, slab, slab, row(D_SSM), row(D_CONV), row(LANES),
        pl.BlockSpec((1, POOL_HALO, D_POOL), lambda i: (i // tps, 0, 0)),
        pl.BlockSpec((1, CONV_HALO, D_CONV), lambda i: (i // tps, 0, 0)),
    )
    in_specs = [row(D_MODEL), _layer_spec(g, layer), _layer_spec(w, layer), _const_spec(pbd.shape), _const_spec(pscale.shape),
                _const_spec(cw.shape), _const_spec(cb.shape), _const_spec(dtb.shape), tab, tab]
    return pl.pallas_call(
        functools.partial(_inproj_kernel, tps),
        grid=(m // tm,),
        in_specs=in_specs,
        out_specs=out_specs,
        out_shape=out_shape,
        scratch_shapes=[pltpu.VMEM((POOL_HALO + tm, D_POOL), f32), pltpu.VMEM((CONV_HALO + tm, D_CONV), f32)],
        compiler_params=_params(1),
        name="inproj",
    )(x, g, w, pbd, pscale, cw, cb, dtb, cos, sin)


def _attn_kernel(q_ref, kp_ref, kc_ref, vp_ref, vc_ref, o_ref, kbuf, vbuf, acc, mrun, lrun):
    t = pl.program_id(2)
    blk = N_BACK
    kbuf[0:ATT_TILE, :] = kp_ref[0, 0]
    kbuf[ATT_TILE:2 * ATT_TILE, :] = kc_ref[0, 0]
    vbuf[0:ATT_TILE, :] = vp_ref[0, 0]
    vbuf[ATT_TILE:2 * ATT_TILE, :] = vc_ref[0, 0]

    qi = lax.broadcasted_iota(jnp.int32, (2 * blk, 2 * blk), 0) % blk
    kj = lax.broadcasted_iota(jnp.int32, (2 * blk, 2 * blk), 1)
    dist = blk + qi - kj
    band = (dist >= 0) & (dist <= N_BACK)
    bias_full = jnp.where(band, 0.0, -jnp.inf).astype(f32)
    bias_cur = jnp.where(band & (kj >= blk), 0.0, -jnp.inf).astype(f32)
    head0 = lax.broadcasted_iota(jnp.int32, (blk, LANES), 1) < HEAD_DIM
    head0_k = lax.broadcasted_iota(jnp.int32, (2 * blk, LANES), 1) < HEAD_DIM
    ones_blk = jnp.concatenate([jnp.where(head0_k, 1.0, 0.0), jnp.where(head0_k, 0.0, 1.0)], axis=0).astype(bf16)
    scale = math.log2(math.e) / math.sqrt(HEAD_DIM)

    for d in DILATIONS:
        first = d == DILATIONS[0]
        last = d == DILATIONS[-1]
        span = blk * d
        n_units = ATT_TILE // blk

        def group(gi, carry, d=d, first=first, last=last, span=span):
            rows, scores, vcats = [], [], []
            for j in range(ATT_UNROLL):
                u = gi * ATT_UNROLL + j
                sb = u // d
                r = u % d
                qstart = sb * span + r
                kstart = ATT_TILE + qstart - span
                if d > 1:
                    qrows = pl.ds(qstart, blk, stride=d)
                    krows = pl.ds(kstart, 2 * blk, stride=d)
                else:
                    qrows = pl.ds(pl.multiple_of(qstart, blk), blk)
                    krows = pl.ds(pl.multiple_of(kstart, blk), 2 * blk)
                rows.append(qrows)
                qb = (q_ref[0, 0, qrows, :] * scale).astype(bf16)
                kb = kbuf[krows, :].astype(bf16)
                vb = vbuf[krows, :].astype(bf16)
                bias = jnp.where(jnp.logical_or(t > 0, sb > 0), bias_full, bias_cur)
                qcat = jnp.concatenate([jnp.where(head0, qb, jnp.zeros_like(qb)), jnp.where(head0, jnp.zeros_like(qb), qb)],
                                       axis=0)
                scores.append(lax.dot_general(qcat, kb, (((1,), (1,)), ((), ())), preferred_element_type=f32) + bias)
                vcat = jnp.concatenate([jnp.where(head0_k, vb, jnp.zeros_like(vb)),
                                        jnp.where(head0_k, jnp.zeros_like(vb), vb)], axis=0)
                vcats.append(jnp.concatenate([vcat, ones_blk], axis=1))
            stats = []
            for j in range(ATT_UNROLL):
                m2 = jnp.max(scores[j], axis=1, keepdims=True)
                pb = jnp.exp2(scores[j] - m2).astype(bf16)
                pcat = jnp.concatenate([pb[0:blk], pb[blk:2 * blk]], axis=1)
                stats.append((pcat, jnp.where(head0, m2[0:blk], m2[blk:2 * blk])))
            pvs = [jnp.dot(stats[j][0], vcats[j], preferred_element_type=f32) for j in range(ATT_UNROLL)]
            for j in range(ATT_UNROLL):
                qrows, pv, l_e, m_e = rows[j], pvs[j][:, 0:LANES], pvs[j][:, LANES:2 * LANES], stats[j][1]
                if first:
                    acc[qrows, :] = pv
                    mrun[qrows, :] = m_e
                    lrun[qrows, :] = l_e
                else:
                    m_old = mrun[qrows, :]
                    m_new = jnp.maximum(m_old, m_e)
                    a = jnp.exp2(m_old - m_new)
                    b = jnp.exp2(m_e - m_new)
                    acc_new = acc[qrows, :] * a + pv * b
                    l_new = lrun[qrows, :] * a + l_e * b
                    if last:
                        o_ref[0, 0, qrows, :] = (acc_new / l_new).astype(o_ref.dtype)
                    else:
                        acc[qrows, :] = acc_new
                        lrun[qrows, :] = l_new
                        mrun[qrows, :] = m_new
            return carry

        lax.fori_loop(0, n_units // ATT_UNROLL, group, 0)


def _attn_prompt(q, k, v, batch, seq):
    nt = seq // ATT_TILE
    q4 = q.reshape(2, batch, seq, LANES)
    k4 = k.reshape(2, batch, seq, LANES)
    v4 = v.reshape(2, batch, seq, LANES)
    cur = pl.BlockSpec((1, 1, ATT_TILE, LANES), lambda b, s, t: (s, b, t, 0))
    prev = pl.BlockSpec((1, 1, ATT_TILE, LANES), lambda b, s, t: (s, b, jnp.maximum(t - 1, 0), 0))
    out = pl.pallas_call(
        _attn_kernel,
        grid=(batch, 2, nt),
        in_specs=[cur, prev, cur, prev, cur],
        out_specs=cur,
        out_shape=jax.ShapeDtypeStruct((2, batch, seq, LANES), bf16),
        scratch_shapes=[pltpu.VMEM((2 * ATT_TILE, LANES), f32), pltpu.VMEM((2 * ATT_TILE, LANES), f32),
                        pltpu.VMEM((ATT_TILE, LANES), f32), pltpu.VMEM((ATT_TILE, LANES), f32),
                        pltpu.VMEM((ATT_TILE, LANES), f32)],
        compiler_params=_params(3),
        name="attn",
    )(q4, k4, k4, v4, v4)
    return out.reshape(2, batch * seq, LANES)


def _split3(x):
    hi = x.astype(bf16)
    rest = x - hi.astype(f32)
    mid = rest.astype(bf16)
    lo = (rest - mid.astype(f32)).astype(bf16)
    return hi, mid, lo


def _ssd_kernel(u_ref, dt_ref, z_ref, a2_ref, e3_ref, dskip_ref, norm_ref, y_ref, state_ref, st):
    @pl.when(pl.program_id(1) == 0)
    def _():
        st[...] = jnp.zeros_like(st)

    q = SSM_CHUNK
    nc = u_ref.shape[0] // q
    gl = D_SSM // SSM_GROUPS
    hpg = N_SSM_HEADS // SSM_GROUPS
    li = lax.broadcasted_iota(jnp.int32, (q, q), 0)
    si = lax.broadcasted_iota(jnp.int32, (q, q), 1)
    causal = li >= si
    tril = causal.astype(bf16)
    triu = (li <= si).astype(bf16)
    lo = lax.broadcasted_iota(jnp.int32, (q, LANES), 1) < SSM_HEAD_DIM
    rows = lambda c: slice(c * q, (c + 1) * q)
    b_of = lambda u, g: u[:, D_SSM + g * SSM_STATE:D_SSM + (g + 1) * SSM_STATE]
    c_of = lambda u, g: u[:, D_SSM + (SSM_GROUPS + g) * SSM_STATE:D_SSM + (SSM_GROUPS + g + 1) * SSM_STATE]

    pre = []
    for c in range(nc):
        dtp = dt_ref[rows(c), :]
        dta = dtp * a2_ref[...]
        cum3 = jnp.dot(tril, jnp.concatenate(_split3(dta), axis=1), preferred_element_type=f32)
        cum = cum3[:, 0:LANES] + cum3[:, LANES:2 * LANES] + cum3[:, 2 * LANES:3 * LANES]
        both = jnp.dot(jnp.concatenate(_split3(jnp.concatenate([dtp, cum], axis=0)), axis=1), e3_ref[...],
                       preferred_element_type=f32)
        acs_row = _dot_f32_by_01(dta.T[0:SUBLANES], triu)
        pre.append((both[0:q], both[q:2 * q], acs_row))

    ydiag, xdts = [], []
    for c in range(nc):
        u = u_ref[rows(c), :]
        dt_exp, acs_exp, acs_row = pre[c]
        xdt = u[:, 0:D_SSM] * dt_exp
        xdt_b = xdt.astype(bf16)
        parts = []
        for g in range(SSM_GROUPS):
            cb = lax.dot_general(c_of(u, g).astype(bf16), b_of(u, g).astype(bf16), (((1,), (1,)), ((), ())),
                                 preferred_element_type=f32)
            for pair in range(hpg // 2):
                mats = []
                for hh in range(2):
                    h = g * hpg + pair * 2 + hh
                    col = acs_exp[:, h * SSM_HEAD_DIM:h * SSM_HEAD_DIM + 1]
                    diff = jnp.where(causal, col - acs_row[h:h + 1, :], -jnp.inf)
                    mats.append((cb * jnp.exp2(diff)).astype(bf16))
                c0 = (g * hpg + pair * 2) * SSM_HEAD_DIM
                xp = xdt_b[:, c0:c0 + LANES]
                rhs = jnp.concatenate([jnp.where(lo, xp, jnp.zeros_like(xp)), jnp.where(lo, jnp.zeros_like(xp), xp)], axis=0)
                parts.append(jnp.dot(jnp.concatenate(mats, axis=1), rhs, preferred_element_type=f32))
        ydiag.append(jnp.concatenate(parts, axis=1))
        xdts.append(xdt)

    yoff = []
    for c in range(nc):
        u = u_ref[rows(c), :]
        acs_exp = pre[c][1]
        acs_last = acs_exp[q - 1:q, :]
        to_end = jnp.exp2(acs_last - acs_exp)
        chunk_decay = jnp.exp2(acs_last)
        offs = []
        for g in range(SSM_GROUPS):
            gs = slice(g * gl, (g + 1) * gl)
            st_g = st[:, gs]
            offs.append(jnp.dot(c_of(u, g).astype(bf16), st_g.astype(bf16), preferred_element_type=f32))
            xw = (xdts[c][:, gs] * to_end[:, gs]).astype(bf16)
            st[:, gs] = st_g * chunk_decay[:, gs] + jnp.dot(b_of(u, g).T.astype(bf16), xw, preferred_element_type=f32)
        yoff.append(jnp.concatenate(offs, axis=1) * jnp.exp2(acs_exp))

    for c in range(nc):
        xs = u_ref[rows(c), 0:D_SSM]
        y = (ydiag[c] + yoff[c] + dskip_ref[...] * xs) * _silu(z_ref[rows(c), :])
        outs = []
        for g in range(SSM_GROUPS):
            yg = y[:, g * gl:(g + 1) * gl]
            outs.append(yg * lax.rsqrt(jnp.mean(yg * yg, -1, keepdims=True) + RMS_EPS))
        y_ref[rows(c), :] = (jnp.concatenate(outs, axis=1) * norm_ref[...]).astype(y_ref.dtype)
    state_ref[0] = st[...]


def _ssd_prompt(u, dt, z, batch, seq, a2_pad, e3_mat, dskip, norm):
    tq = ROW_TILE
    tps = seq // tq
    row = lambda wd: pl.BlockSpec((tq, wd), lambda b, t: (b * tps + t, 0))
    consts = [a2_pad, e3_mat, dskip, norm]
    return pl.pallas_call(
        _ssd_kernel,
        grid=(batch, tps),
        in_specs=[row(D_CONV), row(LANES), row(D_SSM)] + [_const_spec(c.shape) for c in consts],
        out_specs=(row(D_SSM), pl.BlockSpec((1, SSM_STATE, D_SSM), lambda b, t: (b, 0, 0))),
        out_shape=(jax.ShapeDtypeStruct((batch * seq, D_SSM), bf16), jax.ShapeDtypeStruct((batch, SSM_STATE, D_SSM), f32)),
        scratch_shapes=[pltpu.VMEM((SSM_STATE, D_SSM), f32)],
        compiler_params=_params(2),
        name="ssd",
    )(u, dt, z, *consts)


def _sample_pre_kernel(x_ref, g_ref, w_ref, cpool_ref, pbd_ref, pscale_ref, cconv_ref, cw_ref, cb_ref, dtb_ref,
                       cos_ref, sin_ref, ya_ref, q_ref, k_ref, v_ref, z_ref, u_ref, dt_ref, pnew_ref, cnew_ref):
    xn = _rms(x_ref[...], g_ref[...]).astype(bf16)
    proj = jnp.dot(xn, w_ref[...], preferred_element_type=f32)

    xa = proj[:, 0:D_POOL]
    back = lambda i: cpool_ref[POOL_BUF - i]
    s2 = xa + back(1)
    s4 = s2 + back(2) + back(3)
    s8 = s4 + back(4) + back(5) + back(6) + back(7)
    s16 = s8
    for i in range(8, 16):
        s16 = s16 + back(i)
    cnt = jnp.minimum(_pool_window(xa.shape), PAST_LEN + 1).astype(f32)
    dmean = _pool_select(s2, s4, s8, s16) / cnt - xa
    ya = jnp.dot(dmean.astype(bf16), pbd_ref[...], preferred_element_type=f32) * pscale_ref[...]
    ya_ref[...] = ya.astype(ya_ref.dtype)
    for i in range(POOL_BUF - 1):
        pnew_ref[i] = cpool_ref[i + 1]
    pnew_ref[POOL_BUF - 1] = xa

    cos = cos_ref[...]
    sin = sin_ref[...]
    for s in range(2):
        sl = slice(s * LANES, (s + 1) * LANES)
        q_ref[:, sl] = _rope_slab(proj[:, OFF_Q + s * LANES:OFF_Q + (s + 1) * LANES], cos, sin)
        k_ref[:, sl] = _rope_slab(proj[:, OFF_K + s * LANES:OFF_K + (s + 1) * LANES], cos, sin)
    v_ref[...] = proj[:, OFF_V:OFF_V + D_ATT]
    z_ref[...] = proj[:, OFF_Z:OFF_Z + D_SSM]

    xbc = proj[:, OFF_XBC:OFF_XBC + D_CONV]
    conv = cb_ref[...] + xbc * cw_ref[CONV_WIDTH - 1:CONV_WIDTH, :]
    for j in range(1, CONV_WIDTH):
        conv = conv + cconv_ref[CONV_WIDTH - 1 - j] * cw_ref[CONV_WIDTH - 1 - j:CONV_WIDTH - j, :]
    u_ref[...] = _silu(conv)
    for j in range(CONV_WIDTH - 2):
        cnew_ref[j] = cconv_ref[j + 1]
    cnew_ref[CONV_WIDTH - 2] = xbc

    dt_ref[...] = _softplus(proj[:, OFF_DT:OFF_DT + LANES] + dtb_ref[...])


def _sample_pre(x, layer, g, w, cpool, pbd, pscale, cconv, cw, cb, dtb, cos, sin):
    n = x.shape[0]
    args = [x, g, w, cpool, pbd, pscale, cconv, cw, cb, dtb, cos, sin]
    out_shape = (
        jax.ShapeDtypeStruct((n, D_POOL), bf16),
        jax.ShapeDtypeStruct((n, D_ATT), f32),
        jax.ShapeDtypeStruct((n, D_ATT), f32),
        jax.ShapeDtypeStruct((n, D_ATT), f32),
        jax.ShapeDtypeStruct((n, D_SSM), f32),
        jax.ShapeDtypeStruct((n, D_CONV), f32),
        jax.ShapeDtypeStruct((n, LANES), f32),
        jax.ShapeDtypeStruct((POOL_BUF, n, D_POOL), f32),
        jax.ShapeDtypeStruct((CONV_WIDTH - 1, n, D_CONV), f32),
    )
    full = lambda s: pl.BlockSpec(s.shape, lambda i, nd=len(s.shape): (0,) * nd)
    in_specs = [full(a) for a in args]
    in_specs[1] = _layer_spec(g, layer)
    in_specs[2] = _layer_spec(w, layer)
    return pl.pallas_call(
        _sample_pre_kernel,
        grid=(1,),
        in_specs=in_specs,
        out_specs=tuple(full(s) for s in out_shape),
        out_shape=out_shape,
        compiler_params=_params(1),
        name="sample_pre",
    )(*args)


def _sample_ssd_kernel(u_ref, dt_ref, z_ref, st_ref, aexp_ref, e_ref, dskip_ref, norm_ref, y_ref, stnew_ref):
    n = u_ref.shape[0]
    gl = D_SSM // SSM_GROUPS
    u = u_ref[...]
    xs = u[:, 0:D_SSM]
    dt_exp = _dot_f32_by_01(dt_ref[...], e_ref[...])
    pad = jnp.zeros((LANES - n, D_SSM), f32)
    dec_t = jnp.concatenate([jnp.exp(dt_exp * aexp_ref[...]), pad], axis=0).T
    dtx_t = jnp.concatenate([dt_exp * xs, pad], axis=0).T
    lane = lax.broadcasted_iota(jnp.int32, (D_SSM, LANES), 1)
    y_t = jnp.zeros((D_SSM, LANES), f32)
    for i in range(n):
        dec = dec_t[:, i:i + 1]
        dtx = dtx_t[:, i:i + 1]
        ycols = []
        for g in range(SSM_GROUPS):
            rows = slice(g * gl, (g + 1) * gl)
            b_row = u[i:i + 1, D_SSM + g * SSM_STATE:D_SSM + (g + 1) * SSM_STATE]
            c_row = u[i:i + 1, D_SSM + (SSM_GROUPS + g) * SSM_STATE:D_SSM + (SSM_GROUPS + g + 1) * SSM_STATE]
            h_new = dec[rows] * st_ref[i, rows, :] + dtx[rows] * b_row
            stnew_ref[i, rows, :] = h_new
            ycols.append(jnp.sum(h_new * c_row, axis=1, keepdims=True))
        y_t = jnp.where(lane == i, jnp.concatenate(ycols, axis=0), y_t)
    y = y_t.T[0:n]
    y = (y + dskip_ref[...] * xs) * _silu(z_ref[...])
    outs = []
    for g in range(SSM_GROUPS):
        yg = y[:, g * gl:(g + 1) * gl]
        outs.append(yg * lax.rsqrt(jnp.mean(yg * yg, -1, keepdims=True) + RMS_EPS))
    y_ref[...] = (jnp.concatenate(outs, axis=1) * norm_ref[...]).astype(y_ref.dtype)


def _sample_ssd(u, dt, z, state, a_exp, e_mat, dskip, norm):
    n = u.shape[0]
    args = [u, dt, z, state, a_exp, e_mat, dskip, norm]
    out_shape = (jax.ShapeDtypeStruct((n, D_SSM), bf16), jax.ShapeDtypeStruct(state.shape, f32))
    full = lambda s: pl.BlockSpec(s.shape, lambda i, nd=len(s.shape): (0,) * nd)
    return pl.pallas_call(
        _sample_ssd_kernel,
        grid=(1,),
        in_specs=[full(a) for a in args],
        out_specs=tuple(full(s) for s in out_shape),
        out_shape=out_shape,
        compiler_params=_params(1),
        name="sample_ssd",
    )(*args)


def _sample_attn_kernel(has_prev, q_ref, kn_ref, vn_ref, knc_ref, vnc_ref, kc_ref, vc_ref, *rest):
    if has_prev:
        rest = rest[2:]
    y_ref, ks_ref, vs_ref = rest
    wb = kc_ref.shape[3]
    kc = kc_ref[0, 0]
    vc = vc_ref[0, 0]
    q = q_ref[0]
    kn = kn_ref[0]
    vn = vn_ref[0]
    scale = 1.0 / math.sqrt(HEAD_DIM)

    own = (lax.broadcasted_iota(jnp.int32, (SUBLANES, D_ATT), 1) // HEAD_DIM) == lax.broadcasted_iota(
        jnp.int32, (SUBLANES, D_ATT), 0)
    qbd = jnp.where(own, jnp.broadcast_to(q, (SUBLANES, D_ATT)), 0.0)
    s_all = jnp.dot(qbd.astype(bf16), kc.astype(bf16), preferred_element_type=f32) * scale
    s_new = jnp.sum(jnp.where(own, jnp.broadcast_to(q * kn, (SUBLANES, D_ATT)), 0.0), axis=1, keepdims=True) * scale
    dist = wb - lax.broadcasted_iota(jnp.int32, (SUBLANES, wb), 1)
    ms, ps, ls, es = [], [], [], []
    for d in DILATIONS:
        valid = ((dist % d) == 0) & (dist <= N_BACK * d)
        s = jnp.where(valid, s_all, -jnp.inf)
        m = jnp.maximum(jnp.max(s, axis=1, keepdims=True), s_new)
        p = jnp.exp(s - m)
        e_new = jnp.exp(s_new - m)
        ms.append(m)
        ps.append(p)
        es.append(e_new)
        ls.append(jnp.sum(p, axis=1, keepdims=True) + e_new)
    m_all = jnp.maximum(jnp.maximum(ms[0], ms[1]), ms[2])
    p_all = jnp.zeros_like(ps[0])
    w_new = jnp.zeros_like(s_new)
    l_all = jnp.zeros_like(s_new)
    for m, p, e_new, l in zip(ms, ps, es, ls):
        c = jnp.exp(m - m_all)
        p_all = p_all + c * p
        w_new = w_new + c * e_new
        l_all = l_all + c * l
    o_full = lax.dot_general(p_all.astype(bf16), vc.astype(bf16), (((1,), (1,)), ((), ())), preferred_element_type=f32)
    to_row = lambda t: jnp.sum(jnp.where(own, t, 0.0), axis=0, keepdims=True)
    o_row = to_row(o_full) + to_row(jnp.broadcast_to(w_new, (SUBLANES, D_ATT))) * vn
    y_ref[0] = o_row / to_row(jnp.broadcast_to(l_all, (SUBLANES, D_ATT)))

    newest = lax.broadcasted_iota(jnp.int32, (D_ATT, wb), 1) == wb - 1
    ks_ref[0, 0] = jnp.where(newest, knc_ref[0], pltpu.roll(kc, wb - 1, 1))
    vs_ref[0, 0] = jnp.where(newest, vnc_ref[0], pltpu.roll(vc, wb - 1, 1))


def _sample_attn(layer, q, kn, vn, cache_k, cache_v, prev):
    depth, n, da, wb = cache_k.shape
    rowv = pl.BlockSpec((1, 1, da), lambda i: (i, 0, 0))
    colv = pl.BlockSpec((1, da, 1), lambda i: (i, 0, 0))
    big = pl.BlockSpec((1, 1, da, wb), lambda i: (layer, i, 0, 0))
    in_specs = [rowv, rowv, rowv, colv, colv, big, big]
    args = [q.reshape(n, 1, da), kn.reshape(n, 1, da), vn.reshape(n, 1, da), kn.reshape(n, da, 1), vn.reshape(n, da, 1),
            cache_k, cache_v]
    aliases = {}
    if prev is not None:
        in_specs += [pl.BlockSpec(memory_space=pl.ANY)] * 2
        aliases = {len(args): 1, len(args) + 1: 2}
        args += list(prev)
    return pl.pallas_call(
        functools.partial(_sample_attn_kernel, prev is not None),
        grid=(n,),
        in_specs=in_specs,
        out_specs=(rowv, big, big),
        out_shape=(jax.ShapeDtypeStruct((n, 1, da), f32), jax.ShapeDtypeStruct(cache_k.shape, f32),
                   jax.ShapeDtypeStruct(cache_v.shape, f32)),
        input_output_aliases=aliases,
        compiler_params=_params(1),
        name="sample_attn",
    )(*args)


def _rope_tables(pos):
    half = HEAD_DIM // 2
    inv = ROPE_THETA ** (-jnp.arange(half, dtype=f32) / half)
    ang = pos.astype(f32)[:, None] * inv[None]
    cos = jnp.tile(jnp.cos(ang), (1, LANES // half))
    sin = jnp.sin(ang)
    sin_signed = jnp.tile(jnp.concatenate([-sin, sin], axis=1), (1, LANES // HEAD_DIM))
    return cos, sin_signed


def kernel(x_prompt, x_sample, cache_pool, cache_k, cache_v, state_conv, state_ssm, ffn1_norm, ffn1_w_gate, ffn1_w_up,
           ffn1_w_down, mix_norm, w_in, pool_w, pool_scale, conv_w, conv_b, dt_bias, a_log, d_skip, ssm_norm, w_out,
           ffn2_norm, ffn2_w_gate, ffn2_w_up, ffn2_w_down, final_norm):
    batch, seq, _ = x_prompt.shape
    n_dec = x_sample.shape[0]
    depth = w_in.shape[0]
    wb = cache_k.shape[2]
    assert seq % ATT_TILE == 0 and seq % ROW_TILE == 0 and x_sample.shape[1] == 1 and wb == ATT_WIN

    yp = x_prompt.reshape(batch * seq, D_MODEL)
    ys = x_sample.reshape(n_dec, D_MODEL)
    cos_p, sin_p = _rope_tables(jnp.arange(seq))
    cos_s, sin_s = _rope_tables(jnp.full((1,), PAST_LEN))
    e_mat = (jnp.arange(LANES)[:, None] == (jnp.arange(D_SSM)[None, :] // SSM_HEAD_DIM)).astype(bf16)
    e3_mat = jnp.tile(e_mat, (3, 1))
    ck_t = jnp.transpose(cache_k, (0, 1, 3, 4, 2)).reshape(depth, n_dec, D_ATT, wb)
    cv_t = jnp.transpose(cache_v, (0, 1, 3, 4, 2)).reshape(depth, n_dec, D_ATT, wb)
    row2 = lambda a: a.reshape(1, -1)
    row3 = lambda a: a.reshape(depth, 1, -1)

    f1 = (row3(ffn1_norm), ffn1_w_gate.astype(bf16), ffn1_w_up.astype(bf16), ffn1_w_down.astype(bf16))
    f2 = (row3(ffn2_norm), ffn2_w_gate.astype(bf16), ffn2_w_up.astype(bf16), ffn2_w_down.astype(bf16))
    w_all = jnp.concatenate([w_in[:, :, 0:D_POOL],
                             jnp.pad(w_in[:, :, D_IN_PROJ - N_SSM_HEADS:], ((0, 0), (0, 0), (0, LANES - N_SSM_HEADS))),
                             w_in[:, :, D_POOL:D_IN_PROJ - N_SSM_HEADS]], axis=2).astype(bf16)
    wo_all = w_out.astype(bf16)
    g_mix = row3(mix_norm)

    st_p = [[] for _ in range(5)]
    st_s = [[] for _ in range(3)]
    kv_s = None
    for i in range(depth):
        last = i == depth - 1
        pbd = jax.scipy.linalg.block_diag(*[pool_w[i, g] for g in range(len(POOL_WINDOWS))]).astype(bf16)
        pscale = row2(pool_scale[i])
        cb = row2(conv_b[i])
        dtb = row2(jnp.pad(dt_bias[i], (0, LANES - N_SSM_HEADS)))
        a_neg = -jnp.exp(a_log[i].astype(f32))
        a2_pad = row2(jnp.pad(a_neg * math.log2(math.e), (0, LANES - N_SSM_HEADS)))
        a_exp = row2(jnp.repeat(a_neg, SSM_HEAD_DIM))
        dskip = row2(jnp.repeat(d_skip[i], SSM_HEAD_DIM))
        norm = row2(ssm_norm[i])
        fin = row2(final_norm) if last else None

        yp = _ffn(yp, i, *f1)
        ya, q, k, v, z, u, dt, pst, cst = _inproj_prompt(yp, i, batch, seq, g_mix, w_all, pbd, pscale, conv_w[i], cb, dtb,
                                                         cos_p, sin_p)
        yb = _attn_prompt(q, k, v, batch, seq)
        yc, sst = _ssd_prompt(u, dt, z, batch, seq, a2_pad, e3_mat, dskip, norm)
        yp = _ffn(yp, i, *f2, mix=(ya, yb, yc, wo_all), final_g=fin)

        def window(t):
            t = t.reshape(2, batch, seq, LANES)[:, :, seq - wb:]
            return jnp.transpose(t, (1, 2, 0, 3)).reshape(batch, wb, N_ATT_HEADS, HEAD_DIM)

        st_p[0].append(pst[:, POOL_HALO - POOL_BUF:])
        st_p[1].append(window(k))
        st_p[2].append(window(v))
        st_p[3].append(cst[:, CONV_HALO - (CONV_WIDTH - 1):])
        st_p[4].append(jnp.transpose(sst.reshape(batch, SSM_STATE, N_SSM_HEADS, SSM_HEAD_DIM), (0, 2, 3, 1)))

        ys = _ffn(ys, i, *f1)
        cpool = jnp.transpose(cache_pool[i], (1, 0, 2))
        cconv = jnp.transpose(state_conv[i], (1, 0, 2))
        ya_s, q_s, k_s, v_s, z_s, u_s, dt_s, pnew, cnew = _sample_pre(ys, i, g_mix, w_all, cpool, pbd, pscale, cconv,
                                                                    conv_w[i], cb, dtb, cos_s, sin_s)
        yb_s, ks_buf, vs_buf = _sample_attn(i, q_s, k_s, v_s, ck_t, cv_t, kv_s)
        kv_s = (ks_buf, vs_buf)
        yc_s, ssm_new = _sample_ssd(u_s, dt_s, z_s, state_ssm[i].reshape(n_dec, D_SSM, SSM_STATE), a_exp, e_mat, dskip, norm)
        yb_slab = jnp.transpose(yb_s.reshape(n_dec, 2, LANES), (1, 0, 2)).astype(bf16)
        ys = _ffn(ys, i, *f2, mix=(ya_s, yb_slab, yc_s, wo_all), final_g=fin)
        st_s[0].append(jnp.transpose(pnew, (1, 0, 2)))
        st_s[1].append(jnp.transpose(cnew, (1, 0, 2)))
        st_s[2].append(ssm_new.reshape(n_dec, N_SSM_HEADS, SSM_HEAD_DIM, SSM_STATE))

    y_prompt = yp.reshape(batch, seq, D_MODEL)
    y_sample = ys.reshape(n_dec, 1, D_MODEL)
    pool_p, k_p, v_p, conv_p, ssm_p = (jnp.stack(s, 0) for s in st_p)
    pool_s, conv_s, ssm_s = (jnp.stack(s, 0) for s in st_s)
    from_t = lambda t: jnp.transpose(t.reshape(depth, n_dec, N_ATT_HEADS, HEAD_DIM, wb), (0, 1, 4, 2, 3))
    return (y_prompt, y_sample, pool_p, pool_s, k_p, from_t(kv_s[0]), v_p, from_t(kv_s[1]), conv_p, conv_s, ssm_p, ssm_s)
```

```python
import functools
import math

import jax
import jax.numpy as jnp
from jax import lax
from jax.experimental import pallas as pl
from jax.experimental.pallas import tpu as pltpu

f32 = jnp.float32
bf16 = jnp.bfloat16

D_MODEL = 1024
PAST_LEN = 16384
POOL_WINDOWS = (2, 4, 8, 16)
D_POOL = 256
POOL_GROUP = D_POOL // len(POOL_WINDOWS)
POOL_BUF = max(POOL_WINDOWS) - 1
HEAD_DIM = 64
N_ATT_HEADS = 4
D_ATT = N_ATT_HEADS * HEAD_DIM
DILATIONS = (16, 4, 1)
N_BACK = 128
ATT_WIN = 2048
ROPE_THETA = 10000.0
D_SSM = 512
SSM_HEAD_DIM = 64
N_SSM_HEADS = 8
SSM_STATE = 128
SSM_GROUPS = 2
CONV_WIDTH = 4
SSM_CHUNK = 128
D_CONV = D_SSM + 2 * SSM_GROUPS * SSM_STATE
D_IN_PROJ = D_POOL + 3 * D_ATT + D_SSM + D_CONV + N_SSM_HEADS
D_FF = 2816
RMS_EPS = 1e-6

LANES = 128
SUBLANES = 8
VMEM_LIMIT = 56 * 1024 * 1024

OFF_DT = D_POOL
OFF_Q = OFF_DT + LANES
OFF_K = OFF_Q + D_ATT
OFF_V = OFF_K + D_ATT
OFF_Z = OFF_V + D_ATT
OFF_XBC = OFF_Z + D_SSM
N_PROJ = OFF_XBC + D_CONV

ROW_TILE = 512
SSD_TILE = 1024
ATT_TILE = 2048
ATT_UNROLL = 8
POOL_HALO = 16
CONV_HALO = 8


def _params(n_axes, flags=None):
    return pltpu.CompilerParams(dimension_semantics=("arbitrary",) * n_axes, vmem_limit_bytes=VMEM_LIMIT, flags=flags)


def _const_spec(shape):
    nd = len(shape)
    return pl.BlockSpec(shape, lambda *_: (0,) * nd, pipeline_mode=pl.Buffered(1))


def _layer_spec(arr, layer):
    nd = arr.ndim - 1
    return pl.BlockSpec((None,) + arr.shape[1:], lambda *_: (layer,) + (0,) * nd, pipeline_mode=pl.Buffered(1))


def _rms(x, g):
    return x * lax.rsqrt(jnp.mean(x * x, -1, keepdims=True) + RMS_EPS) * g


def _dot_f32_by_01(x, m01, lhs_is_01=False):
    hi = x.astype(bf16)
    rest = x - hi.astype(f32)
    mid = rest.astype(bf16)
    lo = (rest - mid.astype(f32)).astype(bf16)
    out = None
    for part in (hi, mid, lo):
        if lhs_is_01:
            t = jnp.dot(m01, part, preferred_element_type=f32)
        else:
            t = jnp.dot(part, m01, preferred_element_type=f32)
        out = t if out is None else out + t
    return out


def _silu(x):
    return x * jax.nn.sigmoid(x)


def _softplus(x):
    return jnp.maximum(x, 0.0) + jnp.log1p(jnp.exp(-jnp.abs(x)))


def _rope_slab(x, cos, sin_signed):
    lane = lax.broadcasted_iota(jnp.int32, x.shape, 1)
    first_half = (lane % HEAD_DIM) < (HEAD_DIM // 2)
    partner = jnp.where(first_half, pltpu.roll(x, LANES - HEAD_DIM // 2, 1), pltpu.roll(x, HEAD_DIM // 2, 1))
    return x * cos + partner * sin_signed


def _pool_select(s2, s4, s8, s16):
    lane = lax.broadcasted_iota(jnp.int32, s2.shape, 1)
    return jnp.where(lane < POOL_GROUP, s2, jnp.where(lane < 2 * POOL_GROUP, s4, jnp.where(lane < 3 * POOL_GROUP, s8, s16)))


def _pool_window(shape):
    lane = lax.broadcasted_iota(jnp.int32, shape, 1)
    return jnp.where(lane < POOL_GROUP, POOL_WINDOWS[0],
                     jnp.where(lane < 2 * POOL_GROUP, POOL_WINDOWS[1],
                               jnp.where(lane < 3 * POOL_GROUP, POOL_WINDOWS[2], POOL_WINDOWS[3])))


def _ffn_kernel(has_mix, has_final, *refs):
    refs = list(refs)
    x_ref = refs.pop(0)
    if has_mix:
        ya_ref, yb_ref, yc_ref, wo_ref = refs[:4]
        refs = refs[4:]
    g_ref, wg_ref, wu_ref, wd_ref = refs[:4]
    refs = refs[4:]
    if has_final:
        fg_ref = refs.pop(0)
    o_ref = refs.pop(0)

    x = x_ref[...]
    if has_mix:
        cat = jnp.concatenate([ya_ref[...], yb_ref[0], yb_ref[1], yc_ref[...]], axis=1)
        x = x + jnp.dot(cat, wo_ref[...], preferred_element_type=f32)
    xn = _rms(x, g_ref[...]).astype(bf16)
    gate = jnp.dot(xn, wg_ref[...], preferred_element_type=f32)
    up = jnp.dot(xn, wu_ref[...], preferred_element_type=f32)
    h = (_silu(gate) * up).astype(bf16)
    y = x + 0.5 * jnp.dot(h, wd_ref[...], preferred_element_type=f32)
    if has_final:
        y = _rms(y, fg_ref[...])
    o_ref[...] = y


def _ffn(x, layer, g, wg, wu, wd, mix=None, final_g=None):
    m = x.shape[0]
    tm = min(ROW_TILE, m)
    assert m % tm == 0
    row = lambda w: pl.BlockSpec((tm, w), lambda i: (i, 0))
    args, specs = [x], [row(D_MODEL)]
    if mix is not None:
        ya, yb, yc, wo = mix
        args += [ya, yb, yc, wo]
        specs += [row(D_POOL), pl.BlockSpec((2, tm, LANES), lambda i: (0, i, 0)), row(D_SSM), _layer_spec(wo, layer)]
    args += [g, wg, wu, wd]
    specs += [_layer_spec(g, layer), _layer_spec(wg, layer), _layer_spec(wu, layer), _layer_spec(wd, layer)]
    if final_g is not None:
        args.append(final_g)
        specs.append(_const_spec(final_g.shape))
    return pl.pallas_call(
        functools.partial(_ffn_kernel, mix is not None, final_g is not None),
        grid=(m // tm,),
        in_specs=specs,
        out_specs=row(D_MODEL),
        out_shape=jax.ShapeDtypeStruct((m, D_MODEL), f32),
        compiler_params=_params(1),
        name="ffn",
    )(*args)


def _inproj_kernel(tiles_per_seq, x_ref, g_ref, w_ref, pbd_ref, pscale_ref, cw_ref, cb_ref, dtb_ref, cos_ref, sin_ref,
                   ya_ref, q_ref, k_ref, v_ref, z_ref, u_ref, dt_ref, pst_ref, cst_ref, xa_buf, xbc_buf):
    tm = x_ref.shape[0]
    t = pl.program_id(0) % tiles_per_seq

    xn = _rms(x_ref[...], g_ref[...]).astype(bf16)
    proj = lambda a, b: jnp.dot(xn, w_ref[:, a:b], preferred_element_type=f32)

    @pl.when(t == 0)
    def _():
        xa_buf[0:POOL_HALO, :] = jnp.zeros((POOL_HALO, D_POOL), f32)
        xbc_buf[0:CONV_HALO, :] = jnp.zeros((CONV_HALO, D_CONV), f32)

    xbc = proj(OFF_XBC, OFF_XBC + D_CONV)
    pa = proj(0, OFF_Q)
    qkv = proj(OFF_Q, OFF_Z)
    zz = proj(OFF_Z, OFF_XBC)
    xbc_buf[CONV_HALO:CONV_HALO + tm, :] = xbc
    conv = cb_ref[...] + xbc * cw_ref[CONV_WIDTH - 1:CONV_WIDTH, :]
    for j in range(1, CONV_WIDTH):
        conv = conv + xbc_buf[CONV_HALO - j:CONV_HALO - j + tm, :] * cw_ref[CONV_WIDTH - 1 - j:CONV_WIDTH - j, :]
    u_ref[...] = _silu(conv)
    cst_ref[0] = xbc[tm - CONV_HALO:tm, :]
    xbc_buf[0:CONV_HALO, :] = xbc[tm - CONV_HALO:tm, :]

    xa = pa[:, 0:D_POOL]
    dt_ref[...] = _softplus(pa[:, OFF_DT:OFF_DT + LANES] + dtb_ref[...])
    xa_buf[POOL_HALO:POOL_HALO + tm, :] = xa
    ext = xa_buf[...]
    s2 = ext + pltpu.roll(ext, 1, 0)
    s4 = s2 + pltpu.roll(s2, 2, 0)
    s8 = s4 + pltpu.roll(s4, 4, 0)
    s16 = s8 + pltpu.roll(s8, 8, 0)
    sums = _pool_select(s2, s4, s8, s16)[POOL_HALO:POOL_HALO + tm, :]
    pos = t * tm + lax.broadcasted_iota(jnp.int32, (tm, D_POOL), 0)
    cnt = jnp.minimum(_pool_window((tm, D_POOL)), pos + 1).astype(f32)
    dmean = sums / cnt - xa
    ya = jnp.dot(dmean.astype(bf16), pbd_ref[...], preferred_element_type=f32) * pscale_ref[...]
    ya_ref[...] = ya.astype(ya_ref.dtype)
    pst_ref[0] = xa[tm - POOL_HALO:tm, :]
    xa_buf[0:POOL_HALO, :] = xa[tm - POOL_HALO:tm, :]

    cos = cos_ref[...]
    sin = sin_ref[...]
    for s in range(2):
        q_ref[s] = _rope_slab(qkv[:, s * LANES:(s + 1) * LANES], cos, sin)
        k_ref[s] = _rope_slab(qkv[:, D_ATT + s * LANES:D_ATT + (s + 1) * LANES], cos, sin)
        v_ref[s] = qkv[:, 2 * D_ATT + s * LANES:2 * D_ATT + (s + 1) * LANES]

    z_ref[...] = zz


def _inproj_prompt(x, layer, batch, seq, g, w, pbd, pscale, cw, cb, dtb, cos, sin):
    m = x.shape[0]
    tm = ROW_TILE
    tps = seq // tm
    row = lambda wd: pl.BlockSpec((tm, wd), lambda i: (i, 0))
    slab = pl.BlockSpec((2, tm, LANES), lambda i: (0, i, 0))
    tab = pl.BlockSpec((tm, LANES), lambda i: (i % tps, 0))
    out_shape = (
        jax.ShapeDtypeStruct((m, D_POOL), bf16),
        jax.ShapeDtypeStruct((2, m, LANES), f32),
        jax.ShapeDtypeStruct((2, m, LANES), f32),
        jax.ShapeDtypeStruct((2, m, LANES), f32),
        jax.ShapeDtypeStruct((m, D_SSM), f32),
        jax.ShapeDtypeStruct((m, D_CONV), f32),
        jax.ShapeDtypeStruct((m, LANES), f32),
        jax.ShapeDtypeStruct((batch, POOL_HALO, D_POOL), f32),
        jax.ShapeDtypeStruct((batch, CONV_HALO, D_CONV), f32),
    )
    out_specs = (
        row(D_POOL), slab, slab, slab, row(D_SSM), row(D_CONV), row(LANES),
        pl.BlockSpec((1, POOL_HALO, D_POOL), lambda i: (i // tps, 0, 0)),
        pl.BlockSpec((1, CONV_HALO, D_CONV), lambda i: (i // tps, 0, 0)),
    )
    in_specs = [row(D_MODEL), _layer_spec(g, layer), _layer_spec(w, layer), _const_spec(pbd.shape), _const_spec(pscale.shape),
                _const_spec(cw.shape), _const_spec(cb.shape), _const_spec(dtb.shape), tab, tab]
    return pl.pallas_call(
        functools.partial(_inproj_kernel, tps),
        grid=(m // tm,),
        in_specs=in_specs,
        out_specs=out_specs,
        out_shape=out_shape,
        scratch_shapes=[pltpu.VMEM((POOL_HALO + tm, D_POOL), f32), pltpu.VMEM((CONV_HALO + tm, D_CONV), f32)],
        compiler_params=_params(1),
        name="inproj",
    )(x, g, w, pbd, pscale, cw, cb, dtb, cos, sin)


def _attn_kernel(q_ref, kp_ref, kc_ref, vp_ref, vc_ref, o_ref, kbuf, vbuf, acc, mrun, lrun):
    t = pl.program_id(2)
    blk = N_BACK
    kbuf[0:ATT_TILE, :] = kp_ref[0, 0]
    kbuf[ATT_TILE:2 * ATT_TILE, :] = kc_ref[0, 0]
    vbuf[0:ATT_TILE, :] = vp_ref[0, 0]
    vbuf[ATT_TILE:2 * ATT_TILE, :] = vc_ref[0, 0]

    qi = lax.broadcasted_iota(jnp.int32, (2 * blk, 2 * blk), 0) % blk
    kj = lax.broadcasted_iota(jnp.int32, (2 * blk, 2 * blk), 1)
    dist = blk + qi - kj
    band = (dist >= 0) & (dist <= N_BACK)
    bias_full = jnp.where(band, 0.0, -jnp.inf).astype(f32)
    bias_cur = jnp.where(band & (kj >= blk), 0.0, -jnp.inf).astype(f32)
    head0 = lax.broadcasted_iota(jnp.int32, (blk, LANES), 1) < HEAD_DIM
    head0_k = lax.broadcasted_iota(jnp.int32, (2 * blk, LANES), 1) < HEAD_DIM
    ones_blk = jnp.concatenate([jnp.where(head0_k, 1.0, 0.0), jnp.where(head0_k, 0.0, 1.0)], axis=0).astype(bf16)
    scale = math.log2(math.e) / math.sqrt(HEAD_DIM)

    for d in DILATIONS:
        first = d == DILATIONS[0]
        last = d == DILATIONS[-1]
        span = blk * d
        n_units = ATT_TILE // blk

        def group(gi, carry, d=d, first=first, last=last, span=span):
            rows, scores, vcats = [], [], []
            for j in range(ATT_UNROLL):
                u = gi * ATT_UNROLL + j
                sb = u // d
                r = u % d
                qstart = sb * span + r
                kstart = ATT_TILE + qstart - span
                if d > 1:
                    qrows = pl.ds(qstart, blk, stride=d)
                    krows = pl.ds(kstart, 2 * blk, stride=d)
                else:
                    qrows = pl.ds(pl.multiple_of(qstart, blk), blk)
                    krows = pl.ds(pl.multiple_of(kstart, blk), 2 * blk)
                rows.append(qrows)
                qb = (q_ref[0, 0, qrows, :] * scale).astype(bf16)
                kb = kbuf[krows, :].astype(bf16)
                vb = vbuf[krows, :].astype(bf16)
                bias = jnp.where(jnp.logical_or(t > 0, sb > 0), bias_full, bias_cur)
                qcat = jnp.concatenate([jnp.where(head0, qb, jnp.zeros_like(qb)), jnp.where(head0, jnp.zeros_like(qb), qb)],
                                       axis=0)
                scores.append(lax.dot_general(qcat, kb, (((1,), (1,)), ((), ())), preferred_element_type=f32) + bias)
                vcat = jnp.concatenate([jnp.where(head0_k, vb, jnp.zeros_like(vb)),
                                        jnp.where(head0_k, jnp.zeros_like(vb), vb)], axis=0)
                vcats.append(jnp.concatenate([vcat, ones_blk], axis=1))
            stats = []
            for j in range(ATT_UNROLL):
                m2 = jnp.max(scores[j], axis=1, keepdims=True)
                pb = jnp.exp2(scores[j] - m2).astype(bf16)
                pcat = jnp.concatenate([pb[0:blk], pb[blk:2 * blk]], axis=1)
                stats.append((pcat, jnp.where(head0, m2[0:blk], m2[blk:2 * blk])))
            pvs = [jnp.dot(stats[j][0], vcats[j], preferred_element_type=f32) for j in range(ATT_UNROLL)]
            for j in range(ATT_UNROLL):
                qrows, pv, l_e, m_e = rows[j], pvs[j][:, 0:LANES], pvs[j][:, LANES:2 * LANES], stats[j][1]
                if first:
                    acc[qrows, :] = pv
                    mrun[qrows, :] = m_e
                    lrun[qrows, :] = l_e
                else:
                    m_old = mrun[qrows, :]
                    m_new = jnp.maximum(m_old, m_e)
                    a = jnp.exp2(m_old - m_new)
                    b = jnp.exp2(m_e - m_new)
                    acc_new = acc[qrows, :] * a + pv * b
                    l_new = lrun[qrows, :] * a + l_e * b
                    if last:
                        o_ref[0, 0, qrows, :] = (acc_new / l_new).astype(o_ref.dtype)
                    else:
                        acc[qrows, :] = acc_new
                        lrun[qrows, :] = l_new
                        mrun[qrows, :] = m_new
            return carry

        lax.fori_loop(0, n_units // ATT_UNROLL, group, 0)


def _attn_prompt(q, k, v, batch, seq):
    nt = seq // ATT_TILE
    q4 = q.reshape(2, batch, seq, LANES)
    k4 = k.reshape(2, batch, seq, LANES)
    v4 = v.reshape(2, batch, seq, LANES)
    cur = pl.BlockSpec((1, 1, ATT_TILE, LANES), lambda b, s, t: (s, b, t, 0))
    prev = pl.BlockSpec((1, 1, ATT_TILE, LANES), lambda b, s, t: (s, b, jnp.maximum(t - 1, 0), 0))
    out = pl.pallas_call(
        _attn_kernel,
        grid=(batch, 2, nt),
        in_specs=[cur, prev, cur, prev, cur],
        out_specs=cur,
        out_shape=jax.ShapeDtypeStruct((2, batch, seq, LANES), bf16),
        scratch_shapes=[pltpu.VMEM((2 * ATT_TILE, LANES), f32), pltpu.VMEM((2 * ATT_TILE, LANES), f32),
                        pltpu.VMEM((ATT_TILE, LANES), f32), pltpu.VMEM((ATT_TILE, LANES), f32),
                        pltpu.VMEM((ATT_TILE, LANES), f32)],
        compiler_params=_params(3),
        name="attn",
    )(q4, k4, k4, v4, v4)
    return out.reshape(2, batch * seq, LANES)


def _kv_window_kernel(k_ref, v_ref, *rest):
    ko_ref, vo_ref = rest[-2:]
    ko_ref[0] = k_ref[0, 0].T
    vo_ref[0] = v_ref[0, 0].T


def _kv_window(layer, depth, k, v, batch, seq, prev):
    assert seq % ATT_WIN == 0
    k4 = k.reshape(2, batch, seq, LANES)
    v4 = v.reshape(2, batch, seq, LANES)
    src = pl.BlockSpec((1, 1, ATT_WIN, LANES), lambda b, s: (s, b, seq // ATT_WIN - 1, 0))
    dst = pl.BlockSpec((None, 1, LANES, ATT_WIN), lambda b, s: (layer, b, s, 0))
    shape = jax.ShapeDtypeStruct((depth, batch, D_ATT, ATT_WIN), f32)
    args, in_specs, aliases = [k4, v4], [src, src], {}
    if prev is not None:
        in_specs += [pl.BlockSpec(memory_space=pl.ANY)] * 2
        aliases = {2: 0, 3: 1}
        args += list(prev)
    return pl.pallas_call(
        _kv_window_kernel,
        grid=(batch, 2),
        in_specs=in_specs,
        out_specs=(dst, dst),
        out_shape=(shape, shape),
        input_output_aliases=aliases,
        compiler_params=_params(2),
        name="kv_window",
    )(*args)


def _split3(x):
    hi = x.astype(bf16)
    rest = x - hi.astype(f32)
    mid = rest.astype(bf16)
    lo = (rest - mid.astype(f32)).astype(bf16)
    return hi, mid, lo


def _ssd_kernel(u_ref, dt_ref, z_ref, a2_ref, e3_ref, dskip_ref, norm_ref, y_ref, state_ref, st):
    @pl.when(pl.program_id(1) == 0)
    def _():
        st[...] = jnp.zeros_like(st)

    q = SSM_CHUNK
    nc = u_ref.shape[0] // q
    gl = D_SSM // SSM_GROUPS
    hpg = N_SSM_HEADS // SSM_GROUPS
    li = lax.broadcasted_iota(jnp.int32, (q, q), 0)
    si = lax.broadcasted_iota(jnp.int32, (q, q), 1)
    causal = li >= si
    tril = causal.astype(bf16)
    triu = (li <= si).astype(bf16)
    lo = lax.broadcasted_iota(jnp.int32, (q, LANES), 1) < SSM_HEAD_DIM
    rows = lambda c: slice(c * q, (c + 1) * q)
    b_of = lambda u, g: u[:, D_SSM + g * SSM_STATE:D_SSM + (g + 1) * SSM_STATE]
    c_of = lambda u, g: u[:, D_SSM + (SSM_GROUPS + g) * SSM_STATE:D_SSM + (SSM_GROUPS + g + 1) * SSM_STATE]

    v = {}
    chunks = range(nc)
    groups = range(SSM_GROUPS)
    gsl = lambda g: slice(g * gl, (g + 1) * gl)

    def level1():
        for c in chunks:
            u = u_ref[rows(c), :]
            dtp = dt_ref[rows(c), :]
            dta = dtp * a2_ref[...]
            v["dtp", c] = dtp
            v["cum3", c] = jnp.dot(tril, jnp.concatenate(_split3(dta), axis=1), preferred_element_type=f32)
            v["acs_row", c] = _dot_f32_by_01(dta.T[0:SUBLANES], triu)
            for g in groups:
                v["cb", c, g] = lax.dot_general(c_of(u, g).astype(bf16), b_of(u, g).astype(bf16), (((1,), (1,)), ((), ())),
                                                preferred_element_type=f32)

    def level2():
        for c in chunks:
            cum3 = v["cum3", c]
            cum = cum3[:, 0:LANES] + cum3[:, LANES:2 * LANES] + cum3[:, 2 * LANES:3 * LANES]
            v["both", c] = jnp.dot(jnp.concatenate(_split3(jnp.concatenate([v["dtp", c], cum], axis=0)), axis=1), e3_ref[...],
                                   preferred_element_type=f32)

    def level3():
        for c in chunks:
            u = u_ref[rows(c), :]
            dt_exp, acs_exp, acs_row = v["both", c][0:q], v["both", c][q:2 * q], v["acs_row", c]
            xdt = u[:, 0:D_SSM] * dt_exp
            xdt_b = xdt.astype(bf16)
            parts = []
            for g in groups:
                for pair in range(hpg // 2):
                    mats = []
                    for hh in range(2):
                        h = g * hpg + pair * 2 + hh
                        col = acs_exp[:, h * SSM_HEAD_DIM:h * SSM_HEAD_DIM + 1]
                        diff = jnp.where(causal, col - acs_row[h:h + 1, :], -jnp.inf)
                        mats.append((v["cb", c, g] * jnp.exp2(diff)).astype(bf16))
                    c0 = (g * hpg + pair * 2) * SSM_HEAD_DIM
                    xp = xdt_b[:, c0:c0 + LANES]
                    rhs = jnp.concatenate([jnp.where(lo, xp, jnp.zeros_like(xp)), jnp.where(lo, jnp.zeros_like(xp), xp)],
                                          axis=0)
                    parts.append(jnp.dot(jnp.concatenate(mats, axis=1), rhs, preferred_element_type=f32))
            v["ydiag", c] = jnp.concatenate(parts, axis=1)
            acs_last = acs_exp[q - 1:q, :]
            xw = (xdt * jnp.exp2(acs_last - acs_exp)).astype(bf16)
            v["s_new", c] = jnp.concatenate(
                [jnp.dot(b_of(u, g).T.astype(bf16), xw[:, gsl(g)], preferred_element_type=f32) for g in groups], axis=1)
            v["acs", c] = acs_exp

    def level4():
        state = st[...]
        for c in chunks:
            u = u_ref[rows(c), :]
            state_b = state.astype(bf16)
            v["yoff", c] = jnp.concatenate(
                [jnp.dot(c_of(u, g).astype(bf16), state_b[:, gsl(g)], preferred_element_type=f32) for g in groups], axis=1)
            state = state * jnp.exp2(v["acs", c][q - 1:q, :]) + v["s_new", c]
        st[...] = state
        state_ref[0] = state

    def level5():
        for c in chunks:
            xs = u_ref[rows(c), 0:D_SSM]
            y = v["ydiag", c] + v["yoff", c] * jnp.exp2(v["acs", c])
            y = (y + dskip_ref[...] * xs) * _silu(z_ref[rows(c), :])
            outs = []
            for g in groups:
                yg = y[:, gsl(g)]
                outs.append(yg * lax.rsqrt(jnp.mean(yg * yg, -1, keepdims=True) + RMS_EPS))
            y_ref[rows(c), :] = (jnp.concatenate(outs, axis=1) * norm_ref[...]).astype(y_ref.dtype)

    for level in (level1, level2, level3, level4, level5):
        level()


def _ssd_prompt(u, dt, z, batch, seq, a2_pad, e3_mat, dskip, norm):
    tq = SSD_TILE
    tps = seq // tq
    row = lambda wd: pl.BlockSpec((tq, wd), lambda b, t: (b * tps + t, 0))
    consts = [a2_pad, e3_mat, dskip, norm]
    return pl.pallas_call(
        _ssd_kernel,
        grid=(batch, tps),
        in_specs=[row(D_CONV), row(LANES), row(D_SSM)] + [_const_spec(c.shape) for c in consts],
        out_specs=(row(D_SSM), pl.BlockSpec((1, SSM_STATE, D_SSM), lambda b, t: (b, 0, 0))),
        out_shape=(jax.ShapeDtypeStruct((batch * seq, D_SSM), bf16), jax.ShapeDtypeStruct((batch, SSM_STATE, D_SSM), f32)),
        scratch_shapes=[pltpu.VMEM((SSM_STATE, D_SSM), f32)],
        compiler_params=_params(2),
        name="ssd",
    )(u, dt, z, *consts)


def _sample_pre_kernel(x_ref, g_ref, w_ref, cpool_ref, pbd_ref, pscale_ref, cconv_ref, cw_ref, cb_ref, dtb_ref,
                       cos_ref, sin_ref, ya_ref, q_ref, k_ref, v_ref, z_ref, u_ref, dt_ref, pnew_ref, cnew_ref):
    xn = _rms(x_ref[...], g_ref[...]).astype(bf16)
    proj = jnp.dot(xn, w_ref[...], preferred_element_type=f32)

    xa = proj[:, 0:D_POOL]
    back = lambda i: cpool_ref[POOL_BUF - i]
    s2 = xa + back(1)
    s4 = s2 + back(2) + back(3)
    s8 = s4 + back(4) + back(5) + back(6) + back(7)
    s16 = s8
    for i in range(8, 16):
        s16 = s16 + back(i)
    cnt = jnp.minimum(_pool_window(xa.shape), PAST_LEN + 1).astype(f32)
    dmean = _pool_select(s2, s4, s8, s16) / cnt - xa
    ya = jnp.dot(dmean.astype(bf16), pbd_ref[...], preferred_element_type=f32) * pscale_ref[...]
    ya_ref[...] = ya.astype(ya_ref.dtype)
    for i in range(POOL_BUF - 1):
        pnew_ref[i] = cpool_ref[i + 1]
    pnew_ref[POOL_BUF - 1] = xa

    cos = cos_ref[...]
    sin = sin_ref[...]
    for s in range(2):
        sl = slice(s * LANES, (s + 1) * LANES)
        q_ref[:, sl] = _rope_slab(proj[:, OFF_Q + s * LANES:OFF_Q + (s + 1) * LANES], cos, sin)
        k_ref[:, sl] = _rope_slab(proj[:, OFF_K + s * LANES:OFF_K + (s + 1) * LANES], cos, sin)
    v_ref[...] = proj[:, OFF_V:OFF_V + D_ATT]
    z_ref[...] = proj[:, OFF_Z:OFF_Z + D_SSM]

    xbc = proj[:, OFF_XBC:OFF_XBC + D_CONV]
    conv = cb_ref[...] + xbc * cw_ref[CONV_WIDTH - 1:CONV_WIDTH, :]
    for j in range(1, CONV_WIDTH):
        conv = conv + cconv_ref[CONV_WIDTH - 1 - j] * cw_ref[CONV_WIDTH - 1 - j:CONV_WIDTH - j, :]
    u_ref[...] = _silu(conv)
    for j in range(CONV_WIDTH - 2):
        cnew_ref[j] = cconv_ref[j + 1]
    cnew_ref[CONV_WIDTH - 2] = xbc

    dt_ref[...] = _softplus(proj[:, OFF_DT:OFF_DT + LANES] + dtb_ref[...])


def _sample_pre(x, layer, g, w, cpool, pbd, pscale, cconv, cw, cb, dtb, cos, sin):
    n = x.shape[0]
    args = [x, g, w, cpool, pbd, pscale, cconv, cw, cb, dtb, cos, sin]
    out_shape = (
        jax.ShapeDtypeStruct((n, D_POOL), bf16),
        jax.ShapeDtypeStruct((n, D_ATT), f32),
        jax.ShapeDtypeStruct((n, D_ATT), f32),
        jax.ShapeDtypeStruct((n, D_ATT), f32),
        jax.ShapeDtypeStruct((n, D_SSM), f32),
        jax.ShapeDtypeStruct((n, D_CONV), f32),
        jax.ShapeDtypeStruct((n, LANES), f32),
        jax.ShapeDtypeStruct((POOL_BUF, n, D_POOL), f32),
        jax.ShapeDtypeStruct((CONV_WIDTH - 1, n, D_CONV), f32),
    )
    full = lambda s: pl.BlockSpec(s.shape, lambda i, nd=len(s.shape): (0,) * nd)
    in_specs = [full(a) for a in args]
    in_specs[1] = _layer_spec(g, layer)
    in_specs[2] = _layer_spec(w, layer)
    return pl.pallas_call(
        _sample_pre_kernel,
        grid=(1,),
        in_specs=in_specs,
        out_specs=tuple(full(s) for s in out_shape),
        out_shape=out_shape,
        compiler_params=_params(1),
        name="sample_pre",
    )(*args)


def _sample_ssd_kernel(u_ref, dt_ref, z_ref, st_ref, aexp_ref, e_ref, dskip_ref, norm_ref, *rest):
    y_ref, stnew_ref = rest[-2:]
    n = u_ref.shape[0]
    gl = D_SSM // SSM_GROUPS
    u = u_ref[...]
    xs = u[:, 0:D_SSM]
    dt_exp = _dot_f32_by_01(dt_ref[...], e_ref[...])
    pad = jnp.zeros((LANES - n, D_SSM), f32)
    dec_t = jnp.concatenate([jnp.exp(dt_exp * aexp_ref[...]), pad], axis=0).T
    dtx_t = jnp.concatenate([dt_exp * xs, pad], axis=0).T
    lane = lax.broadcasted_iota(jnp.int32, (D_SSM, LANES), 1)
    y_t = jnp.zeros((D_SSM, LANES), f32)
    for i in range(n):
        dec = dec_t[:, i:i + 1]
        dtx = dtx_t[:, i:i + 1]
        ycols = []
        for g in range(SSM_GROUPS):
            rows = slice(g * gl, (g + 1) * gl)
            b_row = u[i:i + 1, D_SSM + g * SSM_STATE:D_SSM + (g + 1) * SSM_STATE]
            c_row = u[i:i + 1, D_SSM + (SSM_GROUPS + g) * SSM_STATE:D_SSM + (SSM_GROUPS + g + 1) * SSM_STATE]
            h_new = dec[rows] * st_ref[i, rows, :] + dtx[rows] * b_row
            stnew_ref[i, rows, :] = h_new
            ycols.append(jnp.sum(h_new * c_row, axis=1, keepdims=True))
        y_t = jnp.where(lane == i, jnp.concatenate(ycols, axis=0), y_t)
    y = y_t.T[0:n]
    y = (y + dskip_ref[...] * xs) * _silu(z_ref[...])
    outs = []
    for g in range(SSM_GROUPS):
        yg = y[:, g * gl:(g + 1) * gl]
        outs.append(yg * lax.rsqrt(jnp.mean(yg * yg, -1, keepdims=True) + RMS_EPS))
    y_ref[...] = (jnp.concatenate(outs, axis=1) * norm_ref[...]).astype(y_ref.dtype)


def _sample_ssd(layer, u, dt, z, state, prev, a_exp, e_mat, dskip, norm):
    n = u.shape[0]
    full = lambda s: pl.BlockSpec(s.shape, lambda i, nd=len(s.shape): (0,) * nd)
    of_layer = pl.BlockSpec((None,) + state.shape[1:], lambda i: (layer, 0, 0, 0))
    args = [u, dt, z, state, a_exp, e_mat, dskip, norm]
    in_specs = [full(u), full(dt), full(z), of_layer, full(a_exp), full(e_mat), full(dskip), full(norm)]
    aliases = {}
    if prev is not None:
        in_specs.append(pl.BlockSpec(memory_space=pl.ANY))
        aliases = {len(args): 1}
        args.append(prev)
    y_shape = jax.ShapeDtypeStruct((n, D_SSM), bf16)
    return pl.pallas_call(
        _sample_ssd_kernel,
        grid=(1,),
        in_specs=in_specs,
        out_specs=(full(y_shape), of_layer),
        out_shape=(y_shape, jax.ShapeDtypeStruct(state.shape, f32)),
        input_output_aliases=aliases,
        compiler_params=_params(1),
        name="sample_ssd",
    )(*args)


def _sample_attn_kernel(has_prev, q_ref, kn_ref, vn_ref, knc_ref, vnc_ref, kc_ref, vc_ref, *rest):
    if has_prev:
        rest = rest[2:]
    y_ref, ks_ref, vs_ref = rest
    wb = kc_ref.shape[3]
    kc = kc_ref[0, 0]
    vc = vc_ref[0, 0]
    q = q_ref[0]
    kn = kn_ref[0]
    vn = vn_ref[0]
    scale = 1.0 / math.sqrt(HEAD_DIM)

    own = (lax.broadcasted_iota(jnp.int32, (SUBLANES, D_ATT), 1) // HEAD_DIM) == lax.broadcasted_iota(
        jnp.int32, (SUBLANES, D_ATT), 0)
    qbd = jnp.where(own, jnp.broadcast_to(q, (SUBLANES, D_ATT)), 0.0)
    s_all = jnp.dot(qbd.astype(bf16), kc.astype(bf16), preferred_element_type=f32) * scale
    s_new = jnp.sum(jnp.where(own, jnp.broadcast_to(q * kn, (SUBLANES, D_ATT)), 0.0), axis=1, keepdims=True) * scale
    dist = wb - lax.broadcasted_iota(jnp.int32, (SUBLANES, wb), 1)
    ms, ps, ls, es = [], [], [], []
    for d in DILATIONS:
        valid = ((dist % d) == 0) & (dist <= N_BACK * d)
        s = jnp.where(valid, s_all, -jnp.inf)
        m = jnp.maximum(jnp.max(s, axis=1, keepdims=True), s_new)
        p = jnp.exp(s - m)
        e_new = jnp.exp(s_new - m)
        ms.append(m)
        ps.append(p)
        es.append(e_new)
        ls.append(jnp.sum(p, axis=1, keepdims=True) + e_new)
    m_all = jnp.maximum(jnp.maximum(ms[0], ms[1]), ms[2])
    p_all = jnp.zeros_like(ps[0])
    w_new = jnp.zeros_like(s_new)
    l_all = jnp.zeros_like(s_new)
    for m, p, e_new, l in zip(ms, ps, es, ls):
        c = jnp.exp(m - m_all)
        p_all = p_all + c * p
        w_new = w_new + c * e_new
        l_all = l_all + c * l
    o_full = lax.dot_general(p_all.astype(bf16), vc.astype(bf16), (((1,), (1,)), ((), ())), preferred_element_type=f32)
    to_row = lambda t: jnp.sum(jnp.where(own, t, 0.0), axis=0, keepdims=True)
    o_row = to_row(o_full) + to_row(jnp.broadcast_to(w_new, (SUBLANES, D_ATT))) * vn
    y_ref[0] = o_row / to_row(jnp.broadcast_to(l_all, (SUBLANES, D_ATT)))

    newest = lax.broadcasted_iota(jnp.int32, (D_ATT, wb), 1) == wb - 1
    ks_ref[0, 0] = jnp.where(newest, knc_ref[0], pltpu.roll(kc, wb - 1, 1))
    vs_ref[0, 0] = jnp.where(newest, vnc_ref[0], pltpu.roll(vc, wb - 1, 1))


def _sample_attn(layer, q, kn, vn, cache_k, cache_v, prev):
    depth, n, da, wb = cache_k.shape
    rowv = pl.BlockSpec((1, 1, da), lambda i: (i, 0, 0))
    colv = pl.BlockSpec((1, da, 1), lambda i: (i, 0, 0))
    big = pl.BlockSpec((1, 1, da, wb), lambda i: (layer, i, 0, 0))
    in_specs = [rowv, rowv, rowv, colv, colv, big, big]
    args = [q.reshape(n, 1, da), kn.reshape(n, 1, da), vn.reshape(n, 1, da), kn.reshape(n, da, 1), vn.reshape(n, da, 1),
            cache_k, cache_v]
    aliases = {}
    if prev is not None:
        in_specs += [pl.BlockSpec(memory_space=pl.ANY)] * 2
        aliases = {len(args): 1, len(args) + 1: 2}
        args += list(prev)
    return pl.pallas_call(
        functools.partial(_sample_attn_kernel, prev is not None),
        grid=(n,),
        in_specs=in_specs,
        out_specs=(rowv, big, big),
        out_shape=(jax.ShapeDtypeStruct((n, 1, da), f32), jax.ShapeDtypeStruct(cache_k.shape, f32),
                   jax.ShapeDtypeStruct(cache_v.shape, f32)),
        input_output_aliases=aliases,
        compiler_params=_params(1),
        name="sample_attn",
    )(*args)


def _rope_tables(pos):
    half = HEAD_DIM // 2
    inv = ROPE_THETA ** (-jnp.arange(half, dtype=f32) / half)
    ang = pos.astype(f32)[:, None] * inv[None]
    cos = jnp.tile(jnp.cos(ang), (1, LANES // half))
    sin = jnp.sin(ang)
    sin_signed = jnp.tile(jnp.concatenate([-sin, sin], axis=1), (1, LANES // HEAD_DIM))
    return cos, sin_signed


def kernel(x_prompt, x_sample, cache_pool, cache_k, cache_v, state_conv, state_ssm, ffn1_norm, ffn1_w_gate, ffn1_w_up,
           ffn1_w_down, mix_norm, w_in, pool_w, pool_scale, conv_w, conv_b, dt_bias, a_log, d_skip, ssm_norm, w_out,
           ffn2_norm, ffn2_w_gate, ffn2_w_up, ffn2_w_down, final_norm):
    batch, seq, _ = x_prompt.shape
    n_dec = x_sample.shape[0]
    depth = w_in.shape[0]
    wb = cache_k.shape[2]
    assert seq % ATT_TILE == 0 and seq % ROW_TILE == 0 and x_sample.shape[1] == 1 and wb == ATT_WIN

    yp = x_prompt.reshape(batch * seq, D_MODEL)
    ys = x_sample.reshape(n_dec, D_MODEL)
    cos_p, sin_p = _rope_tables(jnp.arange(seq))
    cos_s, sin_s = _rope_tables(jnp.full((1,), PAST_LEN))
    e_mat = (jnp.arange(LANES)[:, None] == (jnp.arange(D_SSM)[None, :] // SSM_HEAD_DIM)).astype(bf16)
    e3_mat = jnp.tile(e_mat, (3, 1))
    ck_t = jnp.transpose(cache_k, (0, 1, 3, 4, 2)).reshape(depth, n_dec, D_ATT, wb)
    cv_t = jnp.transpose(cache_v, (0, 1, 3, 4, 2)).reshape(depth, n_dec, D_ATT, wb)
    row2 = lambda a: a.reshape(1, -1)
    row3 = lambda a: a.reshape(depth, 1, -1)

    f1 = (row3(ffn1_norm), ffn1_w_gate.astype(bf16), ffn1_w_up.astype(bf16), ffn1_w_down.astype(bf16))
    f2 = (row3(ffn2_norm), ffn2_w_gate.astype(bf16), ffn2_w_up.astype(bf16), ffn2_w_down.astype(bf16))
    w_all = jnp.concatenate([w_in[:, :, 0:D_POOL],
                             jnp.pad(w_in[:, :, D_IN_PROJ - N_SSM_HEADS:], ((0, 0), (0, 0), (0, LANES - N_SSM_HEADS))),
                             w_in[:, :, D_POOL:D_IN_PROJ - N_SSM_HEADS]], axis=2).astype(bf16)
    wo_all = w_out.astype(bf16)
    g_mix = row3(mix_norm)

    st_p = [[] for _ in range(3)]
    st_s = [[] for _ in range(2)]
    kv_p = kv_s = ssm_s = None
    for i in range(depth):
        last = i == depth - 1
        pbd = jax.scipy.linalg.block_diag(*[pool_w[i, g] for g in range(len(POOL_WINDOWS))]).astype(bf16)
        pscale = row2(pool_scale[i])
        cb = row2(conv_b[i])
        dtb = row2(jnp.pad(dt_bias[i], (0, LANES - N_SSM_HEADS)))
        a_neg = -jnp.exp(a_log[i].astype(f32))
        a2_pad = row2(jnp.pad(a_neg * math.log2(math.e), (0, LANES - N_SSM_HEADS)))
        a_exp = row2(jnp.repeat(a_neg, SSM_HEAD_DIM))
        dskip = row2(jnp.repeat(d_skip[i], SSM_HEAD_DIM))
        norm = row2(ssm_norm[i])
        fin = row2(final_norm) if last else None

        yp = _ffn(yp, i, *f1)
        ya, q, k, v, z, u, dt, pst, cst = _inproj_prompt(yp, i, batch, seq, g_mix, w_all, pbd, pscale, conv_w[i], cb, dtb,
                                                         cos_p, sin_p)
        yb = _attn_prompt(q, k, v, batch, seq)
        yc, sst = _ssd_prompt(u, dt, z, batch, seq, a2_pad, e3_mat, dskip, norm)
        yp = _ffn(yp, i, *f2, mix=(ya, yb, yc, wo_all), final_g=fin)

        kv_p = _kv_window(i, depth, k, v, batch, seq, kv_p)
        st_p[0].append(pst[:, POOL_HALO - POOL_BUF:])
        st_p[1].append(cst[:, CONV_HALO - (CONV_WIDTH - 1):])
        st_p[2].append(jnp.transpose(sst.reshape(batch, SSM_STATE, N_SSM_HEADS, SSM_HEAD_DIM), (0, 2, 3, 1)))

        ys = _ffn(ys, i, *f1)
        cpool = jnp.transpose(cache_pool[i], (1, 0, 2))
        cconv = jnp.transpose(state_conv[i], (1, 0, 2))
        ya_s, q_s, k_s, v_s, z_s, u_s, dt_s, pnew, cnew = _sample_pre(ys, i, g_mix, w_all, cpool, pbd, pscale, cconv,
                                                                    conv_w[i], cb, dtb, cos_s, sin_s)
        yb_s, ks_buf, vs_buf = _sample_attn(i, q_s, k_s, v_s, ck_t, cv_t, kv_s)
        kv_s = (ks_buf, vs_buf)
        yc_s, ssm_s = _sample_ssd(i, u_s, dt_s, z_s, state_ssm.reshape(depth, n_dec, D_SSM, SSM_STATE), ssm_s, a_exp, e_mat,
                                  dskip, norm)
        yb_slab = jnp.transpose(yb_s.reshape(n_dec, 2, LANES), (1, 0, 2)).astype(bf16)
        ys = _ffn(ys, i, *f2, mix=(ya_s, yb_slab, yc_s, wo_all), final_g=fin)
        st_s[0].append(jnp.transpose(pnew, (1, 0, 2)))
        st_s[1].append(jnp.transpose(cnew, (1, 0, 2)))

    y_prompt = yp.reshape(batch, seq, D_MODEL)
    y_sample = ys.reshape(n_dec, 1, D_MODEL)
    pool_p, conv_p, ssm_p = (jnp.stack(s, 0) for s in st_p)
    pool_s, conv_s = (jnp.stack(s, 0) for s in st_s)
    ssm_s = ssm_s.reshape(state_ssm.shape)
    from_t = lambda t: jnp.transpose(t.reshape(depth, -1, N_ATT_HEADS, HEAD_DIM, wb), (0, 1, 4, 2, 3))
    k_p, v_p = from_t(kv_p[0]), from_t(kv_p[1])
    return (y_prompt, y_sample, pool_p, pool_s, k_p, from_t(kv_s[0]), v_p, from_t(kv_s[1]), conv_p, conv_s, ssm_p, ssm_s)
```

```python
import functools
import math

import jax
import jax.numpy as jnp
from jax import lax
from jax.experimental import pallas as pl
from jax.experimental.pallas import tpu as pltpu

f32 = jnp.float32
bf16 = jnp.bfloat16

D_MODEL = 1024
PAST_LEN = 16384
POOL_WINDOWS = (2, 4, 8, 16)
D_POOL = 256
POOL_GROUP = D_POOL // len(POOL_WINDOWS)
POOL_BUF = max(POOL_WINDOWS) - 1
HEAD_DIM = 64
N_ATT_HEADS = 4
D_ATT = N_ATT_HEADS * HEAD_DIM
DILATIONS = (16, 4, 1)
N_BACK = 128
ATT_WIN = 2048
ROPE_THETA = 10000.0
D_SSM = 512
SSM_HEAD_DIM = 64
N_SSM_HEADS = 8
SSM_STATE = 128
SSM_GROUPS = 2
CONV_WIDTH = 4
SSM_CHUNK = 128
D_CONV = D_SSM + 2 * SSM_GROUPS * SSM_STATE
D_IN_PROJ = D_POOL + 3 * D_ATT + D_SSM + D_CONV + N_SSM_HEADS
D_FF = 2816
RMS_EPS = 1e-6

LANES = 128
SUBLANES = 8
VMEM_LIMIT = 56 * 1024 * 1024

OFF_DT = D_POOL
OFF_Q = OFF_DT + LANES
OFF_K = OFF_Q + D_ATT
OFF_V = OFF_K + D_ATT
OFF_Z = OFF_V + D_ATT
OFF_XBC = OFF_Z + D_SSM
N_PROJ = OFF_XBC + D_CONV

ROW_TILE = 512
SSD_TILE = 1024
ATT_TILE = 2048
ATT_UNROLL = 8
INPROJ_TILE = 1024
INPROJ_SPLIT = 4
POOL_HALO = 16
CONV_HALO = 8


def _params(n_axes, flags=None):
    return pltpu.CompilerParams(dimension_semantics=("arbitrary",) * n_axes, vmem_limit_bytes=VMEM_LIMIT, flags=flags)


def _const_spec(shape):
    nd = len(shape)
    return pl.BlockSpec(shape, lambda *_: (0,) * nd, pipeline_mode=pl.Buffered(1))


def _layer_spec(arr, layer):
    nd = arr.ndim - 1
    return pl.BlockSpec((None,) + arr.shape[1:], lambda *_: (layer,) + (0,) * nd, pipeline_mode=pl.Buffered(1))


def _rms(x, g):
    return x * lax.rsqrt(jnp.mean(x * x, -1, keepdims=True) + RMS_EPS) * g


def _dot_f32_by_01(x, m01, lhs_is_01=False):
    hi = x.astype(bf16)
    rest = x - hi.astype(f32)
    mid = rest.astype(bf16)
    lo = (rest - mid.astype(f32)).astype(bf16)
    out = None
    for part in (hi, mid, lo):
        if lhs_is_01:
            t = jnp.dot(m01, part, preferred_element_type=f32)
        else:
            t = jnp.dot(part, m01, preferred_element_type=f32)
        out = t if out is None else out + t
    return out


def _silu(x):
    return x * jax.nn.sigmoid(x)


def _softplus(x):
    return jnp.maximum(x, 0.0) + jnp.log1p(jnp.exp(-jnp.abs(x)))


def _rope_slab(x, cos, sin_signed):
    lane = lax.broadcasted_iota(jnp.int32, x.shape, 1)
    first_half = (lane % HEAD_DIM) < (HEAD_DIM // 2)
    partner = jnp.where(first_half, pltpu.roll(x, LANES - HEAD_DIM // 2, 1), pltpu.roll(x, HEAD_DIM // 2, 1))
    return x * cos + partner * sin_signed


def _pool_select(s2, s4, s8, s16):
    lane = lax.broadcasted_iota(jnp.int32, s2.shape, 1)
    return jnp.where(lane < POOL_GROUP, s2, jnp.where(lane < 2 * POOL_GROUP, s4, jnp.where(lane < 3 * POOL_GROUP, s8, s16)))


def _pool_window(shape):
    lane = lax.broadcasted_iota(jnp.int32, shape, 1)
    return jnp.where(lane < POOL_GROUP, POOL_WINDOWS[0],
                     jnp.where(lane < 2 * POOL_GROUP, POOL_WINDOWS[1],
                               jnp.where(lane < 3 * POOL_GROUP, POOL_WINDOWS[2], POOL_WINDOWS[3])))


def _ffn_kernel(has_mix, has_final, *refs):
    refs = list(refs)
    x_ref = refs.pop(0)
    if has_mix:
        ya_ref, yb_ref, yc_ref, wo_ref = refs[:4]
        refs = refs[4:]
    g_ref, wg_ref, wu_ref, wd_ref = refs[:4]
    refs = refs[4:]
    if has_final:
        fg_ref = refs.pop(0)
    o_ref = refs.pop(0)

    x = x_ref[...]
    if has_mix:
        cat = jnp.concatenate([ya_ref[...], yb_ref[0], yb_ref[1], yc_ref[...]], axis=1)
        x = x + jnp.dot(cat, wo_ref[...], preferred_element_type=f32)
    xn = _rms(x, g_ref[...]).astype(bf16)
    gate = jnp.dot(xn, wg_ref[...], preferred_element_type=f32)
    up = jnp.dot(xn, wu_ref[...], preferred_element_type=f32)
    h = (_silu(gate) * up).astype(bf16)
    y = x + 0.5 * jnp.dot(h, wd_ref[...], preferred_element_type=f32)
    if has_final:
        y = _rms(y, fg_ref[...])
    o_ref[...] = y


def _ffn(x, layer, g, wg, wu, wd, mix=None, final_g=None):
    m = x.shape[0]
    tm = min(ROW_TILE, m)
    assert m % tm == 0
    row = lambda w: pl.BlockSpec((tm, w), lambda i: (i, 0))
    args, specs = [x], [row(D_MODEL)]
    if mix is not None:
        ya, yb, yc, wo = mix
        args += [ya, yb, yc, wo]
        specs += [row(D_POOL), pl.BlockSpec((2, tm, LANES), lambda i: (0, i, 0)), row(D_SSM), _layer_spec(wo, layer)]
    args += [g, wg, wu, wd]
    specs += [_layer_spec(g, layer), _layer_spec(wg, layer), _layer_spec(wu, layer), _layer_spec(wd, layer)]
    if final_g is not None:
        args.append(final_g)
        specs.append(_const_spec(final_g.shape))
    return pl.pallas_call(
        functools.partial(_ffn_kernel, mix is not None, final_g is not None),
        grid=(m // tm,),
        in_specs=specs,
        out_specs=row(D_MODEL),
        out_shape=jax.ShapeDtypeStruct((m, D_MODEL), f32),
        compiler_params=_params(1),
        name="ffn",
    )(*args)


def _inproj_kernel(tiles_per_seq, x_ref, g_ref, w_ref, pbd_ref, pscale_ref, cw_ref, cb_ref, dtb_ref, cos_ref, sin_ref,
                   ya_ref, q_ref, k_ref, v_ref, z_ref, u_ref, dt_ref, pst_ref, cst_ref, xa_buf, xbc_buf):
    tm = x_ref.shape[0]
    n_sub = INPROJ_SPLIT
    sub = tm // n_sub
    t = pl.program_id(0) % tiles_per_seq

    @pl.when(t == 0)
    def _():
        xa_buf[0:POOL_HALO, :] = jnp.zeros((POOL_HALO, D_POOL), f32)
        xbc_buf[0:CONV_HALO, :] = jnp.zeros((CONV_HALO, D_CONV), f32)

    groups = (("xbc", OFF_XBC, N_PROJ), ("pa", 0, OFF_Q), ("qkv", OFF_Q, OFF_Z), ("z", OFF_Z, OFF_XBC))
    projs = []
    for h in range(n_sub):
        xn = _rms(x_ref[h * sub:(h + 1) * sub, :], g_ref[...]).astype(bf16)
        projs.append({name: jnp.dot(xn, w_ref[:, a:b], preferred_element_type=f32) for name, a, b in groups})

    for h in range(n_sub):
        r0 = h * sub
        rows = slice(r0, r0 + sub)
        xbc, pa, qkv = projs[h]["xbc"], projs[h]["pa"], projs[h]["qkv"]

        xbc_buf[CONV_HALO + r0:CONV_HALO + r0 + sub, :] = xbc
        conv = cb_ref[...] + xbc * cw_ref[CONV_WIDTH - 1:CONV_WIDTH, :]
        for j in range(1, CONV_WIDTH):
            conv = conv + xbc_buf[CONV_HALO + r0 - j:CONV_HALO + r0 - j + sub, :] * cw_ref[CONV_WIDTH - 1 - j:CONV_WIDTH - j, :]
        u_ref[rows, :] = _silu(conv)

        xa = pa[:, 0:D_POOL]
        dt_ref[rows, :] = _softplus(pa[:, OFF_DT:OFF_DT + LANES] + dtb_ref[...])
        xa_buf[POOL_HALO + r0:POOL_HALO + r0 + sub, :] = xa
        ext = xa_buf[r0:r0 + sub + POOL_HALO, :]
        s2 = ext + pltpu.roll(ext, 1, 0)
        s4 = s2 + pltpu.roll(s2, 2, 0)
        s8 = s4 + pltpu.roll(s4, 4, 0)
        s16 = s8 + pltpu.roll(s8, 8, 0)
        sums = _pool_select(s2, s4, s8, s16)[POOL_HALO:POOL_HALO + sub, :]
        pos = t * tm + r0 + lax.broadcasted_iota(jnp.int32, (sub, D_POOL), 0)
        cnt = jnp.minimum(_pool_window((sub, D_POOL)), pos + 1).astype(f32)
        dmean = sums / cnt - xa
        ya = jnp.dot(dmean.astype(bf16), pbd_ref[...], preferred_element_type=f32) * pscale_ref[...]
        ya_ref[rows, :] = ya.astype(ya_ref.dtype)

        cos = cos_ref[rows, :]
        sin = sin_ref[rows, :]
        for s in range(2):
            q_ref[s, rows, :] = _rope_slab(qkv[:, s * LANES:(s + 1) * LANES], cos, sin)
            k_ref[s, rows, :] = _rope_slab(qkv[:, D_ATT + s * LANES:D_ATT + (s + 1) * LANES], cos, sin)
            v_ref[s, rows, :] = qkv[:, 2 * D_ATT + s * LANES:2 * D_ATT + (s + 1) * LANES]
        z_ref[rows, :] = projs[h]["z"]

    last_xbc = xbc_buf[tm:tm + CONV_HALO, :]
    cst_ref[0] = last_xbc
    xbc_buf[0:CONV_HALO, :] = last_xbc
    last_xa = xa_buf[tm:tm + POOL_HALO, :]
    pst_ref[0] = last_xa
    xa_buf[0:POOL_HALO, :] = last_xa


def _inproj_prompt(x, layer, batch, seq, g, w, pbd, pscale, cw, cb, dtb, cos, sin):
    m = x.shape[0]
    tm = INPROJ_TILE
    tps = seq // tm
    row = lambda wd: pl.BlockSpec((tm, wd), lambda i: (i, 0))
    slab = pl.BlockSpec((2, tm, LANES), lambda i: (0, i, 0))
    tab = pl.BlockSpec((tm, LANES), lambda i: (i % tps, 0))
    out_shape = (
        jax.ShapeDtypeStruct((m, D_POOL), bf16),
        jax.ShapeDtypeStruct((2, m, LANES), f32),
        jax.ShapeDtypeStruct((2, m, LANES), f32),
        jax.ShapeDtypeStruct((2, m, LANES), f32),
        jax.ShapeDtypeStruct((m, D_SSM), f32),
        jax.ShapeDtypeStruct((m, D_CONV), f32),
        jax.ShapeDtypeStruct((m, LANES), f32),
        jax.ShapeDtypeStruct((batch, POOL_HALO, D_POOL), f32),
        jax.ShapeDtypeStruct((batch, CONV_HALO, D_CONV), f32),
    )
    out_specs = (
        row(D_POOL), slab, slab, slab, row(D_SSM), row(D_CONV), row(LANES),
        pl.BlockSpec((1, POOL_HALO, D_POOL), lambda i: (i // tps, 0, 0)),
        pl.BlockSpec((1, CONV_HALO, D_CONV), lambda i: (i // tps, 0, 0)),
    )
    in_specs = [row(D_MODEL), _layer_spec(g, layer), _layer_spec(w, layer), _const_spec(pbd.shape), _const_spec(pscale.shape),
                _const_spec(cw.shape), _const_spec(cb.shape), _const_spec(dtb.shape), tab, tab]
    return pl.pallas_call(
        functools.partial(_inproj_kernel, tps),
        grid=(m // tm,),
        in_specs=in_specs,
        out_specs=out_specs,
        out_shape=out_shape,
        scratch_shapes=[pltpu.VMEM((POOL_HALO + tm, D_POOL), f32), pltpu.VMEM((CONV_HALO + tm, D_CONV), f32)],
        compiler_params=_params(1),
        name="inproj",
    )(x, g, w, pbd, pscale, cw, cb, dtb, cos, sin)


def _attn_kernel(q_ref, kp_ref, kc_ref, vp_ref, vc_ref, o_ref, acc, mrun, lrun):
    t = pl.program_id(2)
    blk = N_BACK
    qi = lax.broadcasted_iota(jnp.int32, (2 * blk, 2 * blk), 0) % blk
    kj = lax.broadcasted_iota(jnp.int32, (2 * blk, 2 * blk), 1)
    dist = blk + qi - kj
    band = (dist >= 0) & (dist <= N_BACK)
    bias_full = jnp.where(band, 0.0, -jnp.inf).astype(f32)
    bias_cur = jnp.where(band & (kj >= blk), 0.0, -jnp.inf).astype(f32)
    bias_first = jnp.where(t > 0, bias_full, bias_cur)
    head0 = lax.broadcasted_iota(jnp.int32, (blk, LANES), 1) < HEAD_DIM
    head0_k = lax.broadcasted_iota(jnp.int32, (2 * blk, LANES), 1) < HEAD_DIM
    ones_blk = jnp.concatenate([jnp.where(head0_k, 1.0, 0.0), jnp.where(head0_k, 0.0, 1.0)], axis=0).astype(bf16)
    scale = math.log2(math.e) / math.sqrt(HEAD_DIM)
    n_units = ATT_TILE // blk

    def strided(ref, start, n, d):
        return ref[0, 0, pl.ds(start, n, stride=d) if d > 1 else pl.ds(start, n), :]

    def group(d, units):
        first = d == DILATIONS[0]
        last = d == DILATIONS[-1]
        span = blk * d
        rows, scores, vcats = [], [], []
        for u in units:
            sb, r = divmod(u, d)
            qstart = sb * span + r
            rows.append(pl.ds(qstart, blk, stride=d) if d > 1 else pl.ds(qstart, blk))
            qb = (strided(q_ref, qstart, blk, d) * scale).astype(bf16)
            if sb > 0:
                kf = strided(kc_ref, qstart - span, 2 * blk, d)
                vf = strided(vc_ref, qstart - span, 2 * blk, d)
                bias = bias_full
            else:
                kf = jnp.concatenate([strided(kp_ref, ATT_TILE - span + r, blk, d), strided(kc_ref, r, blk, d)], axis=0)
                vf = jnp.concatenate([strided(vp_ref, ATT_TILE - span + r, blk, d), strided(vc_ref, r, blk, d)], axis=0)
                bias = bias_first
            kb = kf.astype(bf16)
            vb = vf.astype(bf16)
            qcat = jnp.concatenate([jnp.where(head0, qb, jnp.zeros_like(qb)), jnp.where(head0, jnp.zeros_like(qb), qb)], axis=0)
            scores.append(lax.dot_general(qcat, kb, (((1,), (1,)), ((), ())), preferred_element_type=f32) + bias)
            vcat = jnp.concatenate([jnp.where(head0_k, vb, jnp.zeros_like(vb)), jnp.where(head0_k, jnp.zeros_like(vb), vb)],
                                   axis=0)
            vcats.append(jnp.concatenate([vcat, ones_blk], axis=1))
        stats = []
        for sc in scores:
            m2 = jnp.max(sc, axis=1, keepdims=True)
            pb = jnp.exp2(sc - m2).astype(bf16)
            stats.append((jnp.concatenate([pb[0:blk], pb[blk:2 * blk]], axis=1), jnp.where(head0, m2[0:blk], m2[blk:2 * blk])))
        pvs = [jnp.dot(pcat, vcat, preferred_element_type=f32) for (pcat, _), vcat in zip(stats, vcats)]
        for qrows, pvl, (_, m_e) in zip(rows, pvs, stats):
            pv, l_e = pvl[:, 0:LANES], pvl[:, LANES:2 * LANES]
            if first:
                acc[qrows, :] = pv
                mrun[qrows, :] = m_e
                lrun[qrows, :] = l_e
            else:
                m_old = mrun[qrows, :]
                m_new = jnp.maximum(m_old, m_e)
                a = jnp.exp2(m_old - m_new)
                b = jnp.exp2(m_e - m_new)
                acc_new = acc[qrows, :] * a + pv * b
                l_new = lrun[qrows, :] * a + l_e * b
                if last:
                    o_ref[0, 0, qrows, :] = (acc_new / l_new).astype(o_ref.dtype)
                else:
                    acc[qrows, :] = acc_new
                    lrun[qrows, :] = l_new
                    mrun[qrows, :] = m_new

    for d in DILATIONS:
        for g0 in range(0, n_units, ATT_UNROLL):
            group(d, range(g0, g0 + ATT_UNROLL))


def _attn_prompt(q, k, v, batch, seq):
    nt = seq // ATT_TILE
    q4 = q.reshape(2, batch, seq, LANES)
    k4 = k.reshape(2, batch, seq, LANES)
    v4 = v.reshape(2, batch, seq, LANES)
    cur = pl.BlockSpec((1, 1, ATT_TILE, LANES), lambda b, s, t: (s, b, t, 0))
    prev = pl.BlockSpec((1, 1, ATT_TILE, LANES), lambda b, s, t: (s, b, jnp.maximum(t - 1, 0), 0))
    out = pl.pallas_call(
        _attn_kernel,
        grid=(batch, 2, nt),
        in_specs=[cur, prev, cur, prev, cur],
        out_specs=cur,
        out_shape=jax.ShapeDtypeStruct((2, batch, seq, LANES), bf16),
        scratch_shapes=[pltpu.VMEM((ATT_TILE, LANES), f32), pltpu.VMEM((ATT_TILE, LANES), f32),
                        pltpu.VMEM((ATT_TILE, LANES), f32)],
        compiler_params=_params(3),
        name="attn",
    )(q4, k4, k4, v4, v4)
    return out.reshape(2, batch * seq, LANES)


def _kv_window_kernel(k_ref, v_ref, *rest):
    ko_ref, vo_ref = rest[-2:]
    ko_ref[0] = k_ref[0, 0].T
    vo_ref[0] = v_ref[0, 0].T


def _kv_window(layer, depth, k, v, batch, seq, prev):
    assert seq % ATT_WIN == 0
    k4 = k.reshape(2, batch, seq, LANES)
    v4 = v.reshape(2, batch, seq, LANES)
    src = pl.BlockSpec((1, 1, ATT_WIN, LANES), lambda b, s: (s, b, seq // ATT_WIN - 1, 0))
    dst = pl.BlockSpec((None, 1, LANES, ATT_WIN), lambda b, s: (layer, b, s, 0))
    shape = jax.ShapeDtypeStruct((depth, batch, D_ATT, ATT_WIN), f32)
    args, in_specs, aliases = [k4, v4], [src, src], {}
    if prev is not None:
        in_specs += [pl.BlockSpec(memory_space=pl.ANY)] * 2
        aliases = {2: 0, 3: 1}
        args += list(prev)
    return pl.pallas_call(
        _kv_window_kernel,
        grid=(batch, 2),
        in_specs=in_specs,
        out_specs=(dst, dst),
        out_shape=(shape, shape),
        input_output_aliases=aliases,
        compiler_params=_params(2),
        name="kv_window",
    )(*args)


def _split3(x):
    hi = x.astype(bf16)
    rest = x - hi.astype(f32)
    mid = rest.astype(bf16)
    lo = (rest - mid.astype(f32)).astype(bf16)
    return hi, mid, lo


def _ssd_kernel(u_ref, dt_ref, z_ref, a2_ref, e3_ref, dskip_ref, norm_ref, y_ref, state_ref, st):
    @pl.when(pl.program_id(1) == 0)
    def _():
        st[...] = jnp.zeros_like(st)

    q = SSM_CHUNK
    nc = u_ref.shape[0] // q
    gl = D_SSM // SSM_GROUPS
    hpg = N_SSM_HEADS // SSM_GROUPS
    li = lax.broadcasted_iota(jnp.int32, (q, q), 0)
    si = lax.broadcasted_iota(jnp.int32, (q, q), 1)
    causal = li >= si
    tril = causal.astype(bf16)
    triu = (li <= si).astype(bf16)
    lo = lax.broadcasted_iota(jnp.int32, (q, LANES), 1) < SSM_HEAD_DIM
    rows = lambda c: slice(c * q, (c + 1) * q)
    b_of = lambda u, g: u[:, D_SSM + g * SSM_STATE:D_SSM + (g + 1) * SSM_STATE]
    c_of = lambda u, g: u[:, D_SSM + (SSM_GROUPS + g) * SSM_STATE:D_SSM + (SSM_GROUPS + g + 1) * SSM_STATE]

    v = {}
    chunks = range(nc)
    groups = range(SSM_GROUPS)
    gsl = lambda g: slice(g * gl, (g + 1) * gl)

    def level1():
        for c in chunks:
            u = u_ref[rows(c), :]
            dtp = dt_ref[rows(c), :]
            dta = dtp * a2_ref[...]
            v["dtp", c] = dtp
            v["cum3", c] = jnp.dot(tril, jnp.concatenate(_split3(dta), axis=1), preferred_element_type=f32)
            v["acs_row", c] = _dot_f32_by_01(dta.T[0:SUBLANES], triu)
            for g in groups:
                v["cb", c, g] = lax.dot_general(c_of(u, g).astype(bf16), b_of(u, g).astype(bf16), (((1,), (1,)), ((), ())),
                                                preferred_element_type=f32)

    def level2():
        for c in chunks:
            cum3 = v["cum3", c]
            cum = cum3[:, 0:LANES] + cum3[:, LANES:2 * LANES] + cum3[:, 2 * LANES:3 * LANES]
            v["both", c] = jnp.dot(jnp.concatenate(_split3(jnp.concatenate([v["dtp", c], cum], axis=0)), axis=1), e3_ref[...],
                                   preferred_element_type=f32)

    def level3():
        for c in chunks:
            u = u_ref[rows(c), :]
            dt_exp, acs_exp, acs_row = v["both", c][0:q], v["both", c][q:2 * q], v["acs_row", c]
            xdt = u[:, 0:D_SSM] * dt_exp
            xdt_b = xdt.astype(bf16)
            parts = []
            for g in groups:
                for pair in range(hpg // 2):
                    mats = []
                    for hh in range(2):
                        h = g * hpg + pair * 2 + hh
                        col = acs_exp[:, h * SSM_HEAD_DIM:h * SSM_HEAD_DIM + 1]
                        diff = jnp.where(causal, col - acs_row[h:h + 1, :], -jnp.inf)
                        mats.append((v["cb", c, g] * jnp.exp2(diff)).astype(bf16))
                    c0 = (g * hpg + pair * 2) * SSM_HEAD_DIM
                    xp = xdt_b[:, c0:c0 + LANES]
                    rhs = jnp.concatenate([jnp.where(lo, xp, jnp.zeros_like(xp)), jnp.where(lo, jnp.zeros_like(xp), xp)],
                                          axis=0)
                    parts.append(jnp.dot(jnp.concatenate(mats, axis=1), rhs, preferred_element_type=f32))
            v["ydiag", c] = jnp.concatenate(parts, axis=1)
            acs_last = acs_exp[q - 1:q, :]
            xw = (xdt * jnp.exp2(acs_last - acs_exp)).astype(bf16)
            v["s_new", c] = jnp.concatenate(
                [jnp.dot(b_of(u, g).T.astype(bf16), xw[:, gsl(g)], preferred_element_type=f32) for g in groups], axis=1)
            v["acs", c] = acs_exp

    def level4():
        state = st[...]
        for c in chunks:
            u = u_ref[rows(c), :]
            state_b = state.astype(bf16)
            v["yoff", c] = jnp.concatenate(
                [jnp.dot(c_of(u, g).astype(bf16), state_b[:, gsl(g)], preferred_element_type=f32) for g in groups], axis=1)
            state = state * jnp.exp2(v["acs", c][q - 1:q, :]) + v["s_new", c]
        st[...] = state
        state_ref[0] = state

    def level5():
        for c in chunks:
            xs = u_ref[rows(c), 0:D_SSM]
            y = v["ydiag", c] + v["yoff", c] * jnp.exp2(v["acs", c])
            y = (y + dskip_ref[...] * xs) * _silu(z_ref[rows(c), :])
            outs = []
            for g in groups:
                yg = y[:, gsl(g)]
                outs.append(yg * lax.rsqrt(jnp.mean(yg * yg, -1, keepdims=True) + RMS_EPS))
            y_ref[rows(c), :] = (jnp.concatenate(outs, axis=1) * norm_ref[...]).astype(y_ref.dtype)

    for level in (level1, level2, level3, level4, level5):
        level()


def _ssd_prompt(u, dt, z, batch, seq, a2_pad, e3_mat, dskip, norm):
    tq = SSD_TILE
    tps = seq // tq
    row = lambda wd: pl.BlockSpec((tq, wd), lambda b, t: (b * tps + t, 0))
    consts = [a2_pad, e3_mat, dskip, norm]
    return pl.pallas_call(
        _ssd_kernel,
        grid=(batch, tps),
        in_specs=[row(D_CONV), row(LANES), row(D_SSM)] + [_const_spec(c.shape) for c in consts],
        out_specs=(row(D_SSM), pl.BlockSpec((1, SSM_STATE, D_SSM), lambda b, t: (b, 0, 0))),
        out_shape=(jax.ShapeDtypeStruct((batch * seq, D_SSM), bf16), jax.ShapeDtypeStruct((batch, SSM_STATE, D_SSM), f32)),
        scratch_shapes=[pltpu.VMEM((SSM_STATE, D_SSM), f32)],
        compiler_params=_params(2),
        name="ssd",
    )(u, dt, z, *consts)


def _sample_pre_kernel(x_ref, g_ref, w_ref, cpool_ref, pbd_ref, pscale_ref, cconv_ref, cw_ref, cb_ref, dtb_ref,
                       cos_ref, sin_ref, ya_ref, q_ref, k_ref, v_ref, z_ref, u_ref, dt_ref, pnew_ref, cnew_ref):
    xn = _rms(x_ref[...], g_ref[...]).astype(bf16)
    proj = jnp.dot(xn, w_ref[...], preferred_element_type=f32)

    xa = proj[:, 0:D_POOL]
    back = lambda i: cpool_ref[POOL_BUF - i]
    s2 = xa + back(1)
    s4 = s2 + back(2) + back(3)
    s8 = s4 + back(4) + back(5) + back(6) + back(7)
    s16 = s8
    for i in range(8, 16):
        s16 = s16 + back(i)
    cnt = jnp.minimum(_pool_window(xa.shape), PAST_LEN + 1).astype(f32)
    dmean = _pool_select(s2, s4, s8, s16) / cnt - xa
    ya = jnp.dot(dmean.astype(bf16), pbd_ref[...], preferred_element_type=f32) * pscale_ref[...]
    ya_ref[...] = ya.astype(ya_ref.dtype)
    for i in range(POOL_BUF - 1):
        pnew_ref[i] = cpool_ref[i + 1]
    pnew_ref[POOL_BUF - 1] = xa

    cos = cos_ref[...]
    sin = sin_ref[...]
    for s in range(2):
        sl = slice(s * LANES, (s + 1) * LANES)
        q_ref[:, sl] = _rope_slab(proj[:, OFF_Q + s * LANES:OFF_Q + (s + 1) * LANES], cos, sin)
        k_ref[:, sl] = _rope_slab(proj[:, OFF_K + s * LANES:OFF_K + (s + 1) * LANES], cos, sin)
    v_ref[...] = proj[:, OFF_V:OFF_V + D_ATT]
    z_ref[...] = proj[:, OFF_Z:OFF_Z + D_SSM]

    xbc = proj[:, OFF_XBC:OFF_XBC + D_CONV]
    conv = cb_ref[...] + xbc * cw_ref[CONV_WIDTH - 1:CONV_WIDTH, :]
    for j in range(1, CONV_WIDTH):
        conv = conv + cconv_ref[CONV_WIDTH - 1 - j] * cw_ref[CONV_WIDTH - 1 - j:CONV_WIDTH - j, :]
    u_ref[...] = _silu(conv)
    for j in range(CONV_WIDTH - 2):
        cnew_ref[j] = cconv_ref[j + 1]
    cnew_ref[CONV_WIDTH - 2] = xbc

    dt_ref[...] = _softplus(proj[:, OFF_DT:OFF_DT + LANES] + dtb_ref[...])


def _sample_pre(x, layer, g, w, cpool, pbd, pscale, cconv, cw, cb, dtb, cos, sin):
    n = x.shape[0]
    args = [x, g, w, cpool, pbd, pscale, cconv, cw, cb, dtb, cos, sin]
    out_shape = (
        jax.ShapeDtypeStruct((n, D_POOL), bf16),
        jax.ShapeDtypeStruct((n, D_ATT), f32),
        jax.ShapeDtypeStruct((n, D_ATT), f32),
        jax.ShapeDtypeStruct((n, D_ATT), f32),
        jax.ShapeDtypeStruct((n, D_SSM), f32),
        jax.ShapeDtypeStruct((n, D_CONV), f32),
        jax.ShapeDtypeStruct((n, LANES), f32),
        jax.ShapeDtypeStruct((POOL_BUF, n, D_POOL), f32),
        jax.ShapeDtypeStruct((CONV_WIDTH - 1, n, D_CONV), f32),
    )
    full = lambda s: pl.BlockSpec(s.shape, lambda i, nd=len(s.shape): (0,) * nd)
    in_specs = [full(a) for a in args]
    in_specs[1] = _layer_spec(g, layer)
    in_specs[2] = _layer_spec(w, layer)
    return pl.pallas_call(
        _sample_pre_kernel,
        grid=(1,),
        in_specs=in_specs,
        out_specs=tuple(full(s) for s in out_shape),
        out_shape=out_shape,
        compiler_params=_params(1),
        name="sample_pre",
    )(*args)


def _sample_ssd_kernel(u_ref, dt_ref, z_ref, st_ref, aexp_ref, e_ref, dskip_ref, norm_ref, *rest):
    y_ref, stnew_ref = rest[-2:]
    n = u_ref.shape[0]
    gl = D_SSM // SSM_GROUPS
    u = u_ref[...]
    xs = u[:, 0:D_SSM]
    dt_exp = _dot_f32_by_01(dt_ref[...], e_ref[...])
    pad = jnp.zeros((LANES - n, D_SSM), f32)
    dec_t = jnp.concatenate([jnp.exp(dt_exp * aexp_ref[...]), pad], axis=0).T
    dtx_t = jnp.concatenate([dt_exp * xs, pad], axis=0).T
    lane = lax.broadcasted_iota(jnp.int32, (D_SSM, LANES), 1)
    y_t = jnp.zeros((D_SSM, LANES), f32)
    for i in range(n):
        dec = dec_t[:, i:i + 1]
        dtx = dtx_t[:, i:i + 1]
        ycols = []
        for g in range(SSM_GROUPS):
            rows = slice(g * gl, (g + 1) * gl)
            b_row = u[i:i + 1, D_SSM + g * SSM_STATE:D_SSM + (g + 1) * SSM_STATE]
            c_row = u[i:i + 1, D_SSM + (SSM_GROUPS + g) * SSM_STATE:D_SSM + (SSM_GROUPS + g + 1) * SSM_STATE]
            h_new = dec[rows] * st_ref[i, rows, :] + dtx[rows] * b_row
            stnew_ref[i, rows, :] = h_new
            ycols.append(jnp.sum(h_new * c_row, axis=1, keepdims=True))
        y_t = jnp.where(lane == i, jnp.concatenate(ycols, axis=0), y_t)
    y = y_t.T[0:n]
    y = (y + dskip_ref[...] * xs) * _silu(z_ref[...])
    outs = []
    for g in range(SSM_GROUPS):
        yg = y[:, g * gl:(g + 1) * gl]
        outs.append(yg * lax.rsqrt(jnp.mean(yg * yg, -1, keepdims=True) + RMS_EPS))
    y_ref[...] = (jnp.concatenate(outs, axis=1) * norm_ref[...]).astype(y_ref.dtype)


def _sample_ssd(layer, u, dt, z, state, prev, a_exp, e_mat, dskip, norm):
    n = u.shape[0]
    full = lambda s: pl.BlockSpec(s.shape, lambda i, nd=len(s.shape): (0,) * nd)
    of_layer = pl.BlockSpec((None,) + state.shape[1:], lambda i: (layer, 0, 0, 0))
    args = [u, dt, z, state, a_exp, e_mat, dskip, norm]
    in_specs = [full(u), full(dt), full(z), of_layer, full(a_exp), full(e_mat), full(dskip), full(norm)]
    aliases = {}
    if prev is not None:
        in_specs.append(pl.BlockSpec(memory_space=pl.ANY))
        aliases = {len(args): 1}
        args.append(prev)
    y_shape = jax.ShapeDtypeStruct((n, D_SSM), bf16)
    return pl.pallas_call(
        _sample_ssd_kernel,
        grid=(1,),
        in_specs=in_specs,
        out_specs=(full(y_shape), of_layer),
        out_shape=(y_shape, jax.ShapeDtypeStruct(state.shape, f32)),
        input_output_aliases=aliases,
        compiler_params=_params(1),
        name="sample_ssd",
    )(*args)


def _sample_attn_kernel(has_prev, q_ref, kn_ref, vn_ref, knc_ref, vnc_ref, kc_ref, vc_ref, *rest):
    if has_prev:
        rest = rest[2:]
    y_ref, ks_ref, vs_ref = rest
    wb = kc_ref.shape[3]
    kc = kc_ref[0, 0]
    vc = vc_ref[0, 0]
    q = q_ref[0]
    kn = kn_ref[0]
    vn = vn_ref[0]
    scale = 1.0 / math.sqrt(HEAD_DIM)

    own = (lax.broadcasted_iota(jnp.int32, (SUBLANES, D_ATT), 1) // HEAD_DIM) == lax.broadcasted_iota(
        jnp.int32, (SUBLANES, D_ATT), 0)
    qbd = jnp.where(own, jnp.broadcast_to(q, (SUBLANES, D_ATT)), 0.0)
    s_all = jnp.dot(qbd.astype(bf16), kc.astype(bf16), preferred_element_type=f32) * scale
    s_new = jnp.sum(jnp.where(own, jnp.broadcast_to(q * kn, (SUBLANES, D_ATT)), 0.0), axis=1, keepdims=True) * scale
    dist = wb - lax.broadcasted_iota(jnp.int32, (SUBLANES, wb), 1)
    ms, ps, ls, es = [], [], [], []
    for d in DILATIONS:
        valid = ((dist % d) == 0) & (dist <= N_BACK * d)
        s = jnp.where(valid, s_all, -jnp.inf)
        m = jnp.maximum(jnp.max(s, axis=1, keepdims=True), s_new)
        p = jnp.exp(s - m)
        e_new = jnp.exp(s_new - m)
        ms.append(m)
        ps.append(p)
        es.append(e_new)
        ls.append(jnp.sum(p, axis=1, keepdims=True) + e_new)
    m_all = jnp.maximum(jnp.maximum(ms[0], ms[1]), ms[2])
    p_all = jnp.zeros_like(ps[0])
    w_new = jnp.zeros_like(s_new)
    l_all = jnp.zeros_like(s_new)
    for m, p, e_new, l in zip(ms, ps, es, ls):
        c = jnp.exp(m - m_all)
        p_all = p_all + c * p
        w_new = w_new + c * e_new
        l_all = l_all + c * l
    o_full = lax.dot_general(p_all.astype(bf16), vc.astype(bf16), (((1,), (1,)), ((), ())), preferred_element_type=f32)
    to_row = lambda t: jnp.sum(jnp.where(own, t, 0.0), axis=0, keepdims=True)
    o_row = to_row(o_full) + to_row(jnp.broadcast_to(w_new, (SUBLANES, D_ATT))) * vn
    y_ref[0] = o_row / to_row(jnp.broadcast_to(l_all, (SUBLANES, D_ATT)))

    newest = lax.broadcasted_iota(jnp.int32, (D_ATT, wb), 1) == wb - 1
    ks_ref[0, 0] = jnp.where(newest, knc_ref[0], pltpu.roll(kc, wb - 1, 1))
    vs_ref[0, 0] = jnp.where(newest, vnc_ref[0], pltpu.roll(vc, wb - 1, 1))


def _sample_attn(layer, q, kn, vn, cache_k, cache_v, prev):
    depth, n, da, wb = cache_k.shape
    rowv = pl.BlockSpec((1, 1, da), lambda i: (i, 0, 0))
    colv = pl.BlockSpec((1, da, 1), lambda i: (i, 0, 0))
    big = pl.BlockSpec((1, 1, da, wb), lambda i: (layer, i, 0, 0))
    in_specs = [rowv, rowv, rowv, colv, colv, big, big]
    args = [q.reshape(n, 1, da), kn.reshape(n, 1, da), vn.reshape(n, 1, da), kn.reshape(n, da, 1), vn.reshape(n, da, 1),
            cache_k, cache_v]
    aliases = {}
    if prev is not None:
        in_specs += [pl.BlockSpec(memory_space=pl.ANY)] * 2
        aliases = {len(args): 1, len(args) + 1: 2}
        args += list(prev)
    return pl.pallas_call(
        functools.partial(_sample_attn_kernel, prev is not None),
        grid=(n,),
        in_specs=in_specs,
        out_specs=(rowv, big, big),
        out_shape=(jax.ShapeDtypeStruct((n, 1, da), f32), jax.ShapeDtypeStruct(cache_k.shape, f32),
                   jax.ShapeDtypeStruct(cache_v.shape, f32)),
        input_output_aliases=aliases,
        compiler_params=_params(1),
        name="sample_attn",
    )(*args)


def _rope_tables(pos):
    half = HEAD_DIM // 2
    inv = ROPE_THETA ** (-jnp.arange(half, dtype=f32) / half)
    ang = pos.astype(f32)[:, None] * inv[None]
    cos = jnp.tile(jnp.cos(ang), (1, LANES // half))
    sin = jnp.sin(ang)
    sin_signed = jnp.tile(jnp.concatenate([-sin, sin], axis=1), (1, LANES // HEAD_DIM))
    return cos, sin_signed


def kernel(x_prompt, x_sample, cache_pool, cache_k, cache_v, state_conv, state_ssm, ffn1_norm, ffn1_w_gate, ffn1_w_up,
           ffn1_w_down, mix_norm, w_in, pool_w, pool_scale, conv_w, conv_b, dt_bias, a_log, d_skip, ssm_norm, w_out,
           ffn2_norm, ffn2_w_gate, ffn2_w_up, ffn2_w_down, final_norm):
    batch, seq, _ = x_prompt.shape
    n_dec = x_sample.shape[0]
    depth = w_in.shape[0]
    wb = cache_k.shape[2]
    assert all(seq % tile == 0 for tile in (ATT_TILE, ROW_TILE, INPROJ_TILE, SSD_TILE))
    assert x_sample.shape[1] == 1 and wb == ATT_WIN

    yp = x_prompt.reshape(batch * seq, D_MODEL)
    ys = x_sample.reshape(n_dec, D_MODEL)
    cos_p, sin_p = _rope_tables(jnp.arange(seq))
    cos_s, sin_s = _rope_tables(jnp.full((1,), PAST_LEN))
    e_mat = (jnp.arange(LANES)[:, None] == (jnp.arange(D_SSM)[None, :] // SSM_HEAD_DIM)).astype(bf16)
    e3_mat = jnp.tile(e_mat, (3, 1))
    ck_t = jnp.transpose(cache_k, (0, 1, 3, 4, 2)).reshape(depth, n_dec, D_ATT, wb)
    cv_t = jnp.transpose(cache_v, (0, 1, 3, 4, 2)).reshape(depth, n_dec, D_ATT, wb)
    row2 = lambda a: a.reshape(1, -1)
    row3 = lambda a: a.reshape(depth, 1, -1)

    f1 = (row3(ffn1_norm), ffn1_w_gate.astype(bf16), ffn1_w_up.astype(bf16), ffn1_w_down.astype(bf16))
    f2 = (row3(ffn2_norm), ffn2_w_gate.astype(bf16), ffn2_w_up.astype(bf16), ffn2_w_down.astype(bf16))
    w_all = jnp.concatenate([w_in[:, :, 0:D_POOL],
                             jnp.pad(w_in[:, :, D_IN_PROJ - N_SSM_HEADS:], ((0, 0), (0, 0), (0, LANES - N_SSM_HEADS))),
                             w_in[:, :, D_POOL:D_IN_PROJ - N_SSM_HEADS]], axis=2).astype(bf16)
    wo_all = w_out.astype(bf16)
    g_mix = row3(mix_norm)

    st_p = [[] for _ in range(3)]
    st_s = [[] for _ in range(2)]
    kv_p = kv_s = ssm_s = None
    for i in range(depth):
        last = i == depth - 1
        pbd = jax.scipy.linalg.block_diag(*[pool_w[i, g] for g in range(len(POOL_WINDOWS))]).astype(bf16)
        pscale = row2(pool_scale[i])
        cb = row2(conv_b[i])
        dtb = row2(jnp.pad(dt_bias[i], (0, LANES - N_SSM_HEADS)))
        a_neg = -jnp.exp(a_log[i].astype(f32))
        a2_pad = row2(jnp.pad(a_neg * math.log2(math.e), (0, LANES - N_SSM_HEADS)))
        a_exp = row2(jnp.repeat(a_neg, SSM_HEAD_DIM))
        dskip = row2(jnp.repeat(d_skip[i], SSM_HEAD_DIM))
        norm = row2(ssm_norm[i])
        fin = row2(final_norm) if last else None

        yp = _ffn(yp, i, *f1)
        ya, q, k, v, z, u, dt, pst, cst = _inproj_prompt(yp, i, batch, seq, g_mix, w_all, pbd, pscale, conv_w[i], cb, dtb,
                                                         cos_p, sin_p)
        yb = _attn_prompt(q, k, v, batch, seq)
        yc, sst = _ssd_prompt(u, dt, z, batch, seq, a2_pad, e3_mat, dskip, norm)
        yp = _ffn(yp, i, *f2, mix=(ya, yb, yc, wo_all), final_g=fin)

        kv_p = _kv_window(i, depth, k, v, batch, seq, kv_p)
        st_p[0].append(pst[:, POOL_HALO - POOL_BUF:])
        st_p[1].append(cst[:, CONV_HALO - (CONV_WIDTH - 1):])
        st_p[2].append(jnp.transpose(sst.reshape(batch, SSM_STATE, N_SSM_HEADS, SSM_HEAD_DIM), (0, 2, 3, 1)))

        ys = _ffn(ys, i, *f1)
        cpool = jnp.transpose(cache_pool[i], (1, 0, 2))
        cconv = jnp.transpose(state_conv[i], (1, 0, 2))
        ya_s, q_s, k_s, v_s, z_s, u_s, dt_s, pnew, cnew = _sample_pre(ys, i, g_mix, w_all, cpool, pbd, pscale, cconv,
                                                                    conv_w[i], cb, dtb, cos_s, sin_s)
        yb_s, ks_buf, vs_buf = _sample_attn(i, q_s, k_s, v_s, ck_t, cv_t, kv_s)
        kv_s = (ks_buf, vs_buf)
        yc_s, ssm_s = _sample_ssd(i, u_s, dt_s, z_s, state_ssm.reshape(depth, n_dec, D_SSM, SSM_STATE), ssm_s, a_exp, e_mat,
                                  dskip, norm)
        yb_slab = jnp.transpose(yb_s.reshape(n_dec, 2, LANES), (1, 0, 2)).astype(bf16)
        ys = _ffn(ys, i, *f2, mix=(ya_s, yb_slab, yc_s, wo_all), final_g=fin)
        st_s[0].append(jnp.transpose(pnew, (1, 0, 2)))
        st_s[1].append(jnp.transpose(cnew, (1, 0, 2)))

    y_prompt = yp.reshape(batch, seq, D_MODEL)
    y_sample = ys.reshape(n_dec, 1, D_MODEL)
    pool_p, conv_p, ssm_p = (jnp.stack(s, 0) for s in st_p)
    pool_s, conv_s = (jnp.stack(s, 0) for s in st_s)
    ssm_s = ssm_s.reshape(state_ssm.shape)
    from_t = lambda t: jnp.transpose(t.reshape(depth, -1, N_ATT_HEADS, HEAD_DIM, wb), (0, 1, 4, 2, 3))
    k_p, v_p = from_t(kv_p[0]), from_t(kv_p[1])
    return (y_prompt, y_sample, pool_p, pool_s, k_p, from_t(kv_s[0]), v_p, from_t(kv_s[1]), conv_p, conv_s, ssm_p, ssm_s)
```

```python
import functools
import math

import jax
import jax.numpy as jnp
from jax import lax
from jax.experimental import pallas as pl
from jax.experimental.pallas import tpu as pltpu

f32 = jnp.float32
bf16 = jnp.bfloat16

D_MODEL = 1024
PAST_LEN = 16384
POOL_WINDOWS = (2, 4, 8, 16)
D_POOL = 256
POOL_GROUP = D_POOL // len(POOL_WINDOWS)
POOL_BUF = max(POOL_WINDOWS) - 1
HEAD_DIM = 64
N_ATT_HEADS = 4
D_ATT = N_ATT_HEADS * HEAD_DIM
DILATIONS = (16, 4, 1)
N_BACK = 128
ATT_WIN = 2048
ROPE_THETA = 10000.0
ATT_SCALE = math.log2(math.e) / math.sqrt(HEAD_DIM)
D_SSM = 512
SSM_HEAD_DIM = 64
N_SSM_HEADS = 8
SSM_STATE = 128
SSM_GROUPS = 2
CONV_WIDTH = 4
SSM_CHUNK = 128
D_CONV = D_SSM + 2 * SSM_GROUPS * SSM_STATE
D_IN_PROJ = D_POOL + 3 * D_ATT + D_SSM + D_CONV + N_SSM_HEADS
D_FF = 2816
RMS_EPS = 1e-6

LANES = 128
SUBLANES = 8
VMEM_LIMIT = 56 * 1024 * 1024

OFF_Q = D_POOL
OFF_K = OFF_Q + D_ATT
OFF_V = OFF_K + D_ATT
OFF_Z = OFF_V + D_ATT
OFF_XBC = OFF_Z + D_SSM
OFF_DT = OFF_XBC + D_CONV
N_PROJ = OFF_DT + LANES

ROW_TILE = 512
FFN_SPLIT = 2
SSD_TILE = 1024
ATT_TILE = 2048
ATT_UNROLL = 4
INPROJ_TILE = 1024
INPROJ_SPLIT = 4
POOL_HALO = 16
CONV_HALO = 8


def _params(n_axes, flags=None):
    return pltpu.CompilerParams(dimension_semantics=("arbitrary",) * n_axes, vmem_limit_bytes=VMEM_LIMIT, flags=flags)


def _const_spec(shape):
    nd = len(shape)
    return pl.BlockSpec(shape, lambda *_: (0,) * nd, pipeline_mode=pl.Buffered(1))


def _layer_spec(arr, layer):
    nd = arr.ndim - 1
    return pl.BlockSpec((None,) + arr.shape[1:], lambda *_: (layer,) + (0,) * nd, pipeline_mode=pl.Buffered(1))


def _rms(x, g):
    return x * lax.rsqrt(jnp.mean(x * x, -1, keepdims=True) + RMS_EPS) * g


def _dot_f32_by_01(x, m01, lhs_is_01=False):
    hi = x.astype(bf16)
    rest = x - hi.astype(f32)
    mid = rest.astype(bf16)
    lo = (rest - mid.astype(f32)).astype(bf16)
    out = None
    for part in (hi, mid, lo):
        if lhs_is_01:
            t = jnp.dot(m01, part, preferred_element_type=f32)
        else:
            t = jnp.dot(part, m01, preferred_element_type=f32)
        out = t if out is None else out + t
    return out


def _silu(x):
    return x * jax.nn.sigmoid(x)


def _softplus(x):
    return jnp.maximum(x, 0.0) + jnp.log1p(jnp.exp(-jnp.abs(x)))


def _rope_slab(x, cos, sin_signed):
    lane = lax.broadcasted_iota(jnp.int32, x.shape, 1)
    first_half = (lane % HEAD_DIM) < (HEAD_DIM // 2)
    partner = jnp.where(first_half, pltpu.roll(x, LANES - HEAD_DIM // 2, 1), pltpu.roll(x, HEAD_DIM // 2, 1))
    return x * cos + partner * sin_signed


def _pool_select(s2, s4, s8, s16):
    lane = lax.broadcasted_iota(jnp.int32, s2.shape, 1)
    return jnp.where(lane < POOL_GROUP, s2, jnp.where(lane < 2 * POOL_GROUP, s4, jnp.where(lane < 3 * POOL_GROUP, s8, s16)))


def _pool_window(shape):
    lane = lax.broadcasted_iota(jnp.int32, shape, 1)
    return jnp.where(lane < POOL_GROUP, POOL_WINDOWS[0],
                     jnp.where(lane < 2 * POOL_GROUP, POOL_WINDOWS[1],
                               jnp.where(lane < 3 * POOL_GROUP, POOL_WINDOWS[2], POOL_WINDOWS[3])))


def _ffn_kernel(has_mix, has_final, *refs):
    refs = list(refs)
    x_ref = refs.pop(0)
    if has_mix:
        ya_ref, yb_ref, yc_ref, wo_ref = refs[:4]
        refs = refs[4:]
    g_ref, wg_ref, wu_ref, wd_ref = refs[:4]
    refs = refs[4:]
    if has_final:
        fg_ref = refs.pop(0)
    o_ref = refs.pop(0)

    tm = x_ref.shape[0]
    n_sub = FFN_SPLIT if tm % (FFN_SPLIT * SUBLANES * 2) == 0 else 1
    subs = [slice(h * (tm // n_sub), (h + 1) * (tm // n_sub)) for h in range(n_sub)]
    xs = [x_ref[rows, :] for rows in subs]
    if has_mix:
        cats = [jnp.concatenate([ya_ref[rows, :], yb_ref[0, rows, :], yb_ref[1, rows, :], yc_ref[rows, :]], axis=1)
                for rows in subs]
        xs = [x + jnp.dot(cat, wo_ref[...], preferred_element_type=f32) for x, cat in zip(xs, cats)]
    xns = [_rms(x, g_ref[...]).astype(bf16) for x in xs]
    gates = [jnp.dot(xn, wg_ref[...], preferred_element_type=f32) for xn in xns]
    ups = [jnp.dot(xn, wu_ref[...], preferred_element_type=f32) for xn in xns]
    hs = [(_silu(gate) * up).astype(bf16) for gate, up in zip(gates, ups)]
    downs = [jnp.dot(h, wd_ref[...], preferred_element_type=f32) for h in hs]
    for rows, x, down in zip(subs, xs, downs):
        y = x + 0.5 * down
        if has_final:
            y = _rms(y, fg_ref[...])
        o_ref[rows, :] = y


def _ffn(x, layer, g, wg, wu, wd, mix=None, final_g=None):
    m = x.shape[0]
    tm = min(ROW_TILE, m)
    assert m % tm == 0
    row = lambda w: pl.BlockSpec((tm, w), lambda i: (i, 0))
    args, specs = [x], [row(D_MODEL)]
    if mix is not None:
        ya, yb, yc, wo = mix
        args += [ya, yb, yc, wo]
        specs += [row(D_POOL), pl.BlockSpec((2, tm, LANES), lambda i: (0, i, 0)), row(D_SSM), _layer_spec(wo, layer)]
    args += [g, wg, wu, wd]
    specs += [_layer_spec(g, layer), _layer_spec(wg, layer), _layer_spec(wu, layer), _layer_spec(wd, layer)]
    if final_g is not None:
        args.append(final_g)
        specs.append(_const_spec(final_g.shape))
    return pl.pallas_call(
        functools.partial(_ffn_kernel, mix is not None, final_g is not None),
        grid=(m // tm,),
        in_specs=specs,
        out_specs=row(D_MODEL),
        out_shape=jax.ShapeDtypeStruct((m, D_MODEL), f32),
        compiler_params=_params(1),
        name="ffn",
    )(*args)


def _inproj_kernel(tiles_per_seq, x_ref, g_ref, w_ref, pbd_ref, pscale_ref, cw_ref, cb_ref, dtb_ref, cos_ref, sin_ref,
                   cosq_ref, sinq_ref,
                   ya_ref, q_ref, k_ref, v_ref, z_ref, u_ref, dt_ref, pst_ref, cst_ref, xa_buf, xbc_buf):
    tm = x_ref.shape[0]
    n_sub = INPROJ_SPLIT
    sub = tm // n_sub
    t = pl.program_id(0) % tiles_per_seq

    @pl.when(t == 0)
    def _():
        xa_buf[0:POOL_HALO, :] = jnp.zeros((POOL_HALO, D_POOL), f32)
        xbc_buf[0:CONV_HALO, :] = jnp.zeros((CONV_HALO, D_CONV), f32)

    groups = (("xbc_dt", OFF_XBC, N_PROJ), ("xa", 0, OFF_Q), ("qkv", OFF_Q, OFF_Z), ("z", OFF_Z, OFF_XBC))
    projs = []
    for h in range(n_sub):
        xn = _rms(x_ref[h * sub:(h + 1) * sub, :], g_ref[...]).astype(bf16)
        projs.append({name: jnp.dot(xn, w_ref[:, a:b], preferred_element_type=f32) for name, a, b in groups})

    for h in range(n_sub):
        r0 = h * sub
        rows = slice(r0, r0 + sub)
        xbc, xa, qkv = projs[h]["xbc_dt"][:, 0:D_CONV], projs[h]["xa"], projs[h]["qkv"]

        xbc_buf[CONV_HALO + r0:CONV_HALO + r0 + sub, :] = xbc
        conv = cb_ref[...] + xbc * cw_ref[CONV_WIDTH - 1:CONV_WIDTH, :]
        for j in range(1, CONV_WIDTH):
            conv = conv + xbc_buf[CONV_HALO + r0 - j:CONV_HALO + r0 - j + sub, :] * cw_ref[CONV_WIDTH - 1 - j:CONV_WIDTH - j, :]
        u_ref[rows, :] = _silu(conv)

        dt_ref[rows, :] = _softplus(projs[h]["xbc_dt"][:, D_CONV:D_CONV + LANES] + dtb_ref[...])
        xa_buf[POOL_HALO + r0:POOL_HALO + r0 + sub, :] = xa
        ext = xa_buf[r0:r0 + sub + POOL_HALO, :]
        s2 = ext + pltpu.roll(ext, 1, 0)
        s4 = s2 + pltpu.roll(s2, 2, 0)
        s8 = s4 + pltpu.roll(s4, 4, 0)
        s16 = s8 + pltpu.roll(s8, 8, 0)
        sums = _pool_select(s2, s4, s8, s16)[POOL_HALO:POOL_HALO + sub, :]
        pos = t * tm + r0 + lax.broadcasted_iota(jnp.int32, (sub, D_POOL), 0)
        cnt = jnp.minimum(_pool_window((sub, D_POOL)), pos + 1).astype(f32)
        dmean = sums / cnt - xa
        ya = jnp.dot(dmean.astype(bf16), pbd_ref[...], preferred_element_type=f32) * pscale_ref[...]
        ya_ref[rows, :] = ya.astype(ya_ref.dtype)

        cos = cos_ref[rows, :]
        sin = sin_ref[rows, :]
        cosq = cosq_ref[rows, :]
        sinq = sinq_ref[rows, :]
        for s in range(2):
            q_ref[s, rows, :] = _rope_slab(qkv[:, s * LANES:(s + 1) * LANES], cosq, sinq)
            k_ref[s, rows, :] = _rope_slab(qkv[:, D_ATT + s * LANES:D_ATT + (s + 1) * LANES], cos, sin)
            v_ref[s, rows, :] = qkv[:, 2 * D_ATT + s * LANES:2 * D_ATT + (s + 1) * LANES]
        z_ref[rows, :] = projs[h]["z"]

    last_xbc = xbc_buf[tm:tm + CONV_HALO, :]
    cst_ref[0] = last_xbc
    xbc_buf[0:CONV_HALO, :] = last_xbc
    last_xa = xa_buf[tm:tm + POOL_HALO, :]
    pst_ref[0] = last_xa
    xa_buf[0:POOL_HALO, :] = last_xa


def _inproj_prompt(x, layer, batch, seq, g, w, pbd, pscale, cw, cb, dtb, cos, sin, cosq, sinq):
    m = x.shape[0]
    tm = INPROJ_TILE
    tps = seq // tm
    row = lambda wd: pl.BlockSpec((tm, wd), lambda i: (i, 0))
    slab = pl.BlockSpec((2, tm, LANES), lambda i: (0, i, 0))
    tab = pl.BlockSpec((tm, LANES), lambda i: (i % tps, 0))
    out_shape = (
        jax.ShapeDtypeStruct((m, D_POOL), bf16),
        jax.ShapeDtypeStruct((2, m, LANES), f32),
        jax.ShapeDtypeStruct((2, m, LANES), f32),
        jax.ShapeDtypeStruct((2, m, LANES), f32),
        jax.ShapeDtypeStruct((m, D_SSM), f32),
        jax.ShapeDtypeStruct((m, D_CONV), f32),
        jax.ShapeDtypeStruct((m, LANES), f32),
        jax.ShapeDtypeStruct((batch, POOL_HALO, D_POOL), f32),
        jax.ShapeDtypeStruct((batch, CONV_HALO, D_CONV), f32),
    )
    out_specs = (
        row(D_POOL), slab, slab, slab, row(D_SSM), row(D_CONV), row(LANES),
        pl.BlockSpec((1, POOL_HALO, D_POOL), lambda i: (i // tps, 0, 0)),
        pl.BlockSpec((1, CONV_HALO, D_CONV), lambda i: (i // tps, 0, 0)),
    )
    in_specs = [row(D_MODEL), _layer_spec(g, layer), _layer_spec(w, layer), _const_spec(pbd.shape), _const_spec(pscale.shape),
                _const_spec(cw.shape), _const_spec(cb.shape), _const_spec(dtb.shape), tab, tab, tab, tab]
    return pl.pallas_call(
        functools.partial(_inproj_kernel, tps),
        grid=(m // tm,),
        in_specs=in_specs,
        out_specs=out_specs,
        out_shape=out_shape,
        scratch_shapes=[pltpu.VMEM((POOL_HALO + tm, D_POOL), f32), pltpu.VMEM((CONV_HALO + tm, D_CONV), f32)],
        compiler_params=_params(1),
        name="inproj",
    )(x, g, w, pbd, pscale, cw, cb, dtb, cos, sin, cosq, sinq)


def _attn_kernel(q_ref, kp_ref, kc_ref, vp_ref, vc_ref, o_ref, acc, mrun, lrun):
    t = pl.program_id(2)
    blk = N_BACK
    qi = lax.broadcasted_iota(jnp.int32, (2 * blk, 2 * blk), 0) % blk
    kj = lax.broadcasted_iota(jnp.int32, (2 * blk, 2 * blk), 1)
    dist = blk + qi - kj
    band = (dist >= 0) & (dist <= N_BACK)
    bias_full = jnp.where(band, 0.0, -jnp.inf).astype(f32)
    bias_cur = jnp.where(band & (kj >= blk), 0.0, -jnp.inf).astype(f32)
    bias_first = jnp.where(t > 0, bias_full, bias_cur)
    head0 = lax.broadcasted_iota(jnp.int32, (blk, LANES), 1) < HEAD_DIM
    head0_k = lax.broadcasted_iota(jnp.int32, (2 * blk, LANES), 1) < HEAD_DIM
    ones_blk = jnp.concatenate([jnp.where(head0_k, 1.0, 0.0), jnp.where(head0_k, 0.0, 1.0)], axis=0).astype(bf16)
    n_units = ATT_TILE // blk

    def strided(ref, start, n, d):
        return ref[0, 0, pl.ds(start, n, stride=d) if d > 1 else pl.ds(start, n), :]

    def group(d, units):
        first = d == DILATIONS[0]
        last = d == DILATIONS[-1]
        span = blk * d
        rows, scores, vcats = [], [], []
        for u in units:
            sb, r = divmod(u, d)
            qstart = sb * span + r
            rows.append(pl.ds(qstart, blk, stride=d) if d > 1 else pl.ds(qstart, blk))
            qb = strided(q_ref, qstart, blk, d).astype(bf16)
            if sb > 0:
                kf = strided(kc_ref, qstart - span, 2 * blk, d)
                vf = strided(vc_ref, qstart - span, 2 * blk, d)
                bias = bias_full
            else:
                kf = jnp.concatenate([strided(kp_ref, ATT_TILE - span + r, blk, d), strided(kc_ref, r, blk, d)], axis=0)
                vf = jnp.concatenate([strided(vp_ref, ATT_TILE - span + r, blk, d), strided(vc_ref, r, blk, d)], axis=0)
                bias = bias_first
            kb = kf.astype(bf16)
            vb = vf.astype(bf16)
            qcat = jnp.concatenate([jnp.where(head0, qb, jnp.zeros_like(qb)), jnp.where(head0, jnp.zeros_like(qb), qb)], axis=0)
            scores.append(lax.dot_general(qcat, kb, (((1,), (1,)), ((), ())), preferred_element_type=f32) + bias)
            vcat = jnp.concatenate([jnp.where(head0_k, vb, jnp.zeros_like(vb)), jnp.where(head0_k, jnp.zeros_like(vb), vb)],
                                   axis=0)
            vcats.append(jnp.concatenate([vcat, ones_blk], axis=1))
        stats = []
        for sc in scores:
            m2 = jnp.max(sc, axis=1, keepdims=True)
            pb = jnp.exp2(sc - m2).astype(bf16)
            stats.append((jnp.concatenate([pb[0:blk], pb[blk:2 * blk]], axis=1), jnp.where(head0, m2[0:blk], m2[blk:2 * blk])))
        pvs = [jnp.dot(pcat, vcat, preferred_element_type=f32) for (pcat, _), vcat in zip(stats, vcats)]
        for qrows, pvl, (_, m_e) in zip(rows, pvs, stats):
            pv, l_e = pvl[:, 0:LANES], pvl[:, LANES:2 * LANES]
            if first:
                acc[qrows, :] = pv
                mrun[qrows, :] = m_e
                lrun[qrows, :] = l_e
            else:
                m_old = mrun[qrows, :]
                m_new = jnp.maximum(m_old, m_e)
                a = jnp.exp2(m_old - m_new)
                b = jnp.exp2(m_e - m_new)
                acc_new = acc[qrows, :] * a + pv * b
                l_new = lrun[qrows, :] * a + l_e * b
                if last:
                    o_ref[0, 0, qrows, :] = (acc_new / l_new).astype(o_ref.dtype)
                else:
                    acc[qrows, :] = acc_new
                    lrun[qrows, :] = l_new
                    mrun[qrows, :] = m_new

    for d in DILATIONS:
        for g0 in range(0, n_units, ATT_UNROLL):
            group(d, range(g0, g0 + ATT_UNROLL))


def _attn_prompt(q, k, v, batch, seq):
    nt = seq // ATT_TILE
    q4 = q.reshape(2, batch, seq, LANES)
    k4 = k.reshape(2, batch, seq, LANES)
    v4 = v.reshape(2, batch, seq, LANES)
    cur = pl.BlockSpec((1, 1, ATT_TILE, LANES), lambda b, s, t: (s, b, t, 0))
    prev = pl.BlockSpec((1, 1, ATT_TILE, LANES), lambda b, s, t: (s, b, jnp.maximum(t - 1, 0), 0))
    out = pl.pallas_call(
        _attn_kernel,
        grid=(batch, 2, nt),
        in_specs=[cur, prev, cur, prev, cur],
        out_specs=cur,
        out_shape=jax.ShapeDtypeStruct((2, batch, seq, LANES), bf16),
        scratch_shapes=[pltpu.VMEM((ATT_TILE, LANES), f32), pltpu.VMEM((ATT_TILE, LANES), f32),
                        pltpu.VMEM((ATT_TILE, LANES), f32)],
        compiler_params=_params(3),
        name="attn",
    )(q4, k4, k4, v4, v4)
    return out.reshape(2, batch * seq, LANES)


def _kv_window_kernel(k_ref, v_ref, *rest):
    ko_ref, vo_ref = rest[-2:]
    ko_ref[0] = k_ref[0, 0].T
    vo_ref[0] = v_ref[0, 0].T


def _kv_window(layer, depth, k, v, batch, seq, prev):
    assert seq % ATT_WIN == 0
    k4 = k.reshape(2, batch, seq, LANES)
    v4 = v.reshape(2, batch, seq, LANES)
    src = pl.BlockSpec((1, 1, ATT_WIN, LANES), lambda b, s: (s, b, seq // ATT_WIN - 1, 0))
    dst = pl.BlockSpec((None, 1, LANES, ATT_WIN), lambda b, s: (layer, b, s, 0))
    shape = jax.ShapeDtypeStruct((depth, batch, D_ATT, ATT_WIN), f32)
    args, in_specs, aliases = [k4, v4], [src, src], {}
    if prev is not None:
        in_specs += [pl.BlockSpec(memory_space=pl.ANY)] * 2
        aliases = {2: 0, 3: 1}
        args += list(prev)
    return pl.pallas_call(
        _kv_window_kernel,
        grid=(batch, 2),
        in_specs=in_specs,
        out_specs=(dst, dst),
        out_shape=(shape, shape),
        input_output_aliases=aliases,
        compiler_params=_params(2),
        name="kv_window",
    )(*args)


def _split3(x):
    hi = x.astype(bf16)
    rest = x - hi.astype(f32)
    mid = rest.astype(bf16)
    lo = (rest - mid.astype(f32)).astype(bf16)
    return hi, mid, lo


def _ssd_kernel(u_ref, dt_ref, z_ref, a2_ref, e3_ref, dskip_ref, norm_ref, y_ref, state_ref, st):
    @pl.when(pl.program_id(1) == 0)
    def _():
        st[...] = jnp.zeros_like(st)

    q = SSM_CHUNK
    nc = u_ref.shape[0] // q
    gl = D_SSM // SSM_GROUPS
    hpg = N_SSM_HEADS // SSM_GROUPS
    li = lax.broadcasted_iota(jnp.int32, (q, q), 0)
    si = lax.broadcasted_iota(jnp.int32, (q, q), 1)
    causal = li >= si
    tril = causal.astype(bf16)
    triu = (li <= si).astype(bf16)
    lo = lax.broadcasted_iota(jnp.int32, (q, LANES), 1) < SSM_HEAD_DIM
    rows = lambda c: slice(c * q, (c + 1) * q)
    b_of = lambda u, g: u[:, D_SSM + g * SSM_STATE:D_SSM + (g + 1) * SSM_STATE]
    c_of = lambda u, g: u[:, D_SSM + (SSM_GROUPS + g) * SSM_STATE:D_SSM + (SSM_GROUPS + g + 1) * SSM_STATE]

    v = {}
    chunks = range(nc)
    groups = range(SSM_GROUPS)
    gsl = lambda g: slice(g * gl, (g + 1) * gl)

    def level1():
        for c in chunks:
            u = u_ref[rows(c), :]
            dtp = dt_ref[rows(c), :]
            dta = dtp * a2_ref[...]
            v["dtp", c] = dtp
            v["cum3", c] = jnp.dot(tril, jnp.concatenate(_split3(dta), axis=1), preferred_element_type=f32)
            v["acs_row", c] = _dot_f32_by_01(dta.T[0:SUBLANES], triu)
            for g in groups:
                v["cb", c, g] = lax.dot_general(c_of(u, g).astype(bf16), b_of(u, g).astype(bf16), (((1,), (1,)), ((), ())),
                                                preferred_element_type=f32)

    def level2():
        for c in chunks:
            cum3 = v["cum3", c]
            cum = cum3[:, 0:LANES] + cum3[:, LANES:2 * LANES] + cum3[:, 2 * LANES:3 * LANES]
            v["both", c] = jnp.dot(jnp.concatenate(_split3(jnp.concatenate([v["dtp", c], cum], axis=0)), axis=1), e3_ref[...],
                                   preferred_element_type=f32)

    def level3():
        for c in chunks:
            u = u_ref[rows(c), :]
            dt_exp, acs_exp, acs_row = v["both", c][0:q], v["both", c][q:2 * q], v["acs_row", c]
            xdt = u[:, 0:D_SSM] * dt_exp
            xdt_b = xdt.astype(bf16)
            parts = []
            for g in groups:
                for pair in range(hpg // 2):
                    mats = []
                    for hh in range(2):
                        h = g * hpg + pair * 2 + hh
                        col = acs_exp[:, h * SSM_HEAD_DIM:h * SSM_HEAD_DIM + 1]
                        diff = jnp.where(causal, col - acs_row[h:h + 1, :], -jnp.inf)
                        mats.append((v["cb", c, g] * jnp.exp2(diff)).astype(bf16))
                    c0 = (g * hpg + pair * 2) * SSM_HEAD_DIM
                    xp = xdt_b[:, c0:c0 + LANES]
                    rhs = jnp.concatenate([jnp.where(lo, xp, jnp.zeros_like(xp)), jnp.where(lo, jnp.zeros_like(xp), xp)],
                                          axis=0)
                    parts.append(jnp.dot(jnp.concatenate(mats, axis=1), rhs, preferred_element_type=f32))
            v["ydiag", c] = jnp.concatenate(parts, axis=1)
            acs_last = acs_exp[q - 1:q, :]
            xw = (xdt * jnp.exp2(acs_last - acs_exp)).astype(bf16)
            v["s_new", c] = jnp.concatenate(
                [jnp.dot(b_of(u, g).T.astype(bf16), xw[:, gsl(g)], preferred_element_type=f32) for g in groups], axis=1)
            v["acs", c] = acs_exp

    def level4():
        state = st[...]
        for c in chunks:
            u = u_ref[rows(c), :]
            state_b = state.astype(bf16)
            v["yoff", c] = jnp.concatenate(
                [jnp.dot(c_of(u, g).astype(bf16), state_b[:, gsl(g)], preferred_element_type=f32) for g in groups], axis=1)
            state = state * jnp.exp2(v["acs", c][q - 1:q, :]) + v["s_new", c]
        st[...] = state
        state_ref[0] = state

    def level5():
        for c in chunks:
            xs = u_ref[rows(c), 0:D_SSM]
            y = v["ydiag", c] + v["yoff", c] * jnp.exp2(v["acs", c])
            y = (y + dskip_ref[...] * xs) * _silu(z_ref[rows(c), :])
            outs = []
            for g in groups:
                yg = y[:, gsl(g)]
                outs.append(yg * lax.rsqrt(jnp.mean(yg * yg, -1, keepdims=True) + RMS_EPS))
            y_ref[rows(c), :] = (jnp.concatenate(outs, axis=1) * norm_ref[...]).astype(y_ref.dtype)

    for level in (level1, level2, level3, level4, level5):
        level()


def _ssd_prompt(u, dt, z, batch, seq, a2_pad, e3_mat, dskip, norm):
    tq = SSD_TILE
    tps = seq // tq
    row = lambda wd: pl.BlockSpec((tq, wd), lambda b, t: (b * tps + t, 0))
    consts = [a2_pad, e3_mat, dskip, norm]
    return pl.pallas_call(
        _ssd_kernel,
        grid=(batch, tps),
        in_specs=[row(D_CONV), row(LANES), row(D_SSM)] + [_const_spec(c.shape) for c in consts],
        out_specs=(row(D_SSM), pl.BlockSpec((1, SSM_STATE, D_SSM), lambda b, t: (b, 0, 0))),
        out_shape=(jax.ShapeDtypeStruct((batch * seq, D_SSM), bf16), jax.ShapeDtypeStruct((batch, SSM_STATE, D_SSM), f32)),
        scratch_shapes=[pltpu.VMEM((SSM_STATE, D_SSM), f32)],
        compiler_params=_params(2),
        name="ssd",
    )(u, dt, z, *consts)


def _sample_pre_kernel(x_ref, g_ref, w_ref, cpool_ref, pbd_ref, pscale_ref, cconv_ref, cw_ref, cb_ref, dtb_ref,
                       cos_ref, sin_ref, ya_ref, q_ref, k_ref, v_ref, z_ref, u_ref, dt_ref, pnew_ref, cnew_ref):
    xn = _rms(x_ref[...], g_ref[...]).astype(bf16)
    proj = jnp.dot(xn, w_ref[...], preferred_element_type=f32)

    xa = proj[:, 0:D_POOL]
    back = lambda i: cpool_ref[POOL_BUF - i]
    s2 = xa + back(1)
    s4 = s2 + back(2) + back(3)
    s8 = s4 + back(4) + back(5) + back(6) + back(7)
    s16 = s8
    for i in range(8, 16):
        s16 = s16 + back(i)
    cnt = jnp.minimum(_pool_window(xa.shape), PAST_LEN + 1).astype(f32)
    dmean = _pool_select(s2, s4, s8, s16) / cnt - xa
    ya = jnp.dot(dmean.astype(bf16), pbd_ref[...], preferred_element_type=f32) * pscale_ref[...]
    ya_ref[...] = ya.astype(ya_ref.dtype)
    for i in range(POOL_BUF - 1):
        pnew_ref[i] = cpool_ref[i + 1]
    pnew_ref[POOL_BUF - 1] = xa

    cos = cos_ref[...]
    sin = sin_ref[...]
    for s in range(2):
        sl = slice(s * LANES, (s + 1) * LANES)
        q_ref[:, sl] = _rope_slab(proj[:, OFF_Q + s * LANES:OFF_Q + (s + 1) * LANES], cos, sin)
        k_ref[:, sl] = _rope_slab(proj[:, OFF_K + s * LANES:OFF_K + (s + 1) * LANES], cos, sin)
    v_ref[...] = proj[:, OFF_V:OFF_V + D_ATT]
    z_ref[...] = proj[:, OFF_Z:OFF_Z + D_SSM]

    xbc = proj[:, OFF_XBC:OFF_XBC + D_CONV]
    conv = cb_ref[...] + xbc * cw_ref[CONV_WIDTH - 1:CONV_WIDTH, :]
    for j in range(1, CONV_WIDTH):
        conv = conv + cconv_ref[CONV_WIDTH - 1 - j] * cw_ref[CONV_WIDTH - 1 - j:CONV_WIDTH - j, :]
    u_ref[...] = _silu(conv)
    for j in range(CONV_WIDTH - 2):
        cnew_ref[j] = cconv_ref[j + 1]
    cnew_ref[CONV_WIDTH - 2] = xbc

    dt_ref[...] = _softplus(proj[:, OFF_DT:OFF_DT + LANES] + dtb_ref[...])


def _sample_pre(x, layer, g, w, cpool, pbd, pscale, cconv, cw, cb, dtb, cos, sin):
    n = x.shape[0]
    args = [x, g, w, cpool, pbd, pscale, cconv, cw, cb, dtb, cos, sin]
    out_shape = (
        jax.ShapeDtypeStruct((n, D_POOL), bf16),
        jax.ShapeDtypeStruct((n, D_ATT), f32),
        jax.ShapeDtypeStruct((n, D_ATT), f32),
        jax.ShapeDtypeStruct((n, D_ATT), f32),
        jax.ShapeDtypeStruct((n, D_SSM), f32),
        jax.ShapeDtypeStruct((n, D_CONV), f32),
        jax.ShapeDtypeStruct((n, LANES), f32),
        jax.ShapeDtypeStruct((POOL_BUF, n, D_POOL), f32),
        jax.ShapeDtypeStruct((CONV_WIDTH - 1, n, D_CONV), f32),
    )
    full = lambda s: pl.BlockSpec(s.shape, lambda i, nd=len(s.shape): (0,) * nd)
    in_specs = [full(a) for a in args]
    in_specs[1] = _layer_spec(g, layer)
    in_specs[2] = _layer_spec(w, layer)
    return pl.pallas_call(
        _sample_pre_kernel,
        grid=(1,),
        in_specs=in_specs,
        out_specs=tuple(full(s) for s in out_shape),
        out_shape=out_shape,
        compiler_params=_params(1),
        name="sample_pre",
    )(*args)


def _sample_ssd_kernel(u_ref, dt_ref, z_ref, st_ref, aexp_ref, e_ref, dskip_ref, norm_ref, *rest):
    y_ref, stnew_ref = rest[-2:]
    n = u_ref.shape[0]
    gl = D_SSM // SSM_GROUPS
    u = u_ref[...]
    xs = u[:, 0:D_SSM]
    dt_exp = _dot_f32_by_01(dt_ref[...], e_ref[...])
    pad = jnp.zeros((LANES - n, D_SSM), f32)
    dec_t = jnp.concatenate([jnp.exp(dt_exp * aexp_ref[...]), pad], axis=0).T
    dtx_t = jnp.concatenate([dt_exp * xs, pad], axis=0).T
    lane = lax.broadcasted_iota(jnp.int32, (D_SSM, LANES), 1)
    y_t = jnp.zeros((D_SSM, LANES), f32)
    for i in range(n):
        dec = dec_t[:, i:i + 1]
        dtx = dtx_t[:, i:i + 1]
        ycols = []
        for g in range(SSM_GROUPS):
            rows = slice(g * gl, (g + 1) * gl)
            b_row = u[i:i + 1, D_SSM + g * SSM_STATE:D_SSM + (g + 1) * SSM_STATE]
            c_row = u[i:i + 1, D_SSM + (SSM_GROUPS + g) * SSM_STATE:D_SSM + (SSM_GROUPS + g + 1) * SSM_STATE]
            h_new = dec[rows] * st_ref[i, rows, :] + dtx[rows] * b_row
            stnew_ref[i, rows, :] = h_new
            ycols.append(jnp.sum(h_new * c_row, axis=1, keepdims=True))
        y_t = jnp.where(lane == i, jnp.concatenate(ycols, axis=0), y_t)
    y = y_t.T[0:n]
    y = (y + dskip_ref[...] * xs) * _silu(z_ref[...])
    outs = []
    for g in range(SSM_GROUPS):
        yg = y[:, g * gl:(g + 1) * gl]
        outs.append(yg * lax.rsqrt(jnp.mean(yg * yg, -1, keepdims=True) + RMS_EPS))
    y_ref[...] = (jnp.concatenate(outs, axis=1) * norm_ref[...]).astype(y_ref.dtype)


def _sample_ssd(layer, u, dt, z, state, prev, a_exp, e_mat, dskip, norm):
    n = u.shape[0]
    full = lambda s: pl.BlockSpec(s.shape, lambda i, nd=len(s.shape): (0,) * nd)
    of_layer = pl.BlockSpec((None,) + state.shape[1:], lambda i: (layer, 0, 0, 0))
    args = [u, dt, z, state, a_exp, e_mat, dskip, norm]
    in_specs = [full(u), full(dt), full(z), of_layer, full(a_exp), full(e_mat), full(dskip), full(norm)]
    aliases = {}
    if prev is not None:
        in_specs.append(pl.BlockSpec(memory_space=pl.ANY))
        aliases = {len(args): 1}
        args.append(prev)
    y_shape = jax.ShapeDtypeStruct((n, D_SSM), bf16)
    return pl.pallas_call(
        _sample_ssd_kernel,
        grid=(1,),
        in_specs=in_specs,
        out_specs=(full(y_shape), of_layer),
        out_shape=(y_shape, jax.ShapeDtypeStruct(state.shape, f32)),
        input_output_aliases=aliases,
        compiler_params=_params(1),
        name="sample_ssd",
    )(*args)


def _sample_attn_kernel(has_prev, q_ref, kn_ref, vn_ref, knc_ref, vnc_ref, kc_ref, vc_ref, *rest):
    if has_prev:
        rest = rest[2:]
    y_ref, ks_ref, vs_ref = rest
    wb = kc_ref.shape[3]
    kc = kc_ref[0, 0]
    vc = vc_ref[0, 0]
    q = q_ref[0]
    kn = kn_ref[0]
    vn = vn_ref[0]
    scale = 1.0 / math.sqrt(HEAD_DIM)

    own = (lax.broadcasted_iota(jnp.int32, (SUBLANES, D_ATT), 1) // HEAD_DIM) == lax.broadcasted_iota(
        jnp.int32, (SUBLANES, D_ATT), 0)
    qbd = jnp.where(own, jnp.broadcast_to(q, (SUBLANES, D_ATT)), 0.0)
    s_all = jnp.dot(qbd.astype(bf16), kc.astype(bf16), preferred_element_type=f32) * scale
    s_new = jnp.sum(jnp.where(own, jnp.broadcast_to(q * kn, (SUBLANES, D_ATT)), 0.0), axis=1, keepdims=True) * scale
    dist = wb - lax.broadcasted_iota(jnp.int32, (SUBLANES, wb), 1)
    ms, ps, ls, es = [], [], [], []
    for d in DILATIONS:
        valid = ((dist % d) == 0) & (dist <= N_BACK * d)
        s = jnp.where(valid, s_all, -jnp.inf)
        m = jnp.maximum(jnp.max(s, axis=1, keepdims=True), s_new)
        p = jnp.exp(s - m)
        e_new = jnp.exp(s_new - m)
        ms.append(m)
        ps.append(p)
        es.append(e_new)
        ls.append(jnp.sum(p, axis=1, keepdims=True) + e_new)
    m_all = jnp.maximum(jnp.maximum(ms[0], ms[1]), ms[2])
    p_all = jnp.zeros_like(ps[0])
    w_new = jnp.zeros_like(s_new)
    l_all = jnp.zeros_like(s_new)
    for m, p, e_new, l in zip(ms, ps, es, ls):
        c = jnp.exp(m - m_all)
        p_all = p_all + c * p
        w_new = w_new + c * e_new
        l_all = l_all + c * l
    o_full = lax.dot_general(p_all.astype(bf16), vc.astype(bf16), (((1,), (1,)), ((), ())), preferred_element_type=f32)
    to_row = lambda t: jnp.sum(jnp.where(own, t, 0.0), axis=0, keepdims=True)
    o_row = to_row(o_full) + to_row(jnp.broadcast_to(w_new, (SUBLANES, D_ATT))) * vn
    y_ref[0] = o_row / to_row(jnp.broadcast_to(l_all, (SUBLANES, D_ATT)))

    newest = lax.broadcasted_iota(jnp.int32, (D_ATT, wb), 1) == wb - 1
    ks_ref[0, 0] = jnp.where(newest, knc_ref[0], pltpu.roll(kc, wb - 1, 1))
    vs_ref[0, 0] = jnp.where(newest, vnc_ref[0], pltpu.roll(vc, wb - 1, 1))


def _sample_attn(layer, q, kn, vn, cache_k, cache_v, prev):
    depth, n, da, wb = cache_k.shape
    rowv = pl.BlockSpec((1, 1, da), lambda i: (i, 0, 0))
    colv = pl.BlockSpec((1, da, 1), lambda i: (i, 0, 0))
    big = pl.BlockSpec((1, 1, da, wb), lambda i: (layer, i, 0, 0))
    in_specs = [rowv, rowv, rowv, colv, colv, big, big]
    args = [q.reshape(n, 1, da), kn.reshape(n, 1, da), vn.reshape(n, 1, da), kn.reshape(n, da, 1), vn.reshape(n, da, 1),
            cache_k, cache_v]
    aliases = {}
    if prev is not None:
        in_specs += [pl.BlockSpec(memory_space=pl.ANY)] * 2
        aliases = {len(args): 1, len(args) + 1: 2}
        args += list(prev)
    return pl.pallas_call(
        functools.partial(_sample_attn_kernel, prev is not None),
        grid=(n,),
        in_specs=in_specs,
        out_specs=(rowv, big, big),
        out_shape=(jax.ShapeDtypeStruct((n, 1, da), f32), jax.ShapeDtypeStruct(cache_k.shape, f32),
                   jax.ShapeDtypeStruct(cache_v.shape, f32)),
        input_output_aliases=aliases,
        compiler_params=_params(1),
        name="sample_attn",
    )(*args)


def _rope_tables(pos):
    half = HEAD_DIM // 2
    inv = ROPE_THETA ** (-jnp.arange(half, dtype=f32) / half)
    ang = pos.astype(f32)[:, None] * inv[None]
    cos = jnp.tile(jnp.cos(ang), (1, LANES // half))
    sin = jnp.sin(ang)
    sin_signed = jnp.tile(jnp.concatenate([-sin, sin], axis=1), (1, LANES // HEAD_DIM))
    return cos, sin_signed


def kernel(x_prompt, x_sample, cache_pool, cache_k, cache_v, state_conv, state_ssm, ffn1_norm, ffn1_w_gate, ffn1_w_up,
           ffn1_w_down, mix_norm, w_in, pool_w, pool_scale, conv_w, conv_b, dt_bias, a_log, d_skip, ssm_norm, w_out,
           ffn2_norm, ffn2_w_gate, ffn2_w_up, ffn2_w_down, final_norm):
    batch, seq, _ = x_prompt.shape
    n_dec = x_sample.shape[0]
    depth = w_in.shape[0]
    wb = cache_k.shape[2]
    assert all(seq % tile == 0 for tile in (ATT_TILE, ROW_TILE, INPROJ_TILE, SSD_TILE))
    assert x_sample.shape[1] == 1 and wb == ATT_WIN

    yp = x_prompt.reshape(batch * seq, D_MODEL)
    ys = x_sample.reshape(n_dec, D_MODEL)
    cos_p, sin_p = _rope_tables(jnp.arange(seq))
    cos_s, sin_s = _rope_tables(jnp.full((1,), PAST_LEN))
    e_mat = (jnp.arange(LANES)[:, None] == (jnp.arange(D_SSM)[None, :] // SSM_HEAD_DIM)).astype(bf16)
    e3_mat = jnp.tile(e_mat, (3, 1))
    ck_t = jnp.transpose(cache_k, (0, 1, 3, 4, 2)).reshape(depth, n_dec, D_ATT, wb)
    cv_t = jnp.transpose(cache_v, (0, 1, 3, 4, 2)).reshape(depth, n_dec, D_ATT, wb)
    row2 = lambda a: a.reshape(1, -1)
    row3 = lambda a: a.reshape(depth, 1, -1)

    f1 = (row3(ffn1_norm), ffn1_w_gate.astype(bf16), ffn1_w_up.astype(bf16), ffn1_w_down.astype(bf16))
    f2 = (row3(ffn2_norm), ffn2_w_gate.astype(bf16), ffn2_w_up.astype(bf16), ffn2_w_down.astype(bf16))
    w_all = jnp.pad(w_in, ((0, 0), (0, 0), (0, N_PROJ - D_IN_PROJ))).astype(bf16)
    wo_all = w_out.astype(bf16)
    g_mix = row3(mix_norm)

    st_p = [[] for _ in range(3)]
    st_s = [[] for _ in range(2)]
    kv_p = kv_s = ssm_s = None
    for i in range(depth):
        last = i == depth - 1
        pbd = jax.scipy.linalg.block_diag(*[pool_w[i, g] for g in range(len(POOL_WINDOWS))]).astype(bf16)
        pscale = row2(pool_scale[i])
        cb = row2(conv_b[i])
        dtb = row2(jnp.pad(dt_bias[i], (0, LANES - N_SSM_HEADS)))
        a_neg = -jnp.exp(a_log[i].astype(f32))
        a2_pad = row2(jnp.pad(a_neg * math.log2(math.e), (0, LANES - N_SSM_HEADS)))
        a_exp = row2(jnp.repeat(a_neg, SSM_HEAD_DIM))
        dskip = row2(jnp.repeat(d_skip[i], SSM_HEAD_DIM))
        norm = row2(ssm_norm[i])
        fin = row2(final_norm) if last else None

        yp = _ffn(yp, i, *f1)
        ya, q, k, v, z, u, dt, pst, cst = _inproj_prompt(yp, i, batch, seq, g_mix, w_all, pbd, pscale, conv_w[i], cb, dtb,
                                                         cos_p, sin_p, cos_p * ATT_SCALE, sin_p * ATT_SCALE)
        yb = _attn_prompt(q, k, v, batch, seq)
        yc, sst = _ssd_prompt(u, dt, z, batch, seq, a2_pad, e3_mat, dskip, norm)
        yp = _ffn(yp, i, *f2, mix=(ya, yb, yc, wo_all), final_g=fin)

        kv_p = _kv_window(i, depth, k, v, batch, seq, kv_p)
        st_p[0].append(pst[:, POOL_HALO - POOL_BUF:])
        st_p[1].append(cst[:, CONV_HALO - (CONV_WIDTH - 1):])
        st_p[2].append(jnp.transpose(sst.reshape(batch, SSM_STATE, N_SSM_HEADS, SSM_HEAD_DIM), (0, 2, 3, 1)))

        ys = _ffn(ys, i, *f1)
        cpool = jnp.transpose(cache_pool[i], (1, 0, 2))
        cconv = jnp.transpose(state_conv[i], (1, 0, 2))
        ya_s, q_s, k_s, v_s, z_s, u_s, dt_s, pnew, cnew = _sample_pre(ys, i, g_mix, w_all, cpool, pbd, pscale, cconv,
                                                                    conv_w[i], cb, dtb, cos_s, sin_s)
        yb_s, ks_buf, vs_buf = _sample_attn(i, q_s, k_s, v_s, ck_t, cv_t, kv_s)
        kv_s = (ks_buf, vs_buf)
        yc_s, ssm_s = _sample_ssd(i, u_s, dt_s, z_s, state_ssm.reshape(depth, n_dec, D_SSM, SSM_STATE), ssm_s, a_exp, e_mat,
                                  dskip, norm)
        yb_slab = jnp.transpose(yb_s.reshape(n_dec, 2, LANES), (1, 0, 2)).astype(bf16)
        ys = _ffn(ys, i, *f2, mix=(ya_s, yb_slab, yc_s, wo_all), final_g=fin)
        st_s[0].append(jnp.transpose(pnew, (1, 0, 2)))
        st_s[1].append(jnp.transpose(cnew, (1, 0, 2)))

    y_prompt = yp.reshape(batch, seq, D_MODEL)
    y_sample = ys.reshape(n_dec, 1, D_MODEL)
    pool_p, conv_p, ssm_p = (jnp.stack(s, 0) for s in st_p)
    pool_s, conv_s = (jnp.stack(s, 0) for s in st_s)
    ssm_s = ssm_s.reshape(state_ssm.shape)
    from_t = lambda t: jnp.transpose(t.reshape(depth, -1, N_ATT_HEADS, HEAD_DIM, wb), (0, 1, 4, 2, 3))
    k_p, v_p = from_t(kv_p[0]), from_t(kv_p[1])
    return (y_prompt, y_sample, pool_p, pool_s, k_p, from_t(kv_s[0]), v_p, from_t(kv_s[1]), conv_p, conv_s, ssm_p, ssm_s)
```

```python
import functools
import math

import jax
import jax.numpy as jnp
from jax import lax
from jax.experimental import pallas as pl
from jax.experimental.pallas import tpu as pltpu

f32 = jnp.float32
bf16 = jnp.bfloat16

D_MODEL = 1024
PAST_LEN = 16384
POOL_WINDOWS = (2, 4, 8, 16)
D_POOL = 256
POOL_GROUP = D_POOL // len(POOL_WINDOWS)
POOL_BUF = max(POOL_WINDOWS) - 1
HEAD_DIM = 64
N_ATT_HEADS = 4
D_ATT = N_ATT_HEADS * HEAD_DIM
DILATIONS = (16, 4, 1)
N_BACK = 128
ATT_WIN = 2048
ROPE_THETA = 10000.0
ATT_SCALE = math.log2(math.e) / math.sqrt(HEAD_DIM)
D_SSM = 512
SSM_HEAD_DIM = 64
N_SSM_HEADS = 8
SSM_STATE = 128
SSM_GROUPS = 2
CONV_WIDTH = 4
SSM_CHUNK = 128
D_CONV = D_SSM + 2 * SSM_GROUPS * SSM_STATE
D_IN_PROJ = D_POOL + 3 * D_ATT + D_SSM + D_CONV + N_SSM_HEADS
D_FF = 2816
RMS_EPS = 1e-6

LANES = 128
SUBLANES = 8
VMEM_LIMIT = 56 * 1024 * 1024

OFF_Q = D_POOL
OFF_K = OFF_Q + D_ATT
OFF_V = OFF_K + D_ATT
OFF_Z = OFF_V + D_ATT
OFF_XBC = OFF_Z + D_SSM
OFF_DT = OFF_XBC + D_CONV
N_PROJ = OFF_DT + LANES

ROW_TILE = 512
FFN_SPLIT = 2
ATT_TILE = 2048
ATT_UNROLL = 4
INPROJ_TILE = 1024
INPROJ_SPLIT = 4
POOL_HALO = 16
CONV_HALO = 8


def _params(n_axes, flags=None):
    return pltpu.CompilerParams(dimension_semantics=("arbitrary",) * n_axes, vmem_limit_bytes=VMEM_LIMIT, flags=flags)


def _const_spec(shape):
    nd = len(shape)
    return pl.BlockSpec(shape, lambda *_: (0,) * nd, pipeline_mode=pl.Buffered(1))


def _layer_spec(arr, layer):
    nd = arr.ndim - 1
    return pl.BlockSpec((None,) + arr.shape[1:], lambda *_: (layer,) + (0,) * nd, pipeline_mode=pl.Buffered(1))


def _rms(x, g):
    return x * lax.rsqrt(jnp.mean(x * x, -1, keepdims=True) + RMS_EPS) * g


def _dot_f32_by_01(x, m01, lhs_is_01=False):
    hi = x.astype(bf16)
    rest = x - hi.astype(f32)
    mid = rest.astype(bf16)
    lo = (rest - mid.astype(f32)).astype(bf16)
    out = None
    for part in (hi, mid, lo):
        if lhs_is_01:
            t = jnp.dot(m01, part, preferred_element_type=f32)
        else:
            t = jnp.dot(part, m01, preferred_element_type=f32)
        out = t if out is None else out + t
    return out


def _silu(x):
    return x * jax.nn.sigmoid(x)


def _softplus(x):
    return jnp.maximum(x, 0.0) + jnp.log1p(jnp.exp(-jnp.abs(x)))


def _rope_slab(x, cos, sin_signed):
    lane = lax.broadcasted_iota(jnp.int32, x.shape, 1)
    first_half = (lane % HEAD_DIM) < (HEAD_DIM // 2)
    partner = jnp.where(first_half, pltpu.roll(x, LANES - HEAD_DIM // 2, 1), pltpu.roll(x, HEAD_DIM // 2, 1))
    return x * cos + partner * sin_signed


def _pool_select(s2, s4, s8, s16):
    lane = lax.broadcasted_iota(jnp.int32, s2.shape, 1)
    return jnp.where(lane < POOL_GROUP, s2, jnp.where(lane < 2 * POOL_GROUP, s4, jnp.where(lane < 3 * POOL_GROUP, s8, s16)))


def _pool_window(shape):
    lane = lax.broadcasted_iota(jnp.int32, shape, 1)
    return jnp.where(lane < POOL_GROUP, POOL_WINDOWS[0],
                     jnp.where(lane < 2 * POOL_GROUP, POOL_WINDOWS[1],
                               jnp.where(lane < 3 * POOL_GROUP, POOL_WINDOWS[2], POOL_WINDOWS[3])))


def _ffn_kernel(has_mix, has_final, *refs):
    refs = list(refs)
    x_ref = refs.pop(0)
    if has_mix:
        ya_ref, yb_ref, yc_ref, wo_ref = refs[:4]
        refs = refs[4:]
    g_ref, wg_ref, wu_ref, wd_ref = refs[:4]
    refs = refs[4:]
    if has_final:
        fg_ref = refs.pop(0)
    o_ref = refs.pop(0)

    tm = x_ref.shape[0]
    n_sub = FFN_SPLIT if tm % (FFN_SPLIT * SUBLANES * 2) == 0 else 1
    subs = [slice(h * (tm // n_sub), (h + 1) * (tm // n_sub)) for h in range(n_sub)]
    xs = [x_ref[rows, :] for rows in subs]
    if has_mix:
        cats = [jnp.concatenate([ya_ref[rows, :], yb_ref[0, rows, :], yb_ref[1, rows, :], yc_ref[rows, :]], axis=1)
                for rows in subs]
        xs = [x + jnp.dot(cat, wo_ref[...], preferred_element_type=f32) for x, cat in zip(xs, cats)]
    xns = [_rms(x, g_ref[...]).astype(bf16) for x in xs]
    gates = [jnp.dot(xn, wg_ref[...], preferred_element_type=f32) for xn in xns]
    ups = [jnp.dot(xn, wu_ref[...], preferred_element_type=f32) for xn in xns]
    hs = [(_silu(gate) * up).astype(bf16) for gate, up in zip(gates, ups)]
    downs = [jnp.dot(h, wd_ref[...], preferred_element_type=f32) for h in hs]
    for rows, x, down in zip(subs, xs, downs):
        y = x + 0.5 * down
        if has_final:
            y = _rms(y, fg_ref[...])
        o_ref[rows, :] = y


def _ffn(x, layer, g, wg, wu, wd, mix=None, final_g=None):
    m = x.shape[0]
    tm = min(ROW_TILE, m)
    assert m % tm == 0
    row = lambda w: pl.BlockSpec((tm, w), lambda i: (i, 0))
    args, specs = [x], [row(D_MODEL)]
    if mix is not None:
        ya, yb, yc, wo = mix
        args += [ya, yb, yc, wo]
        specs += [row(D_POOL), pl.BlockSpec((2, tm, LANES), lambda i: (0, i, 0)), row(D_SSM), _layer_spec(wo, layer)]
    args += [g, wg, wu, wd]
    specs += [_layer_spec(g, layer), _layer_spec(wg, layer), _layer_spec(wu, layer), _layer_spec(wd, layer)]
    if final_g is not None:
        args.append(final_g)
        specs.append(_const_spec(final_g.shape))
    return pl.pallas_call(
        functools.partial(_ffn_kernel, mix is not None, final_g is not None),
        grid=(m // tm,),
        in_specs=specs,
        out_specs=row(D_MODEL),
        out_shape=jax.ShapeDtypeStruct((m, D_MODEL), f32),
        compiler_params=_params(1),
        name="ffn",
    )(*args)


def _inproj_ssd_kernel(tiles_per_seq, x_ref, g_ref, w_ref, pbd_ref, pscale_ref, cw_ref, cb_ref, dtb_ref, cos_ref, sin_ref,
                       cosq_ref, sinq_ref, a2_ref, e3_ref, dskip_ref, norm_ref,
                       ya_ref, q_ref, k_ref, v_ref, yc_ref, pst_ref, cst_ref, state_ref,
                       xa_buf, xbc_buf, u_buf, z_buf, dt_buf, st):
    tm = x_ref.shape[0]
    n_sub = INPROJ_SPLIT
    sub = tm // n_sub
    t = pl.program_id(0) % tiles_per_seq

    @pl.when(t == 0)
    def _():
        xa_buf[0:POOL_HALO, :] = jnp.zeros((POOL_HALO, D_POOL), f32)
        xbc_buf[0:CONV_HALO, :] = jnp.zeros((CONV_HALO, D_CONV), f32)
        st[...] = jnp.zeros_like(st)

    groups = (("xbc_dt", OFF_XBC, N_PROJ), ("xa", 0, OFF_Q), ("qkv", OFF_Q, OFF_Z), ("z", OFF_Z, OFF_XBC))
    projs = {}

    def project(h):
        xn = _rms(x_ref[h * sub:(h + 1) * sub, :], g_ref[...]).astype(bf16)
        projs[h] = {name: jnp.dot(xn, w_ref[:, a:b], preferred_element_type=f32) for name, a, b in groups}

    def tail(h):
        r0 = h * sub
        rows = slice(r0, r0 + sub)
        xbc, xa, qkv = projs[h]["xbc_dt"][:, 0:D_CONV], projs[h]["xa"], projs[h]["qkv"]

        xbc_buf[CONV_HALO + r0:CONV_HALO + r0 + sub, :] = xbc
        conv = cb_ref[...] + xbc * cw_ref[CONV_WIDTH - 1:CONV_WIDTH, :]
        for j in range(1, CONV_WIDTH):
            conv = conv + xbc_buf[CONV_HALO + r0 - j:CONV_HALO + r0 - j + sub, :] * cw_ref[CONV_WIDTH - 1 - j:CONV_WIDTH - j, :]
        u_buf[rows, :] = _silu(conv)
        dt_buf[rows, :] = _softplus(projs[h]["xbc_dt"][:, D_CONV:D_CONV + LANES] + dtb_ref[...])
        z_buf[rows, :] = projs[h]["z"]

        xa_buf[POOL_HALO + r0:POOL_HALO + r0 + sub, :] = xa
        ext = xa_buf[r0:r0 + sub + POOL_HALO, :]
        s2 = ext + pltpu.roll(ext, 1, 0)
        s4 = s2 + pltpu.roll(s2, 2, 0)
        s8 = s4 + pltpu.roll(s4, 4, 0)
        s16 = s8 + pltpu.roll(s8, 8, 0)
        sums = _pool_select(s2, s4, s8, s16)[POOL_HALO:POOL_HALO + sub, :]
        pos = t * tm + r0 + lax.broadcasted_iota(jnp.int32, (sub, D_POOL), 0)
        cnt = jnp.minimum(_pool_window((sub, D_POOL)), pos + 1).astype(f32)
        dmean = sums / cnt - xa
        ya = jnp.dot(dmean.astype(bf16), pbd_ref[...], preferred_element_type=f32) * pscale_ref[...]
        ya_ref[rows, :] = ya.astype(ya_ref.dtype)

        cos = cos_ref[rows, :]
        sin = sin_ref[rows, :]
        cosq = cosq_ref[rows, :]
        sinq = sinq_ref[rows, :]
        for s in range(2):
            q_ref[s, rows, :] = _rope_slab(qkv[:, s * LANES:(s + 1) * LANES], cosq, sinq)
            k_ref[s, rows, :] = _rope_slab(qkv[:, D_ATT + s * LANES:D_ATT + (s + 1) * LANES], cos, sin)
            v_ref[s, rows, :] = qkv[:, 2 * D_ATT + s * LANES:2 * D_ATT + (s + 1) * LANES]

    def carry_history():
        last_xbc = xbc_buf[tm:tm + CONV_HALO, :]
        cst_ref[0] = last_xbc
        xbc_buf[0:CONV_HALO, :] = last_xbc
        last_xa = xa_buf[tm:tm + POOL_HALO, :]
        pst_ref[0] = last_xa
        xa_buf[0:POOL_HALO, :] = last_xa

    q = SSM_CHUNK
    cps = sub // q
    gl = D_SSM // SSM_GROUPS
    hpg = N_SSM_HEADS // SSM_GROUPS
    li = lax.broadcasted_iota(jnp.int32, (q, q), 0)
    si = lax.broadcasted_iota(jnp.int32, (q, q), 1)
    causal = li >= si
    tril = causal.astype(bf16)
    triu = (li <= si).astype(bf16)
    lo = lax.broadcasted_iota(jnp.int32, (q, LANES), 1) < SSM_HEAD_DIM
    crows = lambda c: slice(c * q, (c + 1) * q)
    b_of = lambda u, g: u[:, D_SSM + g * SSM_STATE:D_SSM + (g + 1) * SSM_STATE]
    c_of = lambda u, g: u[:, D_SSM + (SSM_GROUPS + g) * SSM_STATE:D_SSM + (SSM_GROUPS + g + 1) * SSM_STATE]
    sgroups = range(SSM_GROUPS)
    gsl = lambda g: slice(g * gl, (g + 1) * gl)
    chunks_of = lambda h: range(h * cps, (h + 1) * cps)
    v = {}

    def level1(h):
        for c in chunks_of(h):
            u = u_buf[crows(c), :]
            dtp = dt_buf[crows(c), :]
            dta = dtp * a2_ref[...]
            v["dtp", c] = dtp
            v["cum3", c] = jnp.dot(tril, jnp.concatenate(_split3(dta), axis=1), preferred_element_type=f32)
            v["acs_row", c] = _dot_f32_by_01(dta.T[0:SUBLANES], triu)
            for g in sgroups:
                v["cb", c, g] = lax.dot_general(c_of(u, g).astype(bf16), b_of(u, g).astype(bf16), (((1,), (1,)), ((), ())),
                                                preferred_element_type=f32)

    def level2(h):
        for c in chunks_of(h):
            cum3 = v["cum3", c]
            cum = cum3[:, 0:LANES] + cum3[:, LANES:2 * LANES] + cum3[:, 2 * LANES:3 * LANES]
            v["both", c] = jnp.dot(jnp.concatenate(_split3(jnp.concatenate([v["dtp", c], cum], axis=0)), axis=1), e3_ref[...],
                                   preferred_element_type=f32)

    def level3(h):
        for c in chunks_of(h):
            u = u_buf[crows(c), :]
            dt_exp, acs_exp, acs_row = v["both", c][0:q], v["both", c][q:2 * q], v["acs_row", c]
            xdt = u[:, 0:D_SSM] * dt_exp
            xdt_b = xdt.astype(bf16)
            parts = []
            for g in sgroups:
                for pair in range(hpg // 2):
                    mats = []
                    for hh in range(2):
                        hd = g * hpg + pair * 2 + hh
                        col = acs_exp[:, hd * SSM_HEAD_DIM:hd * SSM_HEAD_DIM + 1]
                        diff = jnp.where(causal, col - acs_row[hd:hd + 1, :], -jnp.inf)
                        mats.append((v["cb", c, g] * jnp.exp2(diff)).astype(bf16))
                    c0 = (g * hpg + pair * 2) * SSM_HEAD_DIM
                    xp = xdt_b[:, c0:c0 + LANES]
                    rhs = jnp.concatenate([jnp.where(lo, xp, jnp.zeros_like(xp)), jnp.where(lo, jnp.zeros_like(xp), xp)],
                                          axis=0)
                    parts.append(jnp.dot(jnp.concatenate(mats, axis=1), rhs, preferred_element_type=f32))
            v["ydiag", c] = jnp.concatenate(parts, axis=1)
            acs_last = acs_exp[q - 1:q, :]
            xw = (xdt * jnp.exp2(acs_last - acs_exp)).astype(bf16)
            v["s_new", c] = jnp.concatenate(
                [jnp.dot(b_of(u, g).T.astype(bf16), xw[:, gsl(g)], preferred_element_type=f32) for g in sgroups], axis=1)
            v["acs", c] = acs_exp

    def level4():
        state = st[...]
        for c in range(n_sub * cps):
            u = u_buf[crows(c), :]
            state_b = state.astype(bf16)
            v["yoff", c] = jnp.concatenate(
                [jnp.dot(c_of(u, g).astype(bf16), state_b[:, gsl(g)], preferred_element_type=f32) for g in sgroups], axis=1)
            state = state * jnp.exp2(v["acs", c][q - 1:q, :]) + v["s_new", c]
        st[...] = state
        state_ref[0] = state

    def level5():
        for c in range(n_sub * cps):
            xs = u_buf[crows(c), 0:D_SSM]
            y = v["ydiag", c] + v["yoff", c] * jnp.exp2(v["acs", c])
            y = (y + dskip_ref[...] * xs) * _silu(z_buf[crows(c), :])
            outs = []
            for g in sgroups:
                yg = y[:, gsl(g)]
                outs.append(yg * lax.rsqrt(jnp.mean(yg * yg, -1, keepdims=True) + RMS_EPS))
            yc_ref[crows(c), :] = (jnp.concatenate(outs, axis=1) * norm_ref[...]).astype(yc_ref.dtype)

    project(0)
    project(1)
    for s in range(n_sub + 3):
        if s < n_sub:
            tail(s)
        if s + 2 < n_sub:
            project(s + 2)
        for level, lag in ((level1, 1), (level2, 2), (level3, 3)):
            if 0 <= s - lag < n_sub:
                level(s - lag)
    carry_history()
    level4()
    level5()


def _inproj_ssd_prompt(x, layer, batch, seq, g, w, pbd, pscale, cw, cb, dtb, cos, sin, cosq, sinq, a2_pad, e3_mat, dskip, norm):
    m = x.shape[0]
    tm = INPROJ_TILE
    tps = seq // tm
    row = lambda wd: pl.BlockSpec((tm, wd), lambda i: (i, 0))
    slab = pl.BlockSpec((2, tm, LANES), lambda i: (0, i, 0))
    tab = pl.BlockSpec((tm, LANES), lambda i: (i % tps, 0))
    per_seq = lambda r, c: pl.BlockSpec((1, r, c), lambda i: (i // tps, 0, 0))
    out_shape = (
        jax.ShapeDtypeStruct((m, D_POOL), bf16),
        jax.ShapeDtypeStruct((2, m, LANES), f32),
        jax.ShapeDtypeStruct((2, m, LANES), f32),
        jax.ShapeDtypeStruct((2, m, LANES), f32),
        jax.ShapeDtypeStruct((m, D_SSM), bf16),
        jax.ShapeDtypeStruct((batch, POOL_HALO, D_POOL), f32),
        jax.ShapeDtypeStruct((batch, CONV_HALO, D_CONV), f32),
        jax.ShapeDtypeStruct((batch, SSM_STATE, D_SSM), f32),
    )
    out_specs = (row(D_POOL), slab, slab, slab, row(D_SSM), per_seq(POOL_HALO, D_POOL), per_seq(CONV_HALO, D_CONV),
                 per_seq(SSM_STATE, D_SSM))
    consts = [pbd, pscale, cw, cb, dtb]
    ssd_consts = [a2_pad, e3_mat, dskip, norm]
    in_specs = ([row(D_MODEL), _layer_spec(g, layer), _layer_spec(w, layer)] + [_const_spec(c.shape) for c in consts]
                + [tab, tab, tab, tab] + [_const_spec(c.shape) for c in ssd_consts])
    return pl.pallas_call(
        functools.partial(_inproj_ssd_kernel, tps),
        grid=(m // tm,),
        in_specs=in_specs,
        out_specs=out_specs,
        out_shape=out_shape,
        scratch_shapes=[pltpu.VMEM((POOL_HALO + tm, D_POOL), f32), pltpu.VMEM((CONV_HALO + tm, D_CONV), f32),
                        pltpu.VMEM((tm, D_CONV), f32), pltpu.VMEM((tm, D_SSM), f32), pltpu.VMEM((tm, LANES), f32),
                        pltpu.VMEM((SSM_STATE, D_SSM), f32)],
        compiler_params=_params(1),
        name="inproj_ssd",
    )(x, g, w, *consts, cos, sin, cosq, sinq, *ssd_consts)


def _attn_kernel(q_ref, kp_ref, kc_ref, vp_ref, vc_ref, o_ref, acc, mrun, lrun):
    t = pl.program_id(2)
    blk = N_BACK
    qi = lax.broadcasted_iota(jnp.int32, (2 * blk, 2 * blk), 0) % blk
    kj = lax.broadcasted_iota(jnp.int32, (2 * blk, 2 * blk), 1)
    dist = blk + qi - kj
    band = (dist >= 0) & (dist <= N_BACK)
    bias_full = jnp.where(band, 0.0, -jnp.inf).astype(f32)
    bias_cur = jnp.where(band & (kj >= blk), 0.0, -jnp.inf).astype(f32)
    bias_first = jnp.where(t > 0, bias_full, bias_cur)
    head0 = lax.broadcasted_iota(jnp.int32, (blk, LANES), 1) < HEAD_DIM
    head0_k = lax.broadcasted_iota(jnp.int32, (2 * blk, LANES), 1) < HEAD_DIM
    ones_blk = jnp.concatenate([jnp.where(head0_k, 1.0, 0.0), jnp.where(head0_k, 0.0, 1.0)], axis=0).astype(bf16)
    n_units = ATT_TILE // blk

    def strided(ref, start, n, d):
        return ref[0, 0, pl.ds(start, n, stride=d) if d > 1 else pl.ds(start, n), :]

    def group(d, units):
        first = d == DILATIONS[0]
        last = d == DILATIONS[-1]
        span = blk * d
        rows, scores, vcats = [], [], []
        for u in units:
            sb, r = divmod(u, d)
            qstart = sb * span + r
            rows.append(pl.ds(qstart, blk, stride=d) if d > 1 else pl.ds(qstart, blk))
            qb = strided(q_ref, qstart, blk, d).astype(bf16)
            if sb > 0:
                kf = strided(kc_ref, qstart - span, 2 * blk, d)
                vf = strided(vc_ref, qstart - span, 2 * blk, d)
                bias = bias_full
            else:
                kf = jnp.concatenate([strided(kp_ref, ATT_TILE - span + r, blk, d), strided(kc_ref, r, blk, d)], axis=0)
                vf = jnp.concatenate([strided(vp_ref, ATT_TILE - span + r, blk, d), strided(vc_ref, r, blk, d)], axis=0)
                bias = bias_first
            kb = kf.astype(bf16)
            vb = vf.astype(bf16)
            qcat = jnp.concatenate([jnp.where(head0, qb, jnp.zeros_like(qb)), jnp.where(head0, jnp.zeros_like(qb), qb)], axis=0)
            scores.append(lax.dot_general(qcat, kb, (((1,), (1,)), ((), ())), preferred_element_type=f32) + bias)
            vcat = jnp.concatenate([jnp.where(head0_k, vb, jnp.zeros_like(vb)), jnp.where(head0_k, jnp.zeros_like(vb), vb)],
                                   axis=0)
            vcats.append(jnp.concatenate([vcat, ones_blk], axis=1))
        stats = []
        for sc in scores:
            m2 = jnp.max(sc, axis=1, keepdims=True)
            pb = jnp.exp2(sc - m2).astype(bf16)
            stats.append((jnp.concatenate([pb[0:blk], pb[blk:2 * blk]], axis=1), jnp.where(head0, m2[0:blk], m2[blk:2 * blk])))
        pvs = [jnp.dot(pcat, vcat, preferred_element_type=f32) for (pcat, _), vcat in zip(stats, vcats)]
        for qrows, pvl, (_, m_e) in zip(rows, pvs, stats):
            pv, l_e = pvl[:, 0:LANES], pvl[:, LANES:2 * LANES]
            if first:
                acc[qrows, :] = pv
                mrun[qrows, :] = m_e
                lrun[qrows, :] = l_e
            else:
                m_old = mrun[qrows, :]
                m_new = jnp.maximum(m_old, m_e)
                a = jnp.exp2(m_old - m_new)
                b = jnp.exp2(m_e - m_new)
                acc_new = acc[qrows, :] * a + pv * b
                l_new = lrun[qrows, :] * a + l_e * b
                if last:
                    o_ref[0, 0, qrows, :] = (acc_new / l_new).astype(o_ref.dtype)
                else:
                    acc[qrows, :] = acc_new
                    lrun[qrows, :] = l_new
                    mrun[qrows, :] = m_new

    for d in DILATIONS:
        for g0 in range(0, n_units, ATT_UNROLL):
            group(d, range(g0, g0 + ATT_UNROLL))


def _attn_prompt(q, k, v, batch, seq):
    nt = seq // ATT_TILE
    q4 = q.reshape(2, batch, seq, LANES)
    k4 = k.reshape(2, batch, seq, LANES)
    v4 = v.reshape(2, batch, seq, LANES)
    cur = pl.BlockSpec((1, 1, ATT_TILE, LANES), lambda b, s, t: (s, b, t, 0))
    prev = pl.BlockSpec((1, 1, ATT_TILE, LANES), lambda b, s, t: (s, b, jnp.maximum(t - 1, 0), 0))
    out = pl.pallas_call(
        _attn_kernel,
        grid=(batch, 2, nt),
        in_specs=[cur, prev, cur, prev, cur],
        out_specs=cur,
        out_shape=jax.ShapeDtypeStruct((2, batch, seq, LANES), bf16),
        scratch_shapes=[pltpu.VMEM((ATT_TILE, LANES), f32), pltpu.VMEM((ATT_TILE, LANES), f32),
                        pltpu.VMEM((ATT_TILE, LANES), f32)],
        compiler_params=_params(3),
        name="attn",
    )(q4, k4, k4, v4, v4)
    return out.reshape(2, batch * seq, LANES)


def _kv_window_kernel(k_ref, v_ref, *rest):
    ko_ref, vo_ref = rest[-2:]
    ko_ref[0] = k_ref[0, 0].T
    vo_ref[0] = v_ref[0, 0].T


def _kv_window(layer, depth, k, v, batch, seq, prev):
    assert seq % ATT_WIN == 0
    k4 = k.reshape(2, batch, seq, LANES)
    v4 = v.reshape(2, batch, seq, LANES)
    src = pl.BlockSpec((1, 1, ATT_WIN, LANES), lambda b, s: (s, b, seq // ATT_WIN - 1, 0))
    dst = pl.BlockSpec((None, 1, LANES, ATT_WIN), lambda b, s: (layer, b, s, 0))
    shape = jax.ShapeDtypeStruct((depth, batch, D_ATT, ATT_WIN), f32)
    args, in_specs, aliases = [k4, v4], [src, src], {}
    if prev is not None:
        in_specs += [pl.BlockSpec(memory_space=pl.ANY)] * 2
        aliases = {2: 0, 3: 1}
        args += list(prev)
    return pl.pallas_call(
        _kv_window_kernel,
        grid=(batch, 2),
        in_specs=in_specs,
        out_specs=(dst, dst),
        out_shape=(shape, shape),
        input_output_aliases=aliases,
        compiler_params=_params(2),
        name="kv_window",
    )(*args)


def _split3(x):
    hi = x.astype(bf16)
    rest = x - hi.astype(f32)
    mid = rest.astype(bf16)
    lo = (rest - mid.astype(f32)).astype(bf16)
    return hi, mid, lo


def _sample_pre_kernel(x_ref, g_ref, w_ref, cpool_ref, pbd_ref, pscale_ref, cconv_ref, cw_ref, cb_ref, dtb_ref,
                       cos_ref, sin_ref, ya_ref, q_ref, k_ref, v_ref, z_ref, u_ref, dt_ref, pnew_ref, cnew_ref):
    xn = _rms(x_ref[...], g_ref[...]).astype(bf16)
    proj = jnp.dot(xn, w_ref[...], preferred_element_type=f32)

    xa = proj[:, 0:D_POOL]
    back = lambda i: cpool_ref[POOL_BUF - i]
    s2 = xa + back(1)
    s4 = s2 + back(2) + back(3)
    s8 = s4 + back(4) + back(5) + back(6) + back(7)
    s16 = s8
    for i in range(8, 16):
        s16 = s16 + back(i)
    cnt = jnp.minimum(_pool_window(xa.shape), PAST_LEN + 1).astype(f32)
    dmean = _pool_select(s2, s4, s8, s16) / cnt - xa
    ya = jnp.dot(dmean.astype(bf16), pbd_ref[...], preferred_element_type=f32) * pscale_ref[...]
    ya_ref[...] = ya.astype(ya_ref.dtype)
    for i in range(POOL_BUF - 1):
        pnew_ref[i] = cpool_ref[i + 1]
    pnew_ref[POOL_BUF - 1] = xa

    cos = cos_ref[...]
    sin = sin_ref[...]
    for s in range(2):
        sl = slice(s * LANES, (s + 1) * LANES)
        q_ref[:, sl] = _rope_slab(proj[:, OFF_Q + s * LANES:OFF_Q + (s + 1) * LANES], cos, sin)
        k_ref[:, sl] = _rope_slab(proj[:, OFF_K + s * LANES:OFF_K + (s + 1) * LANES], cos, sin)
    v_ref[...] = proj[:, OFF_V:OFF_V + D_ATT]
    z_ref[...] = proj[:, OFF_Z:OFF_Z + D_SSM]

    xbc = proj[:, OFF_XBC:OFF_XBC + D_CONV]
    conv = cb_ref[...] + xbc * cw_ref[CONV_WIDTH - 1:CONV_WIDTH, :]
    for j in range(1, CONV_WIDTH):
        conv = conv + cconv_ref[CONV_WIDTH - 1 - j] * cw_ref[CONV_WIDTH - 1 - j:CONV_WIDTH - j, :]
    u_ref[...] = _silu(conv)
    for j in range(CONV_WIDTH - 2):
        cnew_ref[j] = cconv_ref[j + 1]
    cnew_ref[CONV_WIDTH - 2] = xbc

    dt_ref[...] = _softplus(proj[:, OFF_DT:OFF_DT + LANES] + dtb_ref[...])


def _sample_pre(x, layer, g, w, cpool, pbd, pscale, cconv, cw, cb, dtb, cos, sin):
    n = x.shape[0]
    args = [x, g, w, cpool, pbd, pscale, cconv, cw, cb, dtb, cos, sin]
    out_shape = (
        jax.ShapeDtypeStruct((n, D_POOL), bf16),
        jax.ShapeDtypeStruct((n, D_ATT), f32),
        jax.ShapeDtypeStruct((n, D_ATT), f32),
        jax.ShapeDtypeStruct((n, D_ATT), f32),
        jax.ShapeDtypeStruct((n, D_SSM), f32),
        jax.ShapeDtypeStruct((n, D_CONV), f32),
        jax.ShapeDtypeStruct((n, LANES), f32),
        jax.ShapeDtypeStruct((POOL_BUF, n, D_POOL), f32),
        jax.ShapeDtypeStruct((CONV_WIDTH - 1, n, D_CONV), f32),
    )
    full = lambda s: pl.BlockSpec(s.shape, lambda i, nd=len(s.shape): (0,) * nd)
    in_specs = [full(a) for a in args]
    in_specs[1] = _layer_spec(g, layer)
    in_specs[2] = _layer_spec(w, layer)
    return pl.pallas_call(
        _sample_pre_kernel,
        grid=(1,),
        in_specs=in_specs,
        out_specs=tuple(full(s) for s in out_shape),
        out_shape=out_shape,
        compiler_params=_params(1),
        name="sample_pre",
    )(*args)


def _sample_ssd_kernel(u_ref, dt_ref, z_ref, st_ref, aexp_ref, e_ref, dskip_ref, norm_ref, *rest):
    y_ref, stnew_ref = rest[-2:]
    n = u_ref.shape[0]
    gl = D_SSM // SSM_GROUPS
    u = u_ref[...]
    xs = u[:, 0:D_SSM]
    dt_exp = _dot_f32_by_01(dt_ref[...], e_ref[...])
    pad = jnp.zeros((LANES - n, D_SSM), f32)
    dec_t = jnp.concatenate([jnp.exp(dt_exp * aexp_ref[...]), pad], axis=0).T
    dtx_t = jnp.concatenate([dt_exp * xs, pad], axis=0).T
    lane = lax.broadcasted_iota(jnp.int32, (D_SSM, LANES), 1)
    y_t = jnp.zeros((D_SSM, LANES), f32)
    for i in range(n):
        dec = dec_t[:, i:i + 1]
        dtx = dtx_t[:, i:i + 1]
        ycols = []
        for g in range(SSM_GROUPS):
            rows = slice(g * gl, (g + 1) * gl)
            b_row = u[i:i + 1, D_SSM + g * SSM_STATE:D_SSM + (g + 1) * SSM_STATE]
            c_row = u[i:i + 1, D_SSM + (SSM_GROUPS + g) * SSM_STATE:D_SSM + (SSM_GROUPS + g + 1) * SSM_STATE]
            h_new = dec[rows] * st_ref[i, rows, :] + dtx[rows] * b_row
            stnew_ref[i, rows, :] = h_new
            ycols.append(jnp.sum(h_new * c_row, axis=1, keepdims=True))
        y_t = jnp.where(lane == i, jnp.concatenate(ycols, axis=0), y_t)
    y = y_t.T[0:n]
    y = (y + dskip_ref[...] * xs) * _silu(z_ref[...])
    outs = []
    for g in range(SSM_GROUPS):
        yg = y[:, g * gl:(g + 1) * gl]
        outs.append(yg * lax.rsqrt(jnp.mean(yg * yg, -1, keepdims=True) + RMS_EPS))
    y_ref[...] = (jnp.concatenate(outs, axis=1) * norm_ref[...]).astype(y_ref.dtype)


def _sample_ssd(layer, u, dt, z, state, prev, a_exp, e_mat, dskip, norm):
    n = u.shape[0]
    full = lambda s: pl.BlockSpec(s.shape, lambda i, nd=len(s.shape): (0,) * nd)
    of_layer = pl.BlockSpec((None,) + state.shape[1:], lambda i: (layer, 0, 0, 0))
    args = [u, dt, z, state, a_exp, e_mat, dskip, norm]
    in_specs = [full(u), full(dt), full(z), of_layer, full(a_exp), full(e_mat), full(dskip), full(norm)]
    aliases = {}
    if prev is not None:
        in_specs.append(pl.BlockSpec(memory_space=pl.ANY))
        aliases = {len(args): 1}
        args.append(prev)
    y_shape = jax.ShapeDtypeStruct((n, D_SSM), bf16)
    return pl.pallas_call(
        _sample_ssd_kernel,
        grid=(1,),
        in_specs=in_specs,
        out_specs=(full(y_shape), of_layer),
        out_shape=(y_shape, jax.ShapeDtypeStruct(state.shape, f32)),
        input_output_aliases=aliases,
        compiler_params=_params(1),
        name="sample_ssd",
    )(*args)


def _sample_attn_kernel(has_prev, q_ref, kn_ref, vn_ref, knc_ref, vnc_ref, kc_ref, vc_ref, *rest):
    if has_prev:
        rest = rest[2:]
    y_ref, ks_ref, vs_ref = rest
    wb = kc_ref.shape[3]
    kc = kc_ref[0, 0]
    vc = vc_ref[0, 0]
    q = q_ref[0]
    kn = kn_ref[0]
    vn = vn_ref[0]
    scale = 1.0 / math.sqrt(HEAD_DIM)

    own = (lax.broadcasted_iota(jnp.int32, (SUBLANES, D_ATT), 1) // HEAD_DIM) == lax.broadcasted_iota(
        jnp.int32, (SUBLANES, D_ATT), 0)
    qbd = jnp.where(own, jnp.broadcast_to(q, (SUBLANES, D_ATT)), 0.0)
    s_all = jnp.dot(qbd.astype(bf16), kc.astype(bf16), preferred_element_type=f32) * scale
    s_new = jnp.sum(jnp.where(own, jnp.broadcast_to(q * kn, (SUBLANES, D_ATT)), 0.0), axis=1, keepdims=True) * scale
    dist = wb - lax.broadcasted_iota(jnp.int32, (SUBLANES, wb), 1)
    ms, ps, ls, es = [], [], [], []
    for d in DILATIONS:
        valid = ((dist % d) == 0) & (dist <= N_BACK * d)
        s = jnp.where(valid, s_all, -jnp.inf)
        m = jnp.maximum(jnp.max(s, axis=1, keepdims=True), s_new)
        p = jnp.exp(s - m)
        e_new = jnp.exp(s_new - m)
        ms.append(m)
        ps.append(p)
        es.append(e_new)
        ls.append(jnp.sum(p, axis=1, keepdims=True) + e_new)
    m_all = jnp.maximum(jnp.maximum(ms[0], ms[1]), ms[2])
    p_all = jnp.zeros_like(ps[0])
    w_new = jnp.zeros_like(s_new)
    l_all = jnp.zeros_like(s_new)
    for m, p, e_new, l in zip(ms, ps, es, ls):
        c = jnp.exp(m - m_all)
        p_all = p_all + c * p
        w_new = w_new + c * e_new
        l_all = l_all + c * l
    o_full = lax.dot_general(p_all.astype(bf16), vc.astype(bf16), (((1,), (1,)), ((), ())), preferred_element_type=f32)
    to_row = lambda t: jnp.sum(jnp.where(own, t, 0.0), axis=0, keepdims=True)
    o_row = to_row(o_full) + to_row(jnp.broadcast_to(w_new, (SUBLANES, D_ATT))) * vn
    y_ref[0] = o_row / to_row(jnp.broadcast_to(l_all, (SUBLANES, D_ATT)))

    newest = lax.broadcasted_iota(jnp.int32, (D_ATT, wb), 1) == wb - 1
    ks_ref[0, 0] = jnp.where(newest, knc_ref[0], pltpu.roll(kc, wb - 1, 1))
    vs_ref[0, 0] = jnp.where(newest, vnc_ref[0], pltpu.roll(vc, wb - 1, 1))


def _sample_attn(layer, q, kn, vn, cache_k, cache_v, prev):
    depth, n, da, wb = cache_k.shape
    rowv = pl.BlockSpec((1, 1, da), lambda i: (i, 0, 0))
    colv = pl.BlockSpec((1, da, 1), lambda i: (i, 0, 0))
    big = pl.BlockSpec((1, 1, da, wb), lambda i: (layer, i, 0, 0))
    in_specs = [rowv, rowv, rowv, colv, colv, big, big]
    args = [q.reshape(n, 1, da), kn.reshape(n, 1, da), vn.reshape(n, 1, da), kn.reshape(n, da, 1), vn.reshape(n, da, 1),
            cache_k, cache_v]
    aliases = {}
    if prev is not None:
        in_specs += [pl.BlockSpec(memory_space=pl.ANY)] * 2
        aliases = {len(args): 1, len(args) + 1: 2}
        args += list(prev)
    return pl.pallas_call(
        functools.partial(_sample_attn_kernel, prev is not None),
        grid=(n,),
        in_specs=in_specs,
        out_specs=(rowv, big, big),
        out_shape=(jax.ShapeDtypeStruct((n, 1, da), f32), jax.ShapeDtypeStruct(cache_k.shape, f32),
                   jax.ShapeDtypeStruct(cache_v.shape, f32)),
        input_output_aliases=aliases,
        compiler_params=_params(1),
        name="sample_attn",
    )(*args)


def _rope_tables(pos):
    half = HEAD_DIM // 2
    inv = ROPE_THETA ** (-jnp.arange(half, dtype=f32) / half)
    ang = pos.astype(f32)[:, None] * inv[None]
    cos = jnp.tile(jnp.cos(ang), (1, LANES // half))
    sin = jnp.sin(ang)
    sin_signed = jnp.tile(jnp.concatenate([-sin, sin], axis=1), (1, LANES // HEAD_DIM))
    return cos, sin_signed


def kernel(x_prompt, x_sample, cache_pool, cache_k, cache_v, state_conv, state_ssm, ffn1_norm, ffn1_w_gate, ffn1_w_up,
           ffn1_w_down, mix_norm, w_in, pool_w, pool_scale, conv_w, conv_b, dt_bias, a_log, d_skip, ssm_norm, w_out,
           ffn2_norm, ffn2_w_gate, ffn2_w_up, ffn2_w_down, final_norm):
    batch, seq, _ = x_prompt.shape
    n_dec = x_sample.shape[0]
    depth = w_in.shape[0]
    wb = cache_k.shape[2]
    assert all(seq % tile == 0 for tile in (ATT_TILE, ROW_TILE, INPROJ_TILE))
    assert x_sample.shape[1] == 1 and wb == ATT_WIN

    yp = x_prompt.reshape(batch * seq, D_MODEL)
    ys = x_sample.reshape(n_dec, D_MODEL)
    cos_p, sin_p = _rope_tables(jnp.arange(seq))
    cos_s, sin_s = _rope_tables(jnp.full((1,), PAST_LEN))
    e_mat = (jnp.arange(LANES)[:, None] == (jnp.arange(D_SSM)[None, :] // SSM_HEAD_DIM)).astype(bf16)
    e3_mat = jnp.tile(e_mat, (3, 1))
    ck_t = jnp.transpose(cache_k, (0, 1, 3, 4, 2)).reshape(depth, n_dec, D_ATT, wb)
    cv_t = jnp.transpose(cache_v, (0, 1, 3, 4, 2)).reshape(depth, n_dec, D_ATT, wb)
    row2 = lambda a: a.reshape(1, -1)
    row3 = lambda a: a.reshape(depth, 1, -1)

    f1 = (row3(ffn1_norm), ffn1_w_gate.astype(bf16), ffn1_w_up.astype(bf16), ffn1_w_down.astype(bf16))
    f2 = (row3(ffn2_norm), ffn2_w_gate.astype(bf16), ffn2_w_up.astype(bf16), ffn2_w_down.astype(bf16))
    w_all = jnp.pad(w_in, ((0, 0), (0, 0), (0, N_PROJ - D_IN_PROJ))).astype(bf16)
    wo_all = w_out.astype(bf16)
    g_mix = row3(mix_norm)

    st_p = [[] for _ in range(3)]
    st_s = [[] for _ in range(2)]
    kv_p = kv_s = ssm_s = None
    for i in range(depth):
        last = i == depth - 1
        pbd = jax.scipy.linalg.block_diag(*[pool_w[i, g] for g in range(len(POOL_WINDOWS))]).astype(bf16)
        pscale = row2(pool_scale[i])
        cb = row2(conv_b[i])
        dtb = row2(jnp.pad(dt_bias[i], (0, LANES - N_SSM_HEADS)))
        a_neg = -jnp.exp(a_log[i].astype(f32))
        a2_pad = row2(jnp.pad(a_neg * math.log2(math.e), (0, LANES - N_SSM_HEADS)))
        a_exp = row2(jnp.repeat(a_neg, SSM_HEAD_DIM))
        dskip = row2(jnp.repeat(d_skip[i], SSM_HEAD_DIM))
        norm = row2(ssm_norm[i])
        fin = row2(final_norm) if last else None

        yp = _ffn(yp, i, *f1)
        ya, q, k, v, yc, pst, cst, sst = _inproj_ssd_prompt(yp, i, batch, seq, g_mix, w_all, pbd, pscale, conv_w[i], cb, dtb,
                                                            cos_p, sin_p, cos_p * ATT_SCALE, sin_p * ATT_SCALE, a2_pad, e3_mat,
                                                            dskip, norm)
        yb = _attn_prompt(q, k, v, batch, seq)
        yp = _ffn(yp, i, *f2, mix=(ya, yb, yc, wo_all), final_g=fin)

        kv_p = _kv_window(i, depth, k, v, batch, seq, kv_p)
        st_p[0].append(pst[:, POOL_HALO - POOL_BUF:])
        st_p[1].append(cst[:, CONV_HALO - (CONV_WIDTH - 1):])
        st_p[2].append(jnp.transpose(sst.reshape(batch, SSM_STATE, N_SSM_HEADS, SSM_HEAD_DIM), (0, 2, 3, 1)))

        ys = _ffn(ys, i, *f1)
        cpool = jnp.transpose(cache_pool[i], (1, 0, 2))
        cconv = jnp.transpose(state_conv[i], (1, 0, 2))
        ya_s, q_s, k_s, v_s, z_s, u_s, dt_s, pnew, cnew = _sample_pre(ys, i, g_mix, w_all, cpool, pbd, pscale, cconv,
                                                                    conv_w[i], cb, dtb, cos_s, sin_s)
        yb_s, ks_buf, vs_buf = _sample_attn(i, q_s, k_s, v_s, ck_t, cv_t, kv_s)
        kv_s = (ks_buf, vs_buf)
        yc_s, ssm_s = _sample_ssd(i, u_s, dt_s, z_s, state_ssm.reshape(depth, n_dec, D_SSM, SSM_STATE), ssm_s, a_exp, e_mat,
                                  dskip, norm)
        yb_slab = jnp.transpose(yb_s.reshape(n_dec, 2, LANES), (1, 0, 2)).astype(bf16)
        ys = _ffn(ys, i, *f2, mix=(ya_s, yb_slab, yc_s, wo_all), final_g=fin)
        st_s[0].append(jnp.transpose(pnew, (1, 0, 2)))
        st_s[1].append(jnp.transpose(cnew, (1, 0, 2)))

    y_prompt = yp.reshape(batch, seq, D_MODEL)
    y_sample = ys.reshape(n_dec, 1, D_MODEL)
    pool_p, conv_p, ssm_p = (jnp.stack(s, 0) for s in st_p)
    pool_s, conv_s = (jnp.stack(s, 0) for s in st_s)
    ssm_s = ssm_s.reshape(state_ssm.shape)
    from_t = lambda t: jnp.transpose(t.reshape(depth, -1, N_ATT_HEADS, HEAD_DIM, wb), (0, 1, 4, 2, 3))
    k_p, v_p = from_t(kv_p[0]), from_t(kv_p[1])
    return (y_prompt, y_sample, pool_p, pool_s, k_p, from_t(kv_s[0]), v_p, from_t(kv_s[1]), conv_p, conv_s, ssm_p, ssm_s)
```

```python
import functools
import math

import jax
import jax.numpy as jnp
from jax import lax
from jax.experimental import pallas as pl
from jax.experimental.pallas import tpu as pltpu

f32 = jnp.float32
bf16 = jnp.bfloat16

D_MODEL = 1024
PAST_LEN = 16384
POOL_WINDOWS = (2, 4, 8, 16)
D_POOL = 256
POOL_GROUP = D_POOL // len(POOL_WINDOWS)
POOL_BUF = max(POOL_WINDOWS) - 1
HEAD_DIM = 64
N_ATT_HEADS = 4
D_ATT = N_ATT_HEADS * HEAD_DIM
DILATIONS = (16, 4, 1)
N_BACK = 128
ATT_WIN = 2048
ROPE_THETA = 10000.0
ATT_SCALE = math.log2(math.e) / math.sqrt(HEAD_DIM)
D_SSM = 512
SSM_HEAD_DIM = 64
N_SSM_HEADS = 8
SSM_STATE = 128
SSM_GROUPS = 2
CONV_WIDTH = 4
SSM_CHUNK = 128
D_CONV = D_SSM + 2 * SSM_GROUPS * SSM_STATE
D_IN_PROJ = D_POOL + 3 * D_ATT + D_SSM + D_CONV + N_SSM_HEADS
D_FF = 2816
RMS_EPS = 1e-6

LANES = 128
SUBLANES = 8
VMEM_LIMIT = 56 * 1024 * 1024

OFF_Q = D_POOL
OFF_K = OFF_Q + D_ATT
OFF_V = OFF_K + D_ATT
OFF_Z = OFF_V + D_ATT
OFF_XBC = OFF_Z + D_SSM
OFF_DT = OFF_XBC + D_CONV
N_PROJ = OFF_DT + LANES

ROW_TILE = 512
FFN_SPLIT = 2
ATT_TILE = 2048
ATT_UNROLL = 4
INPROJ_TILE = 1024
INPROJ_SPLIT = 4
POOL_HALO = 16
CONV_HALO = 8


def _params(n_axes, flags=None):
    return pltpu.CompilerParams(dimension_semantics=("arbitrary",) * n_axes, vmem_limit_bytes=VMEM_LIMIT, flags=flags)


def _const_spec(shape):
    nd = len(shape)
    return pl.BlockSpec(shape, lambda *_: (0,) * nd, pipeline_mode=pl.Buffered(1))


def _layer_spec(arr, layer):
    nd = arr.ndim - 1
    return pl.BlockSpec((None,) + arr.shape[1:], lambda *_: (layer,) + (0,) * nd, pipeline_mode=pl.Buffered(1))


def _rms(x, g):
    return x * lax.rsqrt(jnp.mean(x * x, -1, keepdims=True) + RMS_EPS) * g


def _dot_f32_by_01(x, m01, lhs_is_01=False):
    hi = x.astype(bf16)
    rest = x - hi.astype(f32)
    mid = rest.astype(bf16)
    lo = (rest - mid.astype(f32)).astype(bf16)
    out = None
    for part in (hi, mid, lo):
        if lhs_is_01:
            t = jnp.dot(m01, part, preferred_element_type=f32)
        else:
            t = jnp.dot(part, m01, preferred_element_type=f32)
        out = t if out is None else out + t
    return out


def _silu(x):
    return x * jax.nn.sigmoid(x)


def _softplus(x):
    return jnp.maximum(x, 0.0) + jnp.log1p(jnp.exp(-jnp.abs(x)))


def _rope_slab(x, cos, sin_signed):
    lane = lax.broadcasted_iota(jnp.int32, x.shape, 1)
    first_half = (lane % HEAD_DIM) < (HEAD_DIM // 2)
    partner = jnp.where(first_half, pltpu.roll(x, LANES - HEAD_DIM // 2, 1), pltpu.roll(x, HEAD_DIM // 2, 1))
    return x * cos + partner * sin_signed


def _pool_select(s2, s4, s8, s16):
    lane = lax.broadcasted_iota(jnp.int32, s2.shape, 1)
    return jnp.where(lane < POOL_GROUP, s2, jnp.where(lane < 2 * POOL_GROUP, s4, jnp.where(lane < 3 * POOL_GROUP, s8, s16)))


def _pool_window(shape):
    lane = lax.broadcasted_iota(jnp.int32, shape, 1)
    return jnp.where(lane < POOL_GROUP, POOL_WINDOWS[0],
                     jnp.where(lane < 2 * POOL_GROUP, POOL_WINDOWS[1],
                               jnp.where(lane < 3 * POOL_GROUP, POOL_WINDOWS[2], POOL_WINDOWS[3])))


def _ffn_kernel(has_mix, has_final, *refs):
    refs = list(refs)
    x_ref = refs.pop(0)
    if has_mix:
        ya_ref, yb_ref, yc_ref, wo_ref = refs[:4]
        refs = refs[4:]
    g_ref, wg_ref, wu_ref, wd_ref = refs[:4]
    refs = refs[4:]
    if has_final:
        fg_ref = refs.pop(0)
    o_ref = refs.pop(0)

    tm = x_ref.shape[0]
    n_sub = FFN_SPLIT if tm % (FFN_SPLIT * SUBLANES * 2) == 0 else 1
    subs = [slice(h * (tm // n_sub), (h + 1) * (tm // n_sub)) for h in range(n_sub)]
    xs = [x_ref[rows, :] for rows in subs]
    if has_mix:
        cats = [jnp.concatenate([ya_ref[rows, :], yb_ref[0, rows, :], yb_ref[1, rows, :], yc_ref[rows, :]], axis=1)
                for rows in subs]
        xs = [x + jnp.dot(cat, wo_ref[...], preferred_element_type=f32) for x, cat in zip(xs, cats)]
    xns = [_rms(x, g_ref[...]).astype(bf16) for x in xs]
    gates = [jnp.dot(xn, wg_ref[...], preferred_element_type=f32) for xn in xns]
    ups = [jnp.dot(xn, wu_ref[...], preferred_element_type=f32) for xn in xns]
    hs = [(_silu(gate) * up).astype(bf16) for gate, up in zip(gates, ups)]
    downs = [jnp.dot(h, wd_ref[...], preferred_element_type=f32) for h in hs]
    for rows, x, down in zip(subs, xs, downs):
        y = x + 0.5 * down
        if has_final:
            y = _rms(y, fg_ref[...])
        o_ref[rows, :] = y


def _ffn(x, layer, g, wg, wu, wd, mix=None, final_g=None):
    m = x.shape[0]
    tm = min(ROW_TILE, m)
    assert m % tm == 0
    row = lambda w: pl.BlockSpec((tm, w), lambda i: (i, 0))
    args, specs = [x], [row(D_MODEL)]
    if mix is not None:
        ya, yb, yc, wo = mix
        args += [ya, yb, yc, wo]
        specs += [row(D_POOL), pl.BlockSpec((2, tm, LANES), lambda i: (0, i, 0)), row(D_SSM), _layer_spec(wo, layer)]
    args += [g, wg, wu, wd]
    specs += [_layer_spec(g, layer), _layer_spec(wg, layer), _layer_spec(wu, layer), _layer_spec(wd, layer)]
    if final_g is not None:
        args.append(final_g)
        specs.append(_const_spec(final_g.shape))
    return pl.pallas_call(
        functools.partial(_ffn_kernel, mix is not None, final_g is not None),
        grid=(m // tm,),
        in_specs=specs,
        out_specs=row(D_MODEL),
        out_shape=jax.ShapeDtypeStruct((m, D_MODEL), f32),
        compiler_params=_params(1),
        name="ffn",
    )(*args)


def _inproj_ssd_kernel(tiles_per_seq, x_ref, g_ref, w_ref, pbd_ref, pscale_ref, cw_ref, cb_ref, dtb_ref, cos_ref, sin_ref,
                       cosq_ref, sinq_ref, a2_ref, e3_ref, dskip_ref, norm_ref,
                       ya_ref, q_ref, k_ref, v_ref, yc_ref, pst_ref, cst_ref, state_ref,
                       xa_buf, xbc_buf, u_buf, z_buf, dt_buf, st):
    tm = x_ref.shape[0]
    n_sub = INPROJ_SPLIT
    sub = tm // n_sub
    t = pl.program_id(0) % tiles_per_seq

    @pl.when(t == 0)
    def _():
        xa_buf[0:POOL_HALO, :] = jnp.zeros((POOL_HALO, D_POOL), f32)
        xbc_buf[0:CONV_HALO, :] = jnp.zeros((CONV_HALO, D_CONV), f32)
        st[...] = jnp.zeros_like(st)

    groups = (("xbc_dt", OFF_XBC, N_PROJ), ("xa", 0, OFF_Q), ("qkv", OFF_Q, OFF_Z), ("z", OFF_Z, OFF_XBC))
    projs = {}

    def project(h):
        xn = _rms(x_ref[h * sub:(h + 1) * sub, :], g_ref[...]).astype(bf16)
        projs[h] = {name: jnp.dot(xn, w_ref[:, a:b], preferred_element_type=f32) for name, a, b in groups}

    def tail(h):
        r0 = h * sub
        rows = slice(r0, r0 + sub)
        xbc, xa, qkv = projs[h]["xbc_dt"][:, 0:D_CONV], projs[h]["xa"], projs[h]["qkv"]

        xbc_buf[CONV_HALO + r0:CONV_HALO + r0 + sub, :] = xbc
        conv = cb_ref[...] + xbc * cw_ref[CONV_WIDTH - 1:CONV_WIDTH, :]
        for j in range(1, CONV_WIDTH):
            conv = conv + xbc_buf[CONV_HALO + r0 - j:CONV_HALO + r0 - j + sub, :] * cw_ref[CONV_WIDTH - 1 - j:CONV_WIDTH - j, :]
        u_buf[rows, :] = _silu(conv)
        dt_buf[rows, :] = _softplus(projs[h]["xbc_dt"][:, D_CONV:D_CONV + LANES] + dtb_ref[...])
        z_buf[rows, :] = projs[h]["z"]

        xa_buf[POOL_HALO + r0:POOL_HALO + r0 + sub, :] = xa
        ext = xa_buf[r0:r0 + sub + POOL_HALO, :]
        s2 = ext + pltpu.roll(ext, 1, 0)
        s4 = s2 + pltpu.roll(s2, 2, 0)
        s8 = s4 + pltpu.roll(s4, 4, 0)
        s16 = s8 + pltpu.roll(s8, 8, 0)
        sums = _pool_select(s2, s4, s8, s16)[POOL_HALO:POOL_HALO + sub, :]
        pos = t * tm + r0 + lax.broadcasted_iota(jnp.int32, (sub, D_POOL), 0)
        cnt = jnp.minimum(_pool_window((sub, D_POOL)), pos + 1).astype(f32)
        dmean = sums / cnt - xa
        ya = jnp.dot(dmean.astype(bf16), pbd_ref[...], preferred_element_type=f32) * pscale_ref[...]
        ya_ref[rows, :] = ya.astype(ya_ref.dtype)

        cos = cos_ref[rows, :]
        sin = sin_ref[rows, :]
        cosq = cosq_ref[rows, :]
        sinq = sinq_ref[rows, :]
        for s in range(2):
            q_ref[s, rows, :] = _rope_slab(qkv[:, s * LANES:(s + 1) * LANES], cosq, sinq)
            k_ref[s, rows, :] = _rope_slab(qkv[:, D_ATT + s * LANES:D_ATT + (s + 1) * LANES], cos, sin)
            v_ref[s, rows, :] = qkv[:, 2 * D_ATT + s * LANES:2 * D_ATT + (s + 1) * LANES]

    def carry_history():
        last_xbc = xbc_buf[tm:tm + CONV_HALO, :]
        cst_ref[0] = last_xbc
        xbc_buf[0:CONV_HALO, :] = last_xbc
        last_xa = xa_buf[tm:tm + POOL_HALO, :]
        pst_ref[0] = last_xa
        xa_buf[0:POOL_HALO, :] = last_xa

    q = SSM_CHUNK
    cps = sub // q
    gl = D_SSM // SSM_GROUPS
    hpg = N_SSM_HEADS // SSM_GROUPS
    li = lax.broadcasted_iota(jnp.int32, (q, q), 0)
    si = lax.broadcasted_iota(jnp.int32, (q, q), 1)
    causal = li >= si
    tril = causal.astype(bf16)
    triu = (li <= si).astype(bf16)
    lo = lax.broadcasted_iota(jnp.int32, (q, LANES), 1) < SSM_HEAD_DIM
    crows = lambda c: slice(c * q, (c + 1) * q)
    b_of = lambda u, g: u[:, D_SSM + g * SSM_STATE:D_SSM + (g + 1) * SSM_STATE]
    c_of = lambda u, g: u[:, D_SSM + (SSM_GROUPS + g) * SSM_STATE:D_SSM + (SSM_GROUPS + g + 1) * SSM_STATE]
    sgroups = range(SSM_GROUPS)
    gsl = lambda g: slice(g * gl, (g + 1) * gl)
    chunks_of = lambda h: range(h * cps, (h + 1) * cps)
    v = {}

    def level1(h):
        for c in chunks_of(h):
            u = u_buf[crows(c), :]
            dtp = dt_buf[crows(c), :]
            dta = dtp * a2_ref[...]
            v["dtp", c] = dtp
            v["cum3", c] = jnp.dot(tril, jnp.concatenate(_split3(dta), axis=1), preferred_element_type=f32)
            v["acs_row", c] = _dot_f32_by_01(dta.T[0:SUBLANES], triu)
            for g in sgroups:
                v["cb", c, g] = lax.dot_general(c_of(u, g).astype(bf16), b_of(u, g).astype(bf16), (((1,), (1,)), ((), ())),
                                                preferred_element_type=f32)

    def level2(h):
        for c in chunks_of(h):
            cum3 = v["cum3", c]
            cum = cum3[:, 0:LANES] + cum3[:, LANES:2 * LANES] + cum3[:, 2 * LANES:3 * LANES]
            v["both", c] = jnp.dot(jnp.concatenate(_split3(jnp.concatenate([v["dtp", c], cum], axis=0)), axis=1), e3_ref[...],
                                   preferred_element_type=f32)

    def level3(h):
        for c in chunks_of(h):
            u = u_buf[crows(c), :]
            dt_exp, acs_exp, acs_row = v["both", c][0:q], v["both", c][q:2 * q], v["acs_row", c]
            xdt = u[:, 0:D_SSM] * dt_exp
            xdt_b = xdt.astype(bf16)
            parts = []
            for g in sgroups:
                for pair in range(hpg // 2):
                    mats = []
                    for hh in range(2):
                        hd = g * hpg + pair * 2 + hh
                        col = acs_exp[:, hd * SSM_HEAD_DIM:hd * SSM_HEAD_DIM + 1]
                        diff = jnp.where(causal, col - acs_row[hd:hd + 1, :], -jnp.inf)
                        mats.append((v["cb", c, g] * jnp.exp2(diff)).astype(bf16))
                    c0 = (g * hpg + pair * 2) * SSM_HEAD_DIM
                    xp = xdt_b[:, c0:c0 + LANES]
                    rhs = jnp.concatenate([jnp.where(lo, xp, jnp.zeros_like(xp)), jnp.where(lo, jnp.zeros_like(xp), xp)],
                                          axis=0)
                    parts.append(jnp.dot(jnp.concatenate(mats, axis=1), rhs, preferred_element_type=f32))
            v["ydiag", c] = jnp.concatenate(parts, axis=1)
            acs_last = acs_exp[q - 1:q, :]
            xw = (xdt * jnp.exp2(acs_last - acs_exp)).astype(bf16)
            v["s_new", c] = jnp.concatenate(
                [jnp.dot(b_of(u, g).T.astype(bf16), xw[:, gsl(g)], preferred_element_type=f32) for g in sgroups], axis=1)
            v["acs", c] = acs_exp

    def level4():
        state = st[...]
        for c in range(n_sub * cps):
            u = u_buf[crows(c), :]
            state_b = state.astype(bf16)
            v["yoff", c] = jnp.concatenate(
                [jnp.dot(c_of(u, g).astype(bf16), state_b[:, gsl(g)], preferred_element_type=f32) for g in sgroups], axis=1)
            state = state * jnp.exp2(v["acs", c][q - 1:q, :]) + v["s_new", c]
        st[...] = state
        state_ref[0] = state

    def level5():
        for c in range(n_sub * cps):
            xs = u_buf[crows(c), 0:D_SSM]
            y = v["ydiag", c] + v["yoff", c] * jnp.exp2(v["acs", c])
            y = (y + dskip_ref[...] * xs) * _silu(z_buf[crows(c), :])
            outs = []
            for g in sgroups:
                yg = y[:, gsl(g)]
                outs.append(yg * lax.rsqrt(jnp.mean(yg * yg, -1, keepdims=True) + RMS_EPS))
            yc_ref[crows(c), :] = (jnp.concatenate(outs, axis=1) * norm_ref[...]).astype(yc_ref.dtype)

    project(0)
    project(1)
    for s in range(n_sub + 3):
        if s < n_sub:
            tail(s)
        if s + 2 < n_sub:
            project(s + 2)
        for level, lag in ((level1, 1), (level2, 2), (level3, 3)):
            if 0 <= s - lag < n_sub:
                level(s - lag)
    carry_history()
    level4()
    level5()


def _inproj_ssd_prompt(x, layer, batch, seq, g, w, pbd, pscale, cw, cb, dtb, cos, sin, cosq, sinq, a2_pad, e3_mat, dskip, norm):
    m = x.shape[0]
    tm = INPROJ_TILE
    tps = seq // tm
    row = lambda wd: pl.BlockSpec((tm, wd), lambda i: (i, 0))
    slab = pl.BlockSpec((2, tm, LANES), lambda i: (0, i, 0))
    tab = pl.BlockSpec((tm, LANES), lambda i: (i % tps, 0))
    per_seq = lambda r, c: pl.BlockSpec((1, r, c), lambda i: (i // tps, 0, 0))
    out_shape = (
        jax.ShapeDtypeStruct((m, D_POOL), bf16),
        jax.ShapeDtypeStruct((2, m, LANES), f32),
        jax.ShapeDtypeStruct((2, m, LANES), f32),
        jax.ShapeDtypeStruct((2, m, LANES), f32),
        jax.ShapeDtypeStruct((m, D_SSM), bf16),
        jax.ShapeDtypeStruct((batch, POOL_HALO, D_POOL), f32),
        jax.ShapeDtypeStruct((batch, CONV_HALO, D_CONV), f32),
        jax.ShapeDtypeStruct((batch, SSM_STATE, D_SSM), f32),
    )
    out_specs = (row(D_POOL), slab, slab, slab, row(D_SSM), per_seq(POOL_HALO, D_POOL), per_seq(CONV_HALO, D_CONV),
                 per_seq(SSM_STATE, D_SSM))
    consts = [pbd, pscale, cw, cb, dtb]
    ssd_consts = [a2_pad, e3_mat, dskip, norm]
    in_specs = ([row(D_MODEL), _layer_spec(g, layer), _layer_spec(w, layer)] + [_layer_spec(c, layer) for c in consts]
                + [tab, tab, tab, tab]
                + [_layer_spec(a2_pad, layer), _const_spec(e3_mat.shape), _layer_spec(dskip, layer), _layer_spec(norm, layer)])
    return pl.pallas_call(
        functools.partial(_inproj_ssd_kernel, tps),
        grid=(m // tm,),
        in_specs=in_specs,
        out_specs=out_specs,
        out_shape=out_shape,
        scratch_shapes=[pltpu.VMEM((POOL_HALO + tm, D_POOL), f32), pltpu.VMEM((CONV_HALO + tm, D_CONV), f32),
                        pltpu.VMEM((tm, D_CONV), f32), pltpu.VMEM((tm, D_SSM), f32), pltpu.VMEM((tm, LANES), f32),
                        pltpu.VMEM((SSM_STATE, D_SSM), f32)],
        compiler_params=_params(1),
        name="inproj_ssd",
    )(x, g, w, *consts, cos, sin, cosq, sinq, *ssd_consts)


def _attn_kernel(has_prev, q_ref, kp_ref, kc_ref, vp_ref, vc_ref, *rest):
    if has_prev:
        rest = rest[2:]
    o_ref, kwin_ref, vwin_ref, acc, mrun, lrun = rest
    t = pl.program_id(2)
    blk = N_BACK

    @pl.when(t == pl.num_programs(2) - 1)
    def _():
        kwin_ref[0] = kc_ref[0, 0].T
        vwin_ref[0] = vc_ref[0, 0].T

    qi = lax.broadcasted_iota(jnp.int32, (2 * blk, 2 * blk), 0) % blk
    kj = lax.broadcasted_iota(jnp.int32, (2 * blk, 2 * blk), 1)
    dist = blk + qi - kj
    band = (dist >= 0) & (dist <= N_BACK)
    bias_full = jnp.where(band, 0.0, -jnp.inf).astype(f32)
    bias_cur = jnp.where(band & (kj >= blk), 0.0, -jnp.inf).astype(f32)
    bias_first = jnp.where(t > 0, bias_full, bias_cur)
    head0 = lax.broadcasted_iota(jnp.int32, (blk, LANES), 1) < HEAD_DIM
    head0_k = lax.broadcasted_iota(jnp.int32, (2 * blk, LANES), 1) < HEAD_DIM
    ones_blk = jnp.concatenate([jnp.where(head0_k, 1.0, 0.0), jnp.where(head0_k, 0.0, 1.0)], axis=0).astype(bf16)
    n_units = ATT_TILE // blk

    def strided(ref, start, n, d):
        return ref[0, 0, pl.ds(start, n, stride=d) if d > 1 else pl.ds(start, n), :]

    def group(d, units):
        first = d == DILATIONS[0]
        last = d == DILATIONS[-1]
        span = blk * d
        rows, scores, vcats = [], [], []
        for u in units:
            sb, r = divmod(u, d)
            qstart = sb * span + r
            rows.append(pl.ds(qstart, blk, stride=d) if d > 1 else pl.ds(qstart, blk))
            qb = strided(q_ref, qstart, blk, d).astype(bf16)
            if sb > 0:
                kf = strided(kc_ref, qstart - span, 2 * blk, d)
                vf = strided(vc_ref, qstart - span, 2 * blk, d)
                bias = bias_full
            else:
                kf = jnp.concatenate([strided(kp_ref, ATT_TILE - span + r, blk, d), strided(kc_ref, r, blk, d)], axis=0)
                vf = jnp.concatenate([strided(vp_ref, ATT_TILE - span + r, blk, d), strided(vc_ref, r, blk, d)], axis=0)
                bias = bias_first
            kb = kf.astype(bf16)
            vb = vf.astype(bf16)
            qcat = jnp.concatenate([jnp.where(head0, qb, jnp.zeros_like(qb)), jnp.where(head0, jnp.zeros_like(qb), qb)], axis=0)
            scores.append(lax.dot_general(qcat, kb, (((1,), (1,)), ((), ())), preferred_element_type=f32) + bias)
            vcat = jnp.concatenate([jnp.where(head0_k, vb, jnp.zeros_like(vb)), jnp.where(head0_k, jnp.zeros_like(vb), vb)],
                                   axis=0)
            vcats.append(jnp.concatenate([vcat, ones_blk], axis=1))
        stats = []
        for sc in scores:
            m2 = jnp.max(sc, axis=1, keepdims=True)
            pb = jnp.exp2(sc - m2).astype(bf16)
            stats.append((jnp.concatenate([pb[0:blk], pb[blk:2 * blk]], axis=1), jnp.where(head0, m2[0:blk], m2[blk:2 * blk])))
        pvs = [jnp.dot(pcat, vcat, preferred_element_type=f32) for (pcat, _), vcat in zip(stats, vcats)]
        for qrows, pvl, (_, m_e) in zip(rows, pvs, stats):
            pv, l_e = pvl[:, 0:LANES], pvl[:, LANES:2 * LANES]
            if first:
                acc[qrows, :] = pv
                mrun[qrows, :] = m_e
                lrun[qrows, :] = l_e
            else:
                m_old = mrun[qrows, :]
                m_new = jnp.maximum(m_old, m_e)
                a = jnp.exp2(m_old - m_new)
                b = jnp.exp2(m_e - m_new)
                acc_new = acc[qrows, :] * a + pv * b
                l_new = lrun[qrows, :] * a + l_e * b
                if last:
                    o_ref[0, 0, qrows, :] = (acc_new / l_new).astype(o_ref.dtype)
                else:
                    acc[qrows, :] = acc_new
                    lrun[qrows, :] = l_new
                    mrun[qrows, :] = m_new

    for d in DILATIONS:
        for g0 in range(0, n_units, ATT_UNROLL):
            group(d, range(g0, g0 + ATT_UNROLL))


def _attn_prompt(layer, depth, q, k, v, batch, seq, prev):
    assert ATT_TILE == ATT_WIN
    nt = seq // ATT_TILE
    q4 = q.reshape(2, batch, seq, LANES)
    k4 = k.reshape(2, batch, seq, LANES)
    v4 = v.reshape(2, batch, seq, LANES)
    cur = pl.BlockSpec((1, 1, ATT_TILE, LANES), lambda b, s, t: (s, b, t, 0))
    before = pl.BlockSpec((1, 1, ATT_TILE, LANES), lambda b, s, t: (s, b, jnp.maximum(t - 1, 0), 0))
    win = pl.BlockSpec((None, 1, LANES, ATT_WIN), lambda b, s, t: (layer, b, s, 0))
    win_shape = jax.ShapeDtypeStruct((depth, batch, D_ATT, ATT_WIN), f32)
    args, in_specs, aliases = [q4, k4, k4, v4, v4], [cur, before, cur, before, cur], {}
    if prev is not None:
        in_specs += [pl.BlockSpec(memory_space=pl.ANY)] * 2
        aliases = {5: 1, 6: 2}
        args += list(prev)
    out, kwin, vwin = pl.pallas_call(
        functools.partial(_attn_kernel, prev is not None),
        grid=(batch, 2, nt),
        in_specs=in_specs,
        out_specs=(cur, win, win),
        out_shape=(jax.ShapeDtypeStruct((2, batch, seq, LANES), bf16), win_shape, win_shape),
        scratch_shapes=[pltpu.VMEM((ATT_TILE, LANES), f32), pltpu.VMEM((ATT_TILE, LANES), f32),
                        pltpu.VMEM((ATT_TILE, LANES), f32)],
        input_output_aliases=aliases,
        compiler_params=_params(3),
        name="attn",
    )(*args)
    return out.reshape(2, batch * seq, LANES), (kwin, vwin)


def _split3(x):
    hi = x.astype(bf16)
    rest = x - hi.astype(f32)
    mid = rest.astype(bf16)
    lo = (rest - mid.astype(f32)).astype(bf16)
    return hi, mid, lo


def _sample_pre_kernel(x_ref, g_ref, w_ref, cpool_ref, pbd_ref, pscale_ref, cconv_ref, cw_ref, cb_ref, dtb_ref,
                       cos_ref, sin_ref, ya_ref, q_ref, k_ref, v_ref, z_ref, u_ref, dt_ref, pnew_ref, cnew_ref):
    xn = _rms(x_ref[...], g_ref[...]).astype(bf16)
    proj = jnp.dot(xn, w_ref[...], preferred_element_type=f32)

    xa = proj[:, 0:D_POOL]
    back = lambda i: cpool_ref[POOL_BUF - i]
    s2 = xa + back(1)
    s4 = s2 + back(2) + back(3)
    s8 = s4 + back(4) + back(5) + back(6) + back(7)
    s16 = s8
    for i in range(8, 16):
        s16 = s16 + back(i)
    cnt = jnp.minimum(_pool_window(xa.shape), PAST_LEN + 1).astype(f32)
    dmean = _pool_select(s2, s4, s8, s16) / cnt - xa
    ya = jnp.dot(dmean.astype(bf16), pbd_ref[...], preferred_element_type=f32) * pscale_ref[...]
    ya_ref[...] = ya.astype(ya_ref.dtype)
    for i in range(POOL_BUF - 1):
        pnew_ref[i] = cpool_ref[i + 1]
    pnew_ref[POOL_BUF - 1] = xa

    cos = cos_ref[...]
    sin = sin_ref[...]
    for s in range(2):
        sl = slice(s * LANES, (s + 1) * LANES)
        q_ref[:, sl] = _rope_slab(proj[:, OFF_Q + s * LANES:OFF_Q + (s + 1) * LANES], cos, sin)
        k_ref[:, sl] = _rope_slab(proj[:, OFF_K + s * LANES:OFF_K + (s + 1) * LANES], cos, sin)
    v_ref[...] = proj[:, OFF_V:OFF_V + D_ATT]
    z_ref[...] = proj[:, OFF_Z:OFF_Z + D_SSM]

    xbc = proj[:, OFF_XBC:OFF_XBC + D_CONV]
    conv = cb_ref[...] + xbc * cw_ref[CONV_WIDTH - 1:CONV_WIDTH, :]
    for j in range(1, CONV_WIDTH):
        conv = conv + cconv_ref[CONV_WIDTH - 1 - j] * cw_ref[CONV_WIDTH - 1 - j:CONV_WIDTH - j, :]
    u_ref[...] = _silu(conv)
    for j in range(CONV_WIDTH - 2):
        cnew_ref[j] = cconv_ref[j + 1]
    cnew_ref[CONV_WIDTH - 2] = xbc

    dt_ref[...] = _softplus(proj[:, OFF_DT:OFF_DT + LANES] + dtb_ref[...])


def _sample_pre(x, layer, g, w, cpool, pbd, pscale, cconv, cw, cb, dtb, cos, sin):
    n = x.shape[0]
    args = [x, g, w, cpool, pbd, pscale, cconv, cw, cb, dtb, cos, sin]
    out_shape = (
        jax.ShapeDtypeStruct((n, D_POOL), bf16),
        jax.ShapeDtypeStruct((n, D_ATT), f32),
        jax.ShapeDtypeStruct((n, D_ATT), f32),
        jax.ShapeDtypeStruct((n, D_ATT), f32),
        jax.ShapeDtypeStruct((n, D_SSM), f32),
        jax.ShapeDtypeStruct((n, D_CONV), f32),
        jax.ShapeDtypeStruct((n, LANES), f32),
        jax.ShapeDtypeStruct((POOL_BUF, n, D_POOL), f32),
        jax.ShapeDtypeStruct((CONV_WIDTH - 1, n, D_CONV), f32),
    )
    full = lambda s: pl.BlockSpec(s.shape, lambda i, nd=len(s.shape): (0,) * nd)
    in_specs = [full(x)] + [_layer_spec(a, layer) for a in args[1:-2]] + [full(cos), full(sin)]
    return pl.pallas_call(
        _sample_pre_kernel,
        grid=(1,),
        in_specs=in_specs,
        out_specs=tuple(full(s) for s in out_shape),
        out_shape=out_shape,
        compiler_params=_params(1),
        name="sample_pre",
    )(*args)


def _sample_ssd_kernel(u_ref, dt_ref, z_ref, st_ref, aexp_ref, e_ref, dskip_ref, norm_ref, *rest):
    y_ref, stnew_ref = rest[-2:]
    n = u_ref.shape[0]
    gl = D_SSM // SSM_GROUPS
    u = u_ref[...]
    xs = u[:, 0:D_SSM]
    dt_exp = _dot_f32_by_01(dt_ref[...], e_ref[...])
    pad = jnp.zeros((LANES - n, D_SSM), f32)
    dec_t = jnp.concatenate([jnp.exp(dt_exp * aexp_ref[...]), pad], axis=0).T
    dtx_t = jnp.concatenate([dt_exp * xs, pad], axis=0).T
    lane = lax.broadcasted_iota(jnp.int32, (D_SSM, LANES), 1)
    y_t = jnp.zeros((D_SSM, LANES), f32)
    for i in range(n):
        dec = dec_t[:, i:i + 1]
        dtx = dtx_t[:, i:i + 1]
        ycols = []
        for g in range(SSM_GROUPS):
            rows = slice(g * gl, (g + 1) * gl)
            b_row = u[i:i + 1, D_SSM + g * SSM_STATE:D_SSM + (g + 1) * SSM_STATE]
            c_row = u[i:i + 1, D_SSM + (SSM_GROUPS + g) * SSM_STATE:D_SSM + (SSM_GROUPS + g + 1) * SSM_STATE]
            h_new = dec[rows] * st_ref[i, rows, :] + dtx[rows] * b_row
            stnew_ref[i, rows, :] = h_new
            ycols.append(jnp.sum(h_new * c_row, axis=1, keepdims=True))
        y_t = jnp.where(lane == i, jnp.concatenate(ycols, axis=0), y_t)
    y = y_t.T[0:n]
    y = (y + dskip_ref[...] * xs) * _silu(z_ref[...])
    outs = []
    for g in range(SSM_GROUPS):
        yg = y[:, g * gl:(g + 1) * gl]
        outs.append(yg * lax.rsqrt(jnp.mean(yg * yg, -1, keepdims=True) + RMS_EPS))
    y_ref[...] = (jnp.concatenate(outs, axis=1) * norm_ref[...]).astype(y_ref.dtype)


def _sample_ssd(layer, u, dt, z, state, prev, a_exp, e_mat, dskip, norm):
    n = u.shape[0]
    full = lambda s: pl.BlockSpec(s.shape, lambda i, nd=len(s.shape): (0,) * nd)
    of_layer = pl.BlockSpec((None,) + state.shape[1:], lambda i: (layer, 0, 0, 0))
    args = [u, dt, z, state, a_exp, e_mat, dskip, norm]
    in_specs = [full(u), full(dt), full(z), of_layer, _layer_spec(a_exp, layer), full(e_mat), _layer_spec(dskip, layer),
                _layer_spec(norm, layer)]
    aliases = {}
    if prev is not None:
        in_specs.append(pl.BlockSpec(memory_space=pl.ANY))
        aliases = {len(args): 1}
        args.append(prev)
    y_shape = jax.ShapeDtypeStruct((n, D_SSM), bf16)
    return pl.pallas_call(
        _sample_ssd_kernel,
        grid=(1,),
        in_specs=in_specs,
        out_specs=(full(y_shape), of_layer),
        out_shape=(y_shape, jax.ShapeDtypeStruct(state.shape, f32)),
        input_output_aliases=aliases,
        compiler_params=_params(1),
        name="sample_ssd",
    )(*args)


def _sample_attn_kernel(has_prev, q_ref, kn_ref, vn_ref, knc_ref, vnc_ref, kc_ref, vc_ref, *rest):
    if has_prev:
        rest = rest[2:]
    y_ref, ks_ref, vs_ref = rest
    wb = kc_ref.shape[3]
    kc = kc_ref[0, 0]
    vc = vc_ref[0, 0]
    q = q_ref[0]
    kn = kn_ref[0]
    vn = vn_ref[0]
    scale = 1.0 / math.sqrt(HEAD_DIM)

    own = (lax.broadcasted_iota(jnp.int32, (SUBLANES, D_ATT), 1) // HEAD_DIM) == lax.broadcasted_iota(
        jnp.int32, (SUBLANES, D_ATT), 0)
    qbd = jnp.where(own, jnp.broadcast_to(q, (SUBLANES, D_ATT)), 0.0)
    s_all = jnp.dot(qbd.astype(bf16), kc.astype(bf16), preferred_element_type=f32) * scale
    s_new = jnp.sum(jnp.where(own, jnp.broadcast_to(q * kn, (SUBLANES, D_ATT)), 0.0), axis=1, keepdims=True) * scale
    dist = wb - lax.broadcasted_iota(jnp.int32, (SUBLANES, wb), 1)
    ms, ps, ls, es = [], [], [], []
    for d in DILATIONS:
        valid = ((dist % d) == 0) & (dist <= N_BACK * d)
        s = jnp.where(valid, s_all, -jnp.inf)
        m = jnp.maximum(jnp.max(s, axis=1, keepdims=True), s_new)
        p = jnp.exp(s - m)
        e_new = jnp.exp(s_new - m)
        ms.append(m)
        ps.append(p)
        es.append(e_new)
        ls.append(jnp.sum(p, axis=1, keepdims=True) + e_new)
    m_all = jnp.maximum(jnp.maximum(ms[0], ms[1]), ms[2])
    p_all = jnp.zeros_like(ps[0])
    w_new = jnp.zeros_like(s_new)
    l_all = jnp.zeros_like(s_new)
    for m, p, e_new, l in zip(ms, ps, es, ls):
        c = jnp.exp(m - m_all)
        p_all = p_all + c * p
        w_new = w_new + c * e_new
        l_all = l_all + c * l
    o_full = lax.dot_general(p_all.astype(bf16), vc.astype(bf16), (((1,), (1,)), ((), ())), preferred_element_type=f32)
    to_row = lambda t: jnp.sum(jnp.where(own, t, 0.0), axis=0, keepdims=True)
    o_row = to_row(o_full) + to_row(jnp.broadcast_to(w_new, (SUBLANES, D_ATT))) * vn
    y_ref[0] = o_row / to_row(jnp.broadcast_to(l_all, (SUBLANES, D_ATT)))

    newest = lax.broadcasted_iota(jnp.int32, (D_ATT, wb), 1) == wb - 1
    ks_ref[0, 0] = jnp.where(newest, knc_ref[0], pltpu.roll(kc, wb - 1, 1))
    vs_ref[0, 0] = jnp.where(newest, vnc_ref[0], pltpu.roll(vc, wb - 1, 1))


def _sample_attn(layer, q, kn, vn, cache_k, cache_v, prev):
    depth, n, da, wb = cache_k.shape
    rowv = pl.BlockSpec((1, 1, da), lambda i: (i, 0, 0))
    colv = pl.BlockSpec((1, da, 1), lambda i: (i, 0, 0))
    big = pl.BlockSpec((1, 1, da, wb), lambda i: (layer, i, 0, 0))
    in_specs = [rowv, rowv, rowv, colv, colv, big, big]
    args = [q.reshape(n, 1, da), kn.reshape(n, 1, da), vn.reshape(n, 1, da), kn.reshape(n, da, 1), vn.reshape(n, da, 1),
            cache_k, cache_v]
    aliases = {}
    if prev is not None:
        in_specs += [pl.BlockSpec(memory_space=pl.ANY)] * 2
        aliases = {len(args): 1, len(args) + 1: 2}
        args += list(prev)
    return pl.pallas_call(
        functools.partial(_sample_attn_kernel, prev is not None),
        grid=(n,),
        in_specs=in_specs,
        out_specs=(rowv, big, big),
        out_shape=(jax.ShapeDtypeStruct((n, 1, da), f32), jax.ShapeDtypeStruct(cache_k.shape, f32),
                   jax.ShapeDtypeStruct(cache_v.shape, f32)),
        input_output_aliases=aliases,
        compiler_params=_params(1),
        name="sample_attn",
    )(*args)


def _rope_tables(pos):
    half = HEAD_DIM // 2
    inv = ROPE_THETA ** (-jnp.arange(half, dtype=f32) / half)
    ang = pos.astype(f32)[:, None] * inv[None]
    cos = jnp.tile(jnp.cos(ang), (1, LANES // half))
    sin = jnp.sin(ang)
    sin_signed = jnp.tile(jnp.concatenate([-sin, sin], axis=1), (1, LANES // HEAD_DIM))
    return cos, sin_signed


def kernel(x_prompt, x_sample, cache_pool, cache_k, cache_v, state_conv, state_ssm, ffn1_norm, ffn1_w_gate, ffn1_w_up,
           ffn1_w_down, mix_norm, w_in, pool_w, pool_scale, conv_w, conv_b, dt_bias, a_log, d_skip, ssm_norm, w_out,
           ffn2_norm, ffn2_w_gate, ffn2_w_up, ffn2_w_down, final_norm):
    batch, seq, _ = x_prompt.shape
    n_dec = x_sample.shape[0]
    depth = w_in.shape[0]
    wb = cache_k.shape[2]
    assert all(seq % tile == 0 for tile in (ATT_TILE, ROW_TILE, INPROJ_TILE))
    assert x_sample.shape[1] == 1 and wb == ATT_WIN

    yp = x_prompt.reshape(batch * seq, D_MODEL)
    ys = x_sample.reshape(n_dec, D_MODEL)
    cos_p, sin_p = _rope_tables(jnp.arange(seq))
    cos_s, sin_s = _rope_tables(jnp.full((1,), PAST_LEN))
    e_mat = (jnp.arange(LANES)[:, None] == (jnp.arange(D_SSM)[None, :] // SSM_HEAD_DIM)).astype(bf16)
    e3_mat = jnp.tile(e_mat, (3, 1))
    ck_t = jnp.transpose(cache_k, (0, 1, 3, 4, 2)).reshape(depth, n_dec, D_ATT, wb)
    cv_t = jnp.transpose(cache_v, (0, 1, 3, 4, 2)).reshape(depth, n_dec, D_ATT, wb)
    row2 = lambda a: a.reshape(1, -1)
    row3 = lambda a: a.reshape(depth, 1, -1)

    f1 = (row3(ffn1_norm), ffn1_w_gate.astype(bf16), ffn1_w_up.astype(bf16), ffn1_w_down.astype(bf16))
    f2 = (row3(ffn2_norm), ffn2_w_gate.astype(bf16), ffn2_w_up.astype(bf16), ffn2_w_down.astype(bf16))
    w_all = jnp.pad(w_in, ((0, 0), (0, 0), (0, N_PROJ - D_IN_PROJ))).astype(bf16)
    wo_all = w_out.astype(bf16)
    g_mix = row3(mix_norm)
    n_pool = len(POOL_WINDOWS)
    pbd = (pool_w[:, :, :, None, :] * jnp.eye(n_pool, dtype=f32)[None, :, None, :, None]).reshape(
        depth, D_POOL, D_POOL).astype(bf16)
    pscale = row3(pool_scale)
    cb = row3(conv_b)
    head_pad = ((0, 0), (0, LANES - N_SSM_HEADS))
    dtb = row3(jnp.pad(dt_bias, head_pad))
    a_neg = -jnp.exp(a_log.astype(f32))
    a2_pad = row3(jnp.pad(a_neg * math.log2(math.e), head_pad))
    a_exp = row3(jnp.repeat(a_neg, SSM_HEAD_DIM, axis=1))
    dskip = row3(jnp.repeat(d_skip, SSM_HEAD_DIM, axis=1))
    norm = row3(ssm_norm)
    cpool = jnp.transpose(cache_pool, (0, 2, 1, 3))
    cconv = jnp.transpose(state_conv, (0, 2, 1, 3))
    cosq_p, sinq_p = cos_p * ATT_SCALE, sin_p * ATT_SCALE

    st_p = [[] for _ in range(3)]
    st_s = [[] for _ in range(2)]
    kv_p = kv_s = ssm_s = None
    for i in range(depth):
        last = i == depth - 1
        fin = row2(final_norm) if last else None

        yp = _ffn(yp, i, *f1)
        ya, q, k, v, yc, pst, cst, sst = _inproj_ssd_prompt(yp, i, batch, seq, g_mix, w_all, pbd, pscale, conv_w, cb, dtb,
                                                            cos_p, sin_p, cosq_p, sinq_p, a2_pad, e3_mat, dskip, norm)
        yb, kv_p = _attn_prompt(i, depth, q, k, v, batch, seq, kv_p)
        yp = _ffn(yp, i, *f2, mix=(ya, yb, yc, wo_all), final_g=fin)

        st_p[0].append(pst[:, POOL_HALO - POOL_BUF:])
        st_p[1].append(cst[:, CONV_HALO - (CONV_WIDTH - 1):])
        st_p[2].append(jnp.transpose(sst.reshape(batch, SSM_STATE, N_SSM_HEADS, SSM_HEAD_DIM), (0, 2, 3, 1)))

        ys = _ffn(ys, i, *f1)
        ya_s, q_s, k_s, v_s, z_s, u_s, dt_s, pnew, cnew = _sample_pre(ys, i, g_mix, w_all, cpool, pbd, pscale, cconv,
                                                                    conv_w, cb, dtb, cos_s, sin_s)
        yb_s, ks_buf, vs_buf = _sample_attn(i, q_s, k_s, v_s, ck_t, cv_t, kv_s)
        kv_s = (ks_buf, vs_buf)
        yc_s, ssm_s = _sample_ssd(i, u_s, dt_s, z_s, state_ssm.reshape(depth, n_dec, D_SSM, SSM_STATE), ssm_s, a_exp, e_mat,
                                  dskip, norm)
        yb_slab = jnp.transpose(yb_s.reshape(n_dec, 2, LANES), (1, 0, 2)).astype(bf16)
        ys = _ffn(ys, i, *f2, mix=(ya_s, yb_slab, yc_s, wo_all), final_g=fin)
        st_s[0].append(jnp.transpose(pnew, (1, 0, 2)))
        st_s[1].append(jnp.transpose(cnew, (1, 0, 2)))

    y_prompt = yp.reshape(batch, seq, D_MODEL)
    y_sample = ys.reshape(n_dec, 1, D_MODEL)
    pool_p, conv_p, ssm_p = (jnp.stack(s, 0) for s in st_p)
    pool_s, conv_s = (jnp.stack(s, 0) for s in st_s)
    ssm_s = ssm_s.reshape(state_ssm.shape)
    from_t = lambda t: jnp.transpose(t.reshape(depth, -1, N_ATT_HEADS, HEAD_DIM, wb), (0, 1, 4, 2, 3))
    k_p, v_p = from_t(kv_p[0]), from_t(kv_p[1])
    return (y_prompt, y_sample, pool_p, pool_s, k_p, from_t(kv_s[0]), v_p, from_t(kv_s[1]), conv_p, conv_s, ssm_p, ssm_s)
```

```python
import functools
import math

import jax
import jax.numpy as jnp
from jax import lax
from jax.experimental import pallas as pl
from jax.experimental.pallas import tpu as pltpu

f32 = jnp.float32
bf16 = jnp.bfloat16

D_MODEL = 1024
PAST_LEN = 16384
POOL_WINDOWS = (2, 4, 8, 16)
D_POOL = 256
POOL_GROUP = D_POOL // len(POOL_WINDOWS)
POOL_BUF = max(POOL_WINDOWS) - 1
HEAD_DIM = 64
N_ATT_HEADS = 4
D_ATT = N_ATT_HEADS * HEAD_DIM
DILATIONS = (16, 4, 1)
N_BACK = 128
ATT_WIN = 2048
ROPE_THETA = 10000.0
ATT_SCALE = math.log2(math.e) / math.sqrt(HEAD_DIM)
D_SSM = 512
SSM_HEAD_DIM = 64
N_SSM_HEADS = 8
SSM_STATE = 128
SSM_GROUPS = 2
CONV_WIDTH = 4
SSM_CHUNK = 128
D_CONV = D_SSM + 2 * SSM_GROUPS * SSM_STATE
D_IN_PROJ = D_POOL + 3 * D_ATT + D_SSM + D_CONV + N_SSM_HEADS
D_FF = 2816
RMS_EPS = 1e-6

LANES = 128
SUBLANES = 8
VMEM_LIMIT = 56 * 1024 * 1024

OFF_Q = D_POOL
OFF_K = OFF_Q + D_ATT
OFF_V = OFF_K + D_ATT
OFF_Z = OFF_V + D_ATT
OFF_XBC = OFF_Z + D_SSM
OFF_DT = OFF_XBC + D_CONV
N_PROJ = OFF_DT + LANES

ROW_TILE = 512
FFN_SPLIT = 2
ATT_TILE = 2048
ATT_UNROLL = 4
INPROJ_TILE = 1024
INPROJ_SPLIT = 2
POOL_HALO = 16
CONV_HALO = 8


def _params(n_axes):
    return pltpu.CompilerParams(dimension_semantics=("arbitrary",) * n_axes, vmem_limit_bytes=VMEM_LIMIT)


def _const_spec(shape):
    nd = len(shape)
    return pl.BlockSpec(shape, lambda *_: (0,) * nd, pipeline_mode=pl.Buffered(1))


def _layer_spec(arr, layer):
    nd = arr.ndim - 1
    return pl.BlockSpec((None,) + arr.shape[1:], lambda *_: (layer,) + (0,) * nd, pipeline_mode=pl.Buffered(1))


def _rms(x, g):
    return x * lax.rsqrt(jnp.mean(x * x, -1, keepdims=True) + RMS_EPS) * g


def _split3(x):
    hi = x.astype(bf16)
    rest = x - hi.astype(f32)
    mid = rest.astype(bf16)
    lo = (rest - mid.astype(f32)).astype(bf16)
    return hi, mid, lo


def _dot_f32_by_01(x, m01):
    hi, mid, lo = _split3(x)
    dot = lambda part: jnp.dot(part, m01, preferred_element_type=f32)
    return dot(hi) + dot(mid) + dot(lo)


def _silu(x):
    return x * jax.nn.sigmoid(x)


def _softplus(x):
    return jnp.maximum(x, 0.0) + jnp.log1p(jnp.exp(-jnp.abs(x)))


def _rope_slab(x, cos, sin_signed):
    lane = lax.broadcasted_iota(jnp.int32, x.shape, 1)
    first_half = (lane % HEAD_DIM) < (HEAD_DIM // 2)
    partner = jnp.where(first_half, pltpu.roll(x, LANES - HEAD_DIM // 2, 1), pltpu.roll(x, HEAD_DIM // 2, 1))
    return x * cos + partner * sin_signed


def _pool_select(s2, s4, s8, s16):
    lane = lax.broadcasted_iota(jnp.int32, s2.shape, 1)
    return jnp.where(lane < POOL_GROUP, s2, jnp.where(lane < 2 * POOL_GROUP, s4, jnp.where(lane < 3 * POOL_GROUP, s8, s16)))


def _pool_window(shape):
    lane = lax.broadcasted_iota(jnp.int32, shape, 1)
    return jnp.where(lane < POOL_GROUP, POOL_WINDOWS[0],
                     jnp.where(lane < 2 * POOL_GROUP, POOL_WINDOWS[1],
                               jnp.where(lane < 3 * POOL_GROUP, POOL_WINDOWS[2], POOL_WINDOWS[3])))


def _ffn_kernel(has_mix, has_final, *refs):
    refs = list(refs)
    x_ref = refs.pop(0)
    if has_mix:
        ya_ref, yb_ref, yc_ref, wo_ref = refs[:4]
        refs = refs[4:]
    g_ref, wg_ref, wu_ref, wd_ref = refs[:4]
    refs = refs[4:]
    if has_final:
        fg_ref = refs.pop(0)
    o_ref = refs.pop(0)

    tm = x_ref.shape[0]
    n_sub = FFN_SPLIT if tm % (FFN_SPLIT * SUBLANES * 2) == 0 else 1
    subs = [slice(h * (tm // n_sub), (h + 1) * (tm // n_sub)) for h in range(n_sub)]
    xs = [x_ref[rows, :] for rows in subs]
    if has_mix:
        cats = [jnp.concatenate([ya_ref[rows, :], yb_ref[0, rows, :], yb_ref[1, rows, :], yc_ref[rows, :]], axis=1)
                for rows in subs]
        xs = [x + jnp.dot(cat, wo_ref[...], preferred_element_type=f32) for x, cat in zip(xs, cats)]
    xns = [_rms(x, g_ref[...]).astype(bf16) for x in xs]
    gates = [jnp.dot(xn, wg_ref[...], preferred_element_type=f32) for xn in xns]
    ups = [jnp.dot(xn, wu_ref[...], preferred_element_type=f32) for xn in xns]
    hs = [(_silu(gate) * up).astype(bf16) for gate, up in zip(gates, ups)]
    downs = [jnp.dot(h, wd_ref[...], preferred_element_type=f32) for h in hs]
    for rows, x, down in zip(subs, xs, downs):
        y = x + 0.5 * down
        if has_final:
            y = _rms(y, fg_ref[...])
        o_ref[rows, :] = y


def _ffn(x, layer, g, wg, wu, wd, mix=None, final_g=None):
    m = x.shape[0]
    tm = min(ROW_TILE, m)
    assert m % tm == 0
    row = lambda w: pl.BlockSpec((tm, w), lambda i: (i, 0))
    args, specs = [x], [row(D_MODEL)]
    if mix is not None:
        ya, yb, yc, wo = mix
        args += [ya, yb, yc, wo]
        specs += [row(D_POOL), pl.BlockSpec((2, tm, LANES), lambda i: (0, i, 0)), row(D_SSM), _layer_spec(wo, layer)]
    args += [g, wg, wu, wd]
    specs += [_layer_spec(g, layer), _layer_spec(wg, layer), _layer_spec(wu, layer), _layer_spec(wd, layer)]
    if final_g is not None:
        args.append(final_g)
        specs.append(_const_spec(final_g.shape))
    return pl.pallas_call(
        functools.partial(_ffn_kernel, mix is not None, final_g is not None),
        grid=(m // tm,),
        in_specs=specs,
        out_specs=row(D_MODEL),
        out_shape=jax.ShapeDtypeStruct((m, D_MODEL), f32),
        compiler_params=_params(1),
        name="ffn",
    )(*args)


def _inproj_ssd_kernel(tiles_per_seq, x_ref, g_ref, w_ref, pbd_ref, pscale_ref, cw_ref, cb_ref, dtb_ref, cos_ref, sin_ref,
                       cosq_ref, sinq_ref, a2_ref, e3_ref, dskip_ref, norm_ref,
                       ya_ref, q_ref, k_ref, v_ref, yc_ref, pst_ref, cst_ref, state_ref,
                       xa_buf, xbc_buf, u_buf, z_buf, dt_buf, st):
    tm = x_ref.shape[0]
    n_sub = INPROJ_SPLIT
    sub = tm // n_sub
    t = pl.program_id(0) % tiles_per_seq

    @pl.when(t == 0)
    def _():
        xa_buf[0:POOL_HALO, :] = jnp.zeros((POOL_HALO, D_POOL), f32)
        xbc_buf[0:CONV_HALO, :] = jnp.zeros((CONV_HALO, D_CONV), f32)
        st[...] = jnp.zeros_like(st)

    groups = (("xbc_dt", OFF_XBC, N_PROJ), ("xa", 0, OFF_Q), ("qkv", OFF_Q, OFF_Z), ("z", OFF_Z, OFF_XBC))
    projs = {}

    def project(h):
        xn = _rms(x_ref[h * sub:(h + 1) * sub, :], g_ref[...]).astype(bf16)
        projs[h] = {name: jnp.dot(xn, w_ref[:, a:b], preferred_element_type=f32) for name, a, b in groups}

    def tail(h):
        r0 = h * sub
        rows = slice(r0, r0 + sub)
        xbc, xa, qkv = projs[h]["xbc_dt"][:, 0:D_CONV], projs[h]["xa"], projs[h]["qkv"]

        xbc_buf[CONV_HALO + r0:CONV_HALO + r0 + sub, :] = xbc
        conv = cb_ref[...] + xbc * cw_ref[CONV_WIDTH - 1:CONV_WIDTH, :]
        for j in range(1, CONV_WIDTH):
            conv = conv + xbc_buf[CONV_HALO + r0 - j:CONV_HALO + r0 - j + sub, :] * cw_ref[CONV_WIDTH - 1 - j:CONV_WIDTH - j, :]
        u_buf[rows, :] = _silu(conv)
        dt_buf[rows, :] = _softplus(projs[h]["xbc_dt"][:, D_CONV:D_CONV + LANES] + dtb_ref[...])
        z_buf[rows, :] = projs[h]["z"]

        xa_buf[POOL_HALO + r0:POOL_HALO + r0 + sub, :] = xa
        ext = xa_buf[r0:r0 + sub + POOL_HALO, :]
        s2 = ext + pltpu.roll(ext, 1, 0)
        s4 = s2 + pltpu.roll(s2, 2, 0)
        s8 = s4 + pltpu.roll(s4, 4, 0)
        s16 = s8 + pltpu.roll(s8, 8, 0)
        sums = _pool_select(s2, s4, s8, s16)[POOL_HALO:POOL_HALO + sub, :]
        pos = t * tm + r0 + lax.broadcasted_iota(jnp.int32, (sub, D_POOL), 0)
        cnt = jnp.minimum(_pool_window((sub, D_POOL)), pos + 1).astype(f32)
        dmean = sums / cnt - xa
        ya = jnp.dot(dmean.astype(bf16), pbd_ref[...], preferred_element_type=f32) * pscale_ref[...]
        ya_ref[rows, :] = ya.astype(ya_ref.dtype)

        cos = cos_ref[rows, :]
        sin = sin_ref[rows, :]
        cosq = cosq_ref[rows, :]
        sinq = sinq_ref[rows, :]
        for s in range(2):
            q_ref[s, rows, :] = _rope_slab(qkv[:, s * LANES:(s + 1) * LANES], cosq, sinq)
            k_ref[s, rows, :] = _rope_slab(qkv[:, D_ATT + s * LANES:D_ATT + (s + 1) * LANES], cos, sin)
            v_ref[s, rows, :] = qkv[:, 2 * D_ATT + s * LANES:2 * D_ATT + (s + 1) * LANES]

    def carry_history():
        last_xbc = xbc_buf[tm:tm + CONV_HALO, :]
        cst_ref[0] = last_xbc
        xbc_buf[0:CONV_HALO, :] = last_xbc
        last_xa = xa_buf[tm:tm + POOL_HALO, :]
        pst_ref[0] = last_xa
        xa_buf[0:POOL_HALO, :] = last_xa

    q = SSM_CHUNK
    cps = sub // q
    gl = D_SSM // SSM_GROUPS
    hpg = N_SSM_HEADS // SSM_GROUPS
    li = lax.broadcasted_iota(jnp.int32, (q, q), 0)
    si = lax.broadcasted_iota(jnp.int32, (q, q), 1)
    causal = li >= si
    tril = causal.astype(bf16)
    triu = (li <= si).astype(bf16)
    lo = lax.broadcasted_iota(jnp.int32, (q, LANES), 1) < SSM_HEAD_DIM
    crows = lambda c: slice(c * q, (c + 1) * q)
    b_of = lambda u, g: u[:, D_SSM + g * SSM_STATE:D_SSM + (g + 1) * SSM_STATE]
    c_of = lambda u, g: u[:, D_SSM + (SSM_GROUPS + g) * SSM_STATE:D_SSM + (SSM_GROUPS + g + 1) * SSM_STATE]
    sgroups = range(SSM_GROUPS)
    gsl = lambda g: slice(g * gl, (g + 1) * gl)
    chunks_of = lambda h: range(h * cps, (h + 1) * cps)
    v = {}

    def level1(h):
        for c in chunks_of(h):
            u = u_buf[crows(c), :]
            dtp = dt_buf[crows(c), :]
            dta = dtp * a2_ref[...]
            v["dtp", c] = dtp
            v["cum3", c] = jnp.dot(tril, jnp.concatenate(_split3(dta), axis=1), preferred_element_type=f32)
            v["acs_row", c] = _dot_f32_by_01(dta.T[0:SUBLANES], triu)
            for g in sgroups:
                v["cb", c, g] = lax.dot_general(c_of(u, g).astype(bf16), b_of(u, g).astype(bf16), (((1,), (1,)), ((), ())),
                                                preferred_element_type=f32)

    def level2(h):
        for c in chunks_of(h):
            cum3 = v["cum3", c]
            cum = cum3[:, 0:LANES] + cum3[:, LANES:2 * LANES] + cum3[:, 2 * LANES:3 * LANES]
            v["both", c] = jnp.dot(jnp.concatenate(_split3(jnp.concatenate([v["dtp", c], cum], axis=0)), axis=1), e3_ref[...],
                                   preferred_element_type=f32)

    def level3(h):
        for c in chunks_of(h):
            u = u_buf[crows(c), :]
            dt_exp, acs_exp, acs_row = v["both", c][0:q], v["both", c][q:2 * q], v["acs_row", c]
            xdt = u[:, 0:D_SSM] * dt_exp
            xdt_b = xdt.astype(bf16)
            parts = []
            for g in sgroups:
                for pair in range(hpg // 2):
                    mats = []
                    for hh in range(2):
                        hd = g * hpg + pair * 2 + hh
                        col = acs_exp[:, hd * SSM_HEAD_DIM:hd * SSM_HEAD_DIM + 1]
                        diff = jnp.where(causal, col - acs_row[hd:hd + 1, :], -jnp.inf)
                        mats.append((v["cb", c, g] * jnp.exp2(diff)).astype(bf16))
                    c0 = (g * hpg + pair * 2) * SSM_HEAD_DIM
                    xp = xdt_b[:, c0:c0 + LANES]
                    rhs = jnp.concatenate([jnp.where(lo, xp, jnp.zeros_like(xp)), jnp.where(lo, jnp.zeros_like(xp), xp)],
                                          axis=0)
                    parts.append(jnp.dot(jnp.concatenate(mats, axis=1), rhs, preferred_element_type=f32))
            v["ydiag", c] = jnp.concatenate(parts, axis=1)
            acs_last = acs_exp[q - 1:q, :]
            xw = (xdt * jnp.exp2(acs_last - acs_exp)).astype(bf16)
            v["s_new", c] = jnp.concatenate(
                [jnp.dot(b_of(u, g).T.astype(bf16), xw[:, gsl(g)], preferred_element_type=f32) for g in sgroups], axis=1)
            v["acs", c] = acs_exp

    def level4():
        state = st[...]
        for c in range(n_sub * cps):
            u = u_buf[crows(c), :]
            state_b = state.astype(bf16)
            v["yoff", c] = jnp.concatenate(
                [jnp.dot(c_of(u, g).astype(bf16), state_b[:, gsl(g)], preferred_element_type=f32) for g in sgroups], axis=1)
            state = state * jnp.exp2(v["acs", c][q - 1:q, :]) + v["s_new", c]
        st[...] = state
        state_ref[0] = state

    def level5():
        for c in range(n_sub * cps):
            xs = u_buf[crows(c), 0:D_SSM]
            y = v["ydiag", c] + v["yoff", c] * jnp.exp2(v["acs", c])
            y = (y + dskip_ref[...] * xs) * _silu(z_buf[crows(c), :])
            outs = []
            for g in sgroups:
                yg = y[:, gsl(g)]
                outs.append(yg * lax.rsqrt(jnp.mean(yg * yg, -1, keepdims=True) + RMS_EPS))
            yc_ref[crows(c), :] = (jnp.concatenate(outs, axis=1) * norm_ref[...]).astype(yc_ref.dtype)

    project(0)
    project(1)
    for s in range(n_sub + 3):
        if s < n_sub:
            tail(s)
        if s + 2 < n_sub:
            project(s + 2)
        for level, lag in ((level1, 1), (level2, 2), (level3, 3)):
            if 0 <= s - lag < n_sub:
                level(s - lag)
    carry_history()
    level4()
    level5()


def _inproj_ssd_prompt(x, layer, batch, seq, g, w, pbd, pscale, cw, cb, dtb, cos, sin, cosq, sinq, a2_pad, e3_mat, dskip, norm):
    m = x.shape[0]
    tm = INPROJ_TILE
    tps = seq // tm
    row = lambda wd: pl.BlockSpec((tm, wd), lambda i: (i, 0))
    slab = pl.BlockSpec((2, tm, LANES), lambda i: (0, i, 0))
    tab = pl.BlockSpec((tm, LANES), lambda i: (i % tps, 0))
    per_seq = lambda r, c: pl.BlockSpec((1, r, c), lambda i: (i // tps, 0, 0))
    out_shape = (
        jax.ShapeDtypeStruct((m, D_POOL), bf16),
        jax.ShapeDtypeStruct((2, m, LANES), f32),
        jax.ShapeDtypeStruct((2, m, LANES), f32),
        jax.ShapeDtypeStruct((2, m, LANES), f32),
        jax.ShapeDtypeStruct((m, D_SSM), bf16),
        jax.ShapeDtypeStruct((batch, POOL_HALO, D_POOL), f32),
        jax.ShapeDtypeStruct((batch, CONV_HALO, D_CONV), f32),
        jax.ShapeDtypeStruct((batch, SSM_STATE, D_SSM), f32),
    )
    out_specs = (row(D_POOL), slab, slab, slab, row(D_SSM), per_seq(POOL_HALO, D_POOL), per_seq(CONV_HALO, D_CONV),
                 per_seq(SSM_STATE, D_SSM))
    consts = [pbd, pscale, cw, cb, dtb]
    ssd_consts = [a2_pad, e3_mat, dskip, norm]
    in_specs = ([row(D_MODEL), _layer_spec(g, layer), _layer_spec(w, layer)] + [_layer_spec(c, layer) for c in consts]
                + [tab, tab, tab, tab]
                + [_layer_spec(a2_pad, layer), _const_spec(e3_mat.shape), _layer_spec(dskip, layer), _layer_spec(norm, layer)])
    return pl.pallas_call(
        functools.partial(_inproj_ssd_kernel, tps),
        grid=(m // tm,),
        in_specs=in_specs,
        out_specs=out_specs,
        out_shape=out_shape,
        scratch_shapes=[pltpu.VMEM((POOL_HALO + tm, D_POOL), f32), pltpu.VMEM((CONV_HALO + tm, D_CONV), f32),
                        pltpu.VMEM((tm, D_CONV), f32), pltpu.VMEM((tm, D_SSM), f32), pltpu.VMEM((tm, LANES), f32),
                        pltpu.VMEM((SSM_STATE, D_SSM), f32)],
        compiler_params=_params(1),
        name="inproj_ssd",
    )(x, g, w, *consts, cos, sin, cosq, sinq, *ssd_consts)


def _attn_kernel(has_prev, q_ref, kp_ref, kc_ref, vp_ref, vc_ref, *rest):
    if has_prev:
        rest = rest[2:]
    o_ref, kwin_ref, vwin_ref, acc, mrun, lrun = rest
    t = pl.program_id(2)
    blk = N_BACK

    @pl.when(t == pl.num_programs(2) - 1)
    def _():
        kwin_ref[0] = kc_ref[0, 0].T
        vwin_ref[0] = vc_ref[0, 0].T

    qi = lax.broadcasted_iota(jnp.int32, (2 * blk, 2 * blk), 0) % blk
    kj = lax.broadcasted_iota(jnp.int32, (2 * blk, 2 * blk), 1)
    dist = blk + qi - kj
    band = (dist >= 0) & (dist <= N_BACK)
    bias_full = jnp.where(band, 0.0, -jnp.inf).astype(f32)
    bias_cur = jnp.where(band & (kj >= blk), 0.0, -jnp.inf).astype(f32)
    bias_first = jnp.where(t > 0, bias_full, bias_cur)
    head0 = lax.broadcasted_iota(jnp.int32, (blk, LANES), 1) < HEAD_DIM
    head0_k = lax.broadcasted_iota(jnp.int32, (2 * blk, LANES), 1) < HEAD_DIM
    ones_blk = jnp.concatenate([jnp.where(head0_k, 1.0, 0.0), jnp.where(head0_k, 0.0, 1.0)], axis=0).astype(bf16)
    n_units = ATT_TILE // blk

    def strided(ref, start, n, d):
        return ref[0, 0, pl.ds(start, n, stride=d) if d > 1 else pl.ds(start, n), :]

    def group(d, units):
        first = d == DILATIONS[0]
        last = d == DILATIONS[-1]
        span = blk * d
        rows, scores, vcats = [], [], []
        for u in units:
            sb, r = divmod(u, d)
            qstart = sb * span + r
            rows.append(pl.ds(qstart, blk, stride=d) if d > 1 else pl.ds(qstart, blk))
            qb = strided(q_ref, qstart, blk, d).astype(bf16)
            if sb > 0:
                kf = strided(kc_ref, qstart - span, 2 * blk, d)
                vf = strided(vc_ref, qstart - span, 2 * blk, d)
                bias = bias_full
            else:
                kf = jnp.concatenate([strided(kp_ref, ATT_TILE - span + r, blk, d), strided(kc_ref, r, blk, d)], axis=0)
                vf = jnp.concatenate([strided(vp_ref, ATT_TILE - span + r, blk, d), strided(vc_ref, r, blk, d)], axis=0)
                bias = bias_first
            kb = kf.astype(bf16)
            vb = vf.astype(bf16)
            qcat = jnp.concatenate([jnp.where(head0, qb, jnp.zeros_like(qb)), jnp.where(head0, jnp.zeros_like(qb), qb)], axis=0)
            scores.append(lax.dot_general(qcat, kb, (((1,), (1,)), ((), ())), preferred_element_type=f32) + bias)
            vcat = jnp.concatenate([jnp.where(head0_k, vb, jnp.zeros_like(vb)), jnp.where(head0_k, jnp.zeros_like(vb), vb)],
                                   axis=0)
            vcats.append(jnp.concatenate([vcat, ones_blk], axis=1))
        stats = []
        for sc in scores:
            m2 = jnp.max(sc, axis=1, keepdims=True)
            pb = jnp.exp2(sc - m2).astype(bf16)
            stats.append((jnp.concatenate([pb[0:blk], pb[blk:2 * blk]], axis=1), jnp.where(head0, m2[0:blk], m2[blk:2 * blk])))
        pvs = [jnp.dot(pcat, vcat, preferred_element_type=f32) for (pcat, _), vcat in zip(stats, vcats)]
        for qrows, pvl, (_, m_e) in zip(rows, pvs, stats):
            pv, l_e = pvl[:, 0:LANES], pvl[:, LANES:2 * LANES]
            if first:
                acc[qrows, :] = pv
                mrun[qrows, :] = m_e
                lrun[qrows, :] = l_e
            else:
                m_old = mrun[qrows, :]
                m_new = jnp.maximum(m_old, m_e)
                a = jnp.exp2(m_old - m_new)
                b = jnp.exp2(m_e - m_new)
                acc_new = acc[qrows, :] * a + pv * b
                l_new = lrun[qrows, :] * a + l_e * b
                if last:
                    o_ref[0, 0, qrows, :] = (acc_new / l_new).astype(o_ref.dtype)
                else:
                    acc[qrows, :] = acc_new
                    lrun[qrows, :] = l_new
                    mrun[qrows, :] = m_new

    for d in DILATIONS:
        for g0 in range(0, n_units, ATT_UNROLL):
            group(d, range(g0, g0 + ATT_UNROLL))


def _attn_prompt(layer, depth, q, k, v, batch, seq, prev):
    assert ATT_TILE == ATT_WIN
    nt = seq // ATT_TILE
    q4 = q.reshape(2, batch, seq, LANES)
    k4 = k.reshape(2, batch, seq, LANES)
    v4 = v.reshape(2, batch, seq, LANES)
    cur = pl.BlockSpec((1, 1, ATT_TILE, LANES), lambda b, s, t: (s, b, t, 0))
    before = pl.BlockSpec((1, 1, ATT_TILE, LANES), lambda b, s, t: (s, b, jnp.maximum(t - 1, 0), 0))
    win = pl.BlockSpec((None, 1, LANES, ATT_WIN), lambda b, s, t: (layer, b, s, 0))
    win_shape = jax.ShapeDtypeStruct((depth, batch, D_ATT, ATT_WIN), f32)
    args, in_specs, aliases = [q4, k4, k4, v4, v4], [cur, before, cur, before, cur], {}
    if prev is not None:
        in_specs += [pl.BlockSpec(memory_space=pl.ANY)] * 2
        aliases = {5: 1, 6: 2}
        args += list(prev)
    out, kwin, vwin = pl.pallas_call(
        functools.partial(_attn_kernel, prev is not None),
        grid=(batch, 2, nt),
        in_specs=in_specs,
        out_specs=(cur, win, win),
        out_shape=(jax.ShapeDtypeStruct((2, batch, seq, LANES), bf16), win_shape, win_shape),
        scratch_shapes=[pltpu.VMEM((ATT_TILE, LANES), f32), pltpu.VMEM((ATT_TILE, LANES), f32),
                        pltpu.VMEM((ATT_TILE, LANES), f32)],
        input_output_aliases=aliases,
        compiler_params=_params(3),
        name="attn",
    )(*args)
    return out.reshape(2, batch * seq, LANES), (kwin, vwin)


def _sample_pre_kernel(x_ref, g_ref, w_ref, cpool_ref, pbd_ref, pscale_ref, cconv_ref, cw_ref, cb_ref, dtb_ref,
                       cos_ref, sin_ref, ya_ref, q_ref, k_ref, v_ref, z_ref, u_ref, dt_ref, pnew_ref, cnew_ref):
    xn = _rms(x_ref[...], g_ref[...]).astype(bf16)
    proj = jnp.dot(xn, w_ref[...], preferred_element_type=f32)

    xa = proj[:, 0:D_POOL]
    back = lambda i: cpool_ref[POOL_BUF - i]
    s2 = xa + back(1)
    s4 = s2 + back(2) + back(3)
    s8 = s4 + back(4) + back(5) + back(6) + back(7)
    s16 = s8
    for i in range(8, 16):
        s16 = s16 + back(i)
    cnt = jnp.minimum(_pool_window(xa.shape), PAST_LEN + 1).astype(f32)
    dmean = _pool_select(s2, s4, s8, s16) / cnt - xa
    ya = jnp.dot(dmean.astype(bf16), pbd_ref[...], preferred_element_type=f32) * pscale_ref[...]
    ya_ref[...] = ya.astype(ya_ref.dtype)
    for i in range(POOL_BUF - 1):
        pnew_ref[i] = cpool_ref[i + 1]
    pnew_ref[POOL_BUF - 1] = xa

    cos = cos_ref[...]
    sin = sin_ref[...]
    for s in range(2):
        sl = slice(s * LANES, (s + 1) * LANES)
        q_ref[:, sl] = _rope_slab(proj[:, OFF_Q + s * LANES:OFF_Q + (s + 1) * LANES], cos, sin)
        k_ref[:, sl] = _rope_slab(proj[:, OFF_K + s * LANES:OFF_K + (s + 1) * LANES], cos, sin)
    v_ref[...] = proj[:, OFF_V:OFF_V + D_ATT]
    z_ref[...] = proj[:, OFF_Z:OFF_Z + D_SSM]

    xbc = proj[:, OFF_XBC:OFF_XBC + D_CONV]
    conv = cb_ref[...] + xbc * cw_ref[CONV_WIDTH - 1:CONV_WIDTH, :]
    for j in range(1, CONV_WIDTH):
        conv = conv + cconv_ref[CONV_WIDTH - 1 - j] * cw_ref[CONV_WIDTH - 1 - j:CONV_WIDTH - j, :]
    u_ref[...] = _silu(conv)
    for j in range(CONV_WIDTH - 2):
        cnew_ref[j] = cconv_ref[j + 1]
    cnew_ref[CONV_WIDTH - 2] = xbc

    dt_ref[...] = _softplus(proj[:, OFF_DT:OFF_DT + LANES] + dtb_ref[...])


def _sample_pre(x, layer, g, w, cpool, pbd, pscale, cconv, cw, cb, dtb, cos, sin):
    n = x.shape[0]
    args = [x, g, w, cpool, pbd, pscale, cconv, cw, cb, dtb, cos, sin]
    out_shape = (
        jax.ShapeDtypeStruct((n, D_POOL), bf16),
        jax.ShapeDtypeStruct((n, D_ATT), f32),
        jax.ShapeDtypeStruct((n, D_ATT), f32),
        jax.ShapeDtypeStruct((n, D_ATT), f32),
        jax.ShapeDtypeStruct((n, D_SSM), f32),
        jax.ShapeDtypeStruct((n, D_CONV), f32),
        jax.ShapeDtypeStruct((n, LANES), f32),
        jax.ShapeDtypeStruct((POOL_BUF, n, D_POOL), f32),
        jax.ShapeDtypeStruct((CONV_WIDTH - 1, n, D_CONV), f32),
    )
    full = lambda s: pl.BlockSpec(s.shape, lambda i, nd=len(s.shape): (0,) * nd)
    in_specs = [full(x)] + [_layer_spec(a, layer) for a in args[1:-2]] + [full(cos), full(sin)]
    return pl.pallas_call(
        _sample_pre_kernel,
        grid=(1,),
        in_specs=in_specs,
        out_specs=tuple(full(s) for s in out_shape),
        out_shape=out_shape,
        compiler_params=_params(1),
        name="sample_pre",
    )(*args)


def _sample_ssd_kernel(u_ref, dt_ref, z_ref, st_ref, aexp_ref, e_ref, dskip_ref, norm_ref, *rest):
    y_ref, stnew_ref = rest[-2:]
    n = u_ref.shape[0]
    gl = D_SSM // SSM_GROUPS
    u = u_ref[...]
    xs = u[:, 0:D_SSM]
    dt_exp = _dot_f32_by_01(dt_ref[...], e_ref[...])
    pad = jnp.zeros((LANES - n, D_SSM), f32)
    dec_t = jnp.concatenate([jnp.exp(dt_exp * aexp_ref[...]), pad], axis=0).T
    dtx_t = jnp.concatenate([dt_exp * xs, pad], axis=0).T
    lane = lax.broadcasted_iota(jnp.int32, (D_SSM, LANES), 1)
    y_t = jnp.zeros((D_SSM, LANES), f32)
    for i in range(n):
        dec = dec_t[:, i:i + 1]
        dtx = dtx_t[:, i:i + 1]
        ycols = []
        for g in range(SSM_GROUPS):
            rows = slice(g * gl, (g + 1) * gl)
            b_row = u[i:i + 1, D_SSM + g * SSM_STATE:D_SSM + (g + 1) * SSM_STATE]
            c_row = u[i:i + 1, D_SSM + (SSM_GROUPS + g) * SSM_STATE:D_SSM + (SSM_GROUPS + g + 1) * SSM_STATE]
            h_new = dec[rows] * st_ref[i, rows, :] + dtx[rows] * b_row
            stnew_ref[i, rows, :] = h_new
            ycols.append(jnp.sum(h_new * c_row, axis=1, keepdims=True))
        y_t = jnp.where(lane == i, jnp.concatenate(ycols, axis=0), y_t)
    y = y_t.T[0:n]
    y = (y + dskip_ref[...] * xs) * _silu(z_ref[...])
    outs = []
    for g in range(SSM_GROUPS):
        yg = y[:, g * gl:(g + 1) * gl]
        outs.append(yg * lax.rsqrt(jnp.mean(yg * yg, -1, keepdims=True) + RMS_EPS))
    y_ref[...] = (jnp.concatenate(outs, axis=1) * norm_ref[...]).astype(y_ref.dtype)


def _sample_ssd(layer, u, dt, z, state, prev, a_exp, e_mat, dskip, norm):
    n = u.shape[0]
    full = lambda s: pl.BlockSpec(s.shape, lambda i, nd=len(s.shape): (0,) * nd)
    of_layer = pl.BlockSpec((None,) + state.shape[1:], lambda i: (layer, 0, 0, 0))
    args = [u, dt, z, state, a_exp, e_mat, dskip, norm]
    in_specs = [full(u), full(dt), full(z), of_layer, _layer_spec(a_exp, layer), full(e_mat), _layer_spec(dskip, layer),
                _layer_spec(norm, layer)]
    aliases = {}
    if prev is not None:
        in_specs.append(pl.BlockSpec(memory_space=pl.ANY))
        aliases = {len(args): 1}
        args.append(prev)
    y_shape = jax.ShapeDtypeStruct((n, D_SSM), bf16)
    return pl.pallas_call(
        _sample_ssd_kernel,
        grid=(1,),
        in_specs=in_specs,
        out_specs=(full(y_shape), of_layer),
        out_shape=(y_shape, jax.ShapeDtypeStruct(state.shape, f32)),
        input_output_aliases=aliases,
        compiler_params=_params(1),
        name="sample_ssd",
    )(*args)


def _sample_attn_kernel(has_prev, q_ref, kn_ref, vn_ref, knc_ref, vnc_ref, kc_ref, vc_ref, *rest):
    if has_prev:
        rest = rest[2:]
    y_ref, ks_ref, vs_ref = rest
    wb = kc_ref.shape[3]
    kc = kc_ref[0, 0]
    vc = vc_ref[0, 0]
    q = q_ref[0]
    kn = kn_ref[0]
    vn = vn_ref[0]
    scale = 1.0 / math.sqrt(HEAD_DIM)

    own = (lax.broadcasted_iota(jnp.int32, (SUBLANES, D_ATT), 1) // HEAD_DIM) == lax.broadcasted_iota(
        jnp.int32, (SUBLANES, D_ATT), 0)
    qbd = jnp.where(own, jnp.broadcast_to(q, (SUBLANES, D_ATT)), 0.0)
    s_all = jnp.dot(qbd.astype(bf16), kc.astype(bf16), preferred_element_type=f32) * scale
    s_new = jnp.sum(jnp.where(own, jnp.broadcast_to(q * kn, (SUBLANES, D_ATT)), 0.0), axis=1, keepdims=True) * scale
    dist = wb - lax.broadcasted_iota(jnp.int32, (SUBLANES, wb), 1)
    ms, ps, ls, es = [], [], [], []
    for d in DILATIONS:
        valid = ((dist % d) == 0) & (dist <= N_BACK * d)
        s = jnp.where(valid, s_all, -jnp.inf)
        m = jnp.maximum(jnp.max(s, axis=1, keepdims=True), s_new)
        p = jnp.exp(s - m)
        e_new = jnp.exp(s_new - m)
        ms.append(m)
        ps.append(p)
        es.append(e_new)
        ls.append(jnp.sum(p, axis=1, keepdims=True) + e_new)
    m_all = jnp.maximum(jnp.maximum(ms[0], ms[1]), ms[2])
    p_all = jnp.zeros_like(ps[0])
    w_new = jnp.zeros_like(s_new)
    l_all = jnp.zeros_like(s_new)
    for m, p, e_new, l in zip(ms, ps, es, ls):
        c = jnp.exp(m - m_all)
        p_all = p_all + c * p
        w_new = w_new + c * e_new
        l_all = l_all + c * l
    o_full = lax.dot_general(p_all.astype(bf16), vc.astype(bf16), (((1,), (1,)), ((), ())), preferred_element_type=f32)
    to_row = lambda t: jnp.sum(jnp.where(own, t, 0.0), axis=0, keepdims=True)
    o_row = to_row(o_full) + to_row(jnp.broadcast_to(w_new, (SUBLANES, D_ATT))) * vn
    y_ref[0] = o_row / to_row(jnp.broadcast_to(l_all, (SUBLANES, D_ATT)))

    newest = lax.broadcasted_iota(jnp.int32, (D_ATT, wb), 1) == wb - 1
    ks_ref[0, 0] = jnp.where(newest, knc_ref[0], pltpu.roll(kc, wb - 1, 1))
    vs_ref[0, 0] = jnp.where(newest, vnc_ref[0], pltpu.roll(vc, wb - 1, 1))


def _sample_attn(layer, q, kn, vn, cache_k, cache_v, prev):
    depth, n, da, wb = cache_k.shape
    rowv = pl.BlockSpec((1, 1, da), lambda i: (i, 0, 0))
    colv = pl.BlockSpec((1, da, 1), lambda i: (i, 0, 0))
    big = pl.BlockSpec((1, 1, da, wb), lambda i: (layer, i, 0, 0))
    in_specs = [rowv, rowv, rowv, colv, colv, big, big]
    args = [q.reshape(n, 1, da), kn.reshape(n, 1, da), vn.reshape(n, 1, da), kn.reshape(n, da, 1), vn.reshape(n, da, 1),
            cache_k, cache_v]
    aliases = {}
    if prev is not None:
        in_specs += [pl.BlockSpec(memory_space=pl.ANY)] * 2
        aliases = {len(args): 1, len(args) + 1: 2}
        args += list(prev)
    return pl.pallas_call(
        functools.partial(_sample_attn_kernel, prev is not None),
        grid=(n,),
        in_specs=in_specs,
        out_specs=(rowv, big, big),
        out_shape=(jax.ShapeDtypeStruct((n, 1, da), f32), jax.ShapeDtypeStruct(cache_k.shape, f32),
                   jax.ShapeDtypeStruct(cache_v.shape, f32)),
        input_output_aliases=aliases,
        compiler_params=_params(1),
        name="sample_attn",
    )(*args)


def _rope_tables(pos):
    half = HEAD_DIM // 2
    inv = ROPE_THETA ** (-jnp.arange(half, dtype=f32) / half)
    ang = pos.astype(f32)[:, None] * inv[None]
    cos = jnp.tile(jnp.cos(ang), (1, LANES // half))
    sin = jnp.sin(ang)
    sin_signed = jnp.tile(jnp.concatenate([-sin, sin], axis=1), (1, LANES // HEAD_DIM))
    return cos, sin_signed


def kernel(x_prompt, x_sample, cache_pool, cache_k, cache_v, state_conv, state_ssm, ffn1_norm, ffn1_w_gate, ffn1_w_up,
           ffn1_w_down, mix_norm, w_in, pool_w, pool_scale, conv_w, conv_b, dt_bias, a_log, d_skip, ssm_norm, w_out,
           ffn2_norm, ffn2_w_gate, ffn2_w_up, ffn2_w_down, final_norm):
    batch, seq, _ = x_prompt.shape
    n_dec = x_sample.shape[0]
    depth = w_in.shape[0]
    wb = cache_k.shape[2]
    assert all(seq % tile == 0 for tile in (ATT_TILE, ROW_TILE, INPROJ_TILE))
    assert x_sample.shape[1] == 1 and wb == ATT_WIN

    yp = x_prompt.reshape(batch * seq, D_MODEL)
    ys = x_sample.reshape(n_dec, D_MODEL)
    cos_p, sin_p = _rope_tables(jnp.arange(seq))
    cos_s, sin_s = _rope_tables(jnp.full((1,), PAST_LEN))
    e_mat = (jnp.arange(LANES)[:, None] == (jnp.arange(D_SSM)[None, :] // SSM_HEAD_DIM)).astype(bf16)
    e3_mat = jnp.tile(e_mat, (3, 1))
    ck_t = jnp.transpose(cache_k, (0, 1, 3, 4, 2)).reshape(depth, n_dec, D_ATT, wb)
    cv_t = jnp.transpose(cache_v, (0, 1, 3, 4, 2)).reshape(depth, n_dec, D_ATT, wb)
    row2 = lambda a: a.reshape(1, -1)
    row3 = lambda a: a.reshape(depth, 1, -1)

    f1 = (row3(ffn1_norm), ffn1_w_gate.astype(bf16), ffn1_w_up.astype(bf16), ffn1_w_down.astype(bf16))
    f2 = (row3(ffn2_norm), ffn2_w_gate.astype(bf16), ffn2_w_up.astype(bf16), ffn2_w_down.astype(bf16))
    w_all = jnp.pad(w_in, ((0, 0), (0, 0), (0, N_PROJ - D_IN_PROJ))).astype(bf16)
    wo_all = w_out.astype(bf16)
    g_mix = row3(mix_norm)
    n_pool = len(POOL_WINDOWS)
    pbd = (pool_w[:, :, :, None, :] * jnp.eye(n_pool, dtype=f32)[None, :, None, :, None]).reshape(
        depth, D_POOL, D_POOL).astype(bf16)
    pscale = row3(pool_scale)
    cb = row3(conv_b)
    head_pad = ((0, 0), (0, LANES - N_SSM_HEADS))
    dtb = row3(jnp.pad(dt_bias, head_pad))
    a_neg = -jnp.exp(a_log.astype(f32))
    a2_pad = row3(jnp.pad(a_neg * math.log2(math.e), head_pad))
    a_exp = row3(jnp.repeat(a_neg, SSM_HEAD_DIM, axis=1))
    dskip = row3(jnp.repeat(d_skip, SSM_HEAD_DIM, axis=1))
    norm = row3(ssm_norm)
    cpool = jnp.transpose(cache_pool, (0, 2, 1, 3))
    cconv = jnp.transpose(state_conv, (0, 2, 1, 3))
    cosq_p, sinq_p = cos_p * ATT_SCALE, sin_p * ATT_SCALE

    st_p = [[] for _ in range(3)]
    st_s = [[] for _ in range(2)]
    kv_p = kv_s = ssm_s = None
    for i in range(depth):
        last = i == depth - 1
        fin = row2(final_norm) if last else None

        yp = _ffn(yp, i, *f1)
        ya, q, k, v, yc, pst, cst, sst = _inproj_ssd_prompt(yp, i, batch, seq, g_mix, w_all, pbd, pscale, conv_w, cb, dtb,
                                                            cos_p, sin_p, cosq_p, sinq_p, a2_pad, e3_mat, dskip, norm)
        yb, kv_p = _attn_prompt(i, depth, q, k, v, batch, seq, kv_p)
        yp = _ffn(yp, i, *f2, mix=(ya, yb, yc, wo_all), final_g=fin)

        st_p[0].append(pst[:, POOL_HALO - POOL_BUF:])
        st_p[1].append(cst[:, CONV_HALO - (CONV_WIDTH - 1):])
        st_p[2].append(jnp.transpose(sst.reshape(batch, SSM_STATE, N_SSM_HEADS, SSM_HEAD_DIM), (0, 2, 3, 1)))

        ys = _ffn(ys, i, *f1)
        ya_s, q_s, k_s, v_s, z_s, u_s, dt_s, pnew, cnew = _sample_pre(ys, i, g_mix, w_all, cpool, pbd, pscale, cconv,
                                                                    conv_w, cb, dtb, cos_s, sin_s)
        yb_s, ks_buf, vs_buf = _sample_attn(i, q_s, k_s, v_s, ck_t, cv_t, kv_s)
        kv_s = (ks_buf, vs_buf)
        yc_s, ssm_s = _sample_ssd(i, u_s, dt_s, z_s, state_ssm.reshape(depth, n_dec, D_SSM, SSM_STATE), ssm_s, a_exp, e_mat,
                                  dskip, norm)
        yb_slab = jnp.transpose(yb_s.reshape(n_dec, 2, LANES), (1, 0, 2)).astype(bf16)
        ys = _ffn(ys, i, *f2, mix=(ya_s, yb_slab, yc_s, wo_all), final_g=fin)
        st_s[0].append(jnp.transpose(pnew, (1, 0, 2)))
        st_s[1].append(jnp.transpose(cnew, (1, 0, 2)))

    y_prompt = yp.reshape(batch, seq, D_MODEL)
    y_sample = ys.reshape(n_dec, 1, D_MODEL)
    pool_p, conv_p, ssm_p = (jnp.stack(s, 0) for s in st_p)
    pool_s, conv_s = (jnp.stack(s, 0) for s in st_s)
    ssm_s = ssm_s.reshape(state_ssm.shape)
    from_t = lambda t: jnp.transpose(t.reshape(depth, -1, N_ATT_HEADS, HEAD_DIM, wb), (0, 1, 4, 2, 3))
    k_p, v_p = from_t(kv_p[0]), from_t(kv_p[1])
    return (y_prompt, y_sample, pool_p, pool_s, k_p, from_t(kv_s[0]), v_p, from_t(kv_s[1]), conv_p, conv_s, ssm_p, ssm_s)
```

```python
import functools
import math

import jax
import jax.numpy as jnp
from jax import lax
from jax.experimental import pallas as pl
from jax.experimental.pallas import tpu as pltpu

f32 = jnp.float32
bf16 = jnp.bfloat16

D_MODEL = 1024
PAST_LEN = 16384
POOL_WINDOWS = (2, 4, 8, 16)
D_POOL = 256
POOL_GROUP = D_POOL // len(POOL_WINDOWS)
POOL_BUF = max(POOL_WINDOWS) - 1
HEAD_DIM = 64
N_ATT_HEADS = 4
D_ATT = N_ATT_HEADS * HEAD_DIM
DILATIONS = (16, 4, 1)
N_BACK = 128
ATT_WIN = 2048
ROPE_THETA = 10000.0
ATT_SCALE = math.log2(math.e) / math.sqrt(HEAD_DIM)
D_SSM = 512
SSM_HEAD_DIM = 64
N_SSM_HEADS = 8
SSM_STATE = 128
SSM_GROUPS = 2
CONV_WIDTH = 4
SSM_CHUNK = 128
D_CONV = D_SSM + 2 * SSM_GROUPS * SSM_STATE
D_IN_PROJ = D_POOL + 3 * D_ATT + D_SSM + D_CONV + N_SSM_HEADS
D_FF = 2816
RMS_EPS = 1e-6

LANES = 128
SUBLANES = 8
VMEM_LIMIT = 56 * 1024 * 1024

OFF_Q = D_POOL
OFF_K = OFF_Q + D_ATT
OFF_V = OFF_K + D_ATT
OFF_Z = OFF_V + D_ATT
OFF_XBC = OFF_Z + D_SSM
OFF_DT = OFF_XBC + D_CONV
N_PROJ = OFF_DT + LANES

ROW_TILE = 512
FFN_SPLIT = 2
ATT_TILE = 4096
ATT_UNROLL = 4
INPROJ_TILE = 1024
INPROJ_SPLIT = 2
POOL_HALO = 16
CONV_HALO = 8


def _params(n_axes):
    return pltpu.CompilerParams(dimension_semantics=("arbitrary",) * n_axes, vmem_limit_bytes=VMEM_LIMIT)


def _const_spec(shape):
    nd = len(shape)
    return pl.BlockSpec(shape, lambda *_: (0,) * nd, pipeline_mode=pl.Buffered(1))


def _layer_spec(arr, layer):
    nd = arr.ndim - 1
    return pl.BlockSpec((None,) + arr.shape[1:], lambda *_: (layer,) + (0,) * nd, pipeline_mode=pl.Buffered(1))


def _rms(x, g):
    return x * lax.rsqrt(jnp.mean(x * x, -1, keepdims=True) + RMS_EPS) * g


def _split3(x):
    hi = x.astype(bf16)
    rest = x - hi.astype(f32)
    mid = rest.astype(bf16)
    lo = (rest - mid.astype(f32)).astype(bf16)
    return hi, mid, lo


def _dot_f32_by_01(x, m01):
    hi, mid, lo = _split3(x)
    dot = lambda part: jnp.dot(part, m01, preferred_element_type=f32)
    return dot(hi) + dot(mid) + dot(lo)


def _silu(x):
    return x * jax.nn.sigmoid(x)


def _softplus(x):
    return jnp.maximum(x, 0.0) + jnp.log1p(jnp.exp(-jnp.abs(x)))


def _rope_slab(x, cos, sin_signed):
    lane = lax.broadcasted_iota(jnp.int32, x.shape, 1)
    first_half = (lane % HEAD_DIM) < (HEAD_DIM // 2)
    partner = jnp.where(first_half, pltpu.roll(x, LANES - HEAD_DIM // 2, 1), pltpu.roll(x, HEAD_DIM // 2, 1))
    return x * cos + partner * sin_signed


def _pool_select(s2, s4, s8, s16):
    lane = lax.broadcasted_iota(jnp.int32, s2.shape, 1)
    return jnp.where(lane < POOL_GROUP, s2, jnp.where(lane < 2 * POOL_GROUP, s4, jnp.where(lane < 3 * POOL_GROUP, s8, s16)))


def _pool_window(shape):
    lane = lax.broadcasted_iota(jnp.int32, shape, 1)
    return jnp.where(lane < POOL_GROUP, POOL_WINDOWS[0],
                     jnp.where(lane < 2 * POOL_GROUP, POOL_WINDOWS[1],
                               jnp.where(lane < 3 * POOL_GROUP, POOL_WINDOWS[2], POOL_WINDOWS[3])))


def _ffn_kernel(has_mix, has_final, *refs):
    refs = list(refs)
    x_ref = refs.pop(0)
    if has_mix:
        ya_ref, yb_ref, yc_ref, wo_ref = refs[:4]
        refs = refs[4:]
    g_ref, wg_ref, wu_ref, wd_ref = refs[:4]
    refs = refs[4:]
    if has_final:
        fg_ref = refs.pop(0)
    o_ref = refs.pop(0)

    tm = x_ref.shape[0]
    n_sub = FFN_SPLIT if tm % (FFN_SPLIT * SUBLANES * 2) == 0 else 1
    subs = [slice(h * (tm // n_sub), (h + 1) * (tm // n_sub)) for h in range(n_sub)]
    xs = [x_ref[rows, :] for rows in subs]
    if has_mix:
        cats = [jnp.concatenate([ya_ref[rows, :], yb_ref[0, rows, :], yb_ref[1, rows, :], yc_ref[rows, :]], axis=1)
                for rows in subs]
        xs = [x + jnp.dot(cat, wo_ref[...], preferred_element_type=f32) for x, cat in zip(xs, cats)]
    xns = [_rms(x, g_ref[...]).astype(bf16) for x in xs]
    gates = [jnp.dot(xn, wg_ref[...], preferred_element_type=f32) for xn in xns]
    ups = [jnp.dot(xn, wu_ref[...], preferred_element_type=f32) for xn in xns]
    hs = [(_silu(gate) * up).astype(bf16) for gate, up in zip(gates, ups)]
    downs = [jnp.dot(h, wd_ref[...], preferred_element_type=f32) for h in hs]
    for rows, x, down in zip(subs, xs, downs):
        y = x + 0.5 * down
        if has_final:
            y = _rms(y, fg_ref[...])
        o_ref[rows, :] = y


def _ffn(x, layer, g, wg, wu, wd, mix=None, final_g=None):
    m = x.shape[0]
    tm = min(ROW_TILE, m)
    assert m % tm == 0
    row = lambda w: pl.BlockSpec((tm, w), lambda i: (i, 0))
    args, specs = [x], [row(D_MODEL)]
    if mix is not None:
        ya, yb, yc, wo = mix
        args += [ya, yb, yc, wo]
        specs += [row(D_POOL), pl.BlockSpec((2, tm, LANES), lambda i: (0, i, 0)), row(D_SSM), _layer_spec(wo, layer)]
    args += [g, wg, wu, wd]
    specs += [_layer_spec(g, layer), _layer_spec(wg, layer), _layer_spec(wu, layer), _layer_spec(wd, layer)]
    if final_g is not None:
        args.append(final_g)
        specs.append(_const_spec(final_g.shape))
    return pl.pallas_call(
        functools.partial(_ffn_kernel, mix is not None, final_g is not None),
        grid=(m // tm,),
        in_specs=specs,
        out_specs=row(D_MODEL),
        out_shape=jax.ShapeDtypeStruct((m, D_MODEL), f32),
        compiler_params=_params(1),
        name="ffn",
    )(*args)


def _inproj_ssd_kernel(tiles_per_seq, x_ref, g_ref, w_ref, pbd_ref, pscale_ref, cw_ref, cb_ref, dtb_ref, cos_ref, sin_ref,
                       cosq_ref, sinq_ref, a2_ref, e3_ref, dskip_ref, norm_ref,
                       ya_ref, q_ref, k_ref, v_ref, yc_ref, pst_ref, cst_ref, state_ref,
                       xa_buf, xbc_buf, u_buf, z_buf, dt_buf, st):
    tm = x_ref.shape[0]
    n_sub = INPROJ_SPLIT
    sub = tm // n_sub
    t = pl.program_id(0) % tiles_per_seq

    @pl.when(t == 0)
    def _():
        xa_buf[0:POOL_HALO, :] = jnp.zeros((POOL_HALO, D_POOL), f32)
        xbc_buf[0:CONV_HALO, :] = jnp.zeros((CONV_HALO, D_CONV), f32)
        st[...] = jnp.zeros_like(st)

    groups = (("xbc_dt", OFF_XBC, N_PROJ), ("xa", 0, OFF_Q), ("qkv", OFF_Q, OFF_Z), ("z", OFF_Z, OFF_XBC))
    projs = {}

    def project(h):
        xn = _rms(x_ref[h * sub:(h + 1) * sub, :], g_ref[...]).astype(bf16)
        projs[h] = {name: jnp.dot(xn, w_ref[:, a:b], preferred_element_type=f32) for name, a, b in groups}

    def tail(h):
        r0 = h * sub
        rows = slice(r0, r0 + sub)
        xbc, xa, qkv = projs[h]["xbc_dt"][:, 0:D_CONV], projs[h]["xa"], projs[h]["qkv"]

        xbc_buf[CONV_HALO + r0:CONV_HALO + r0 + sub, :] = xbc
        conv = cb_ref[...] + xbc * cw_ref[CONV_WIDTH - 1:CONV_WIDTH, :]
        for j in range(1, CONV_WIDTH):
            conv = conv + xbc_buf[CONV_HALO + r0 - j:CONV_HALO + r0 - j + sub, :] * cw_ref[CONV_WIDTH - 1 - j:CONV_WIDTH - j, :]
        u_buf[rows, :] = _silu(conv)
        dt_buf[rows, :] = _softplus(projs[h]["xbc_dt"][:, D_CONV:D_CONV + LANES] + dtb_ref[...])
        z_buf[rows, :] = projs[h]["z"]

        xa_buf[POOL_HALO + r0:POOL_HALO + r0 + sub, :] = xa
        ext = xa_buf[r0:r0 + sub + POOL_HALO, :]
        s2 = ext + pltpu.roll(ext, 1, 0)
        s4 = s2 + pltpu.roll(s2, 2, 0)
        s8 = s4 + pltpu.roll(s4, 4, 0)
        s16 = s8 + pltpu.roll(s8, 8, 0)
        sums = _pool_select(s2, s4, s8, s16)[POOL_HALO:POOL_HALO + sub, :]
        pos = t * tm + r0 + lax.broadcasted_iota(jnp.int32, (sub, D_POOL), 0)
        cnt = jnp.minimum(_pool_window((sub, D_POOL)), pos + 1).astype(f32)
        dmean = sums / cnt - xa
        ya = jnp.dot(dmean.astype(bf16), pbd_ref[...], preferred_element_type=f32) * pscale_ref[...]
        ya_ref[rows, :] = ya.astype(ya_ref.dtype)

        cos = cos_ref[rows, :]
        sin = sin_ref[rows, :]
        cosq = cosq_ref[rows, :]
        sinq = sinq_ref[rows, :]
        for s in range(2):
            q_ref[s, rows, :] = _rope_slab(qkv[:, s * LANES:(s + 1) * LANES], cosq, sinq)
            k_ref[s, rows, :] = _rope_slab(qkv[:, D_ATT + s * LANES:D_ATT + (s + 1) * LANES], cos, sin)
            v_ref[s, rows, :] = qkv[:, 2 * D_ATT + s * LANES:2 * D_ATT + (s + 1) * LANES]

    def carry_history():
        last_xbc = xbc_buf[tm:tm + CONV_HALO, :]
        cst_ref[0] = last_xbc
        xbc_buf[0:CONV_HALO, :] = last_xbc
        last_xa = xa_buf[tm:tm + POOL_HALO, :]
        pst_ref[0] = last_xa
        xa_buf[0:POOL_HALO, :] = last_xa

    q = SSM_CHUNK
    cps = sub // q
    gl = D_SSM // SSM_GROUPS
    hpg = N_SSM_HEADS // SSM_GROUPS
    li = lax.broadcasted_iota(jnp.int32, (q, q), 0)
    si = lax.broadcasted_iota(jnp.int32, (q, q), 1)
    causal = li >= si
    tril = causal.astype(bf16)
    triu = (li <= si).astype(bf16)
    lo = lax.broadcasted_iota(jnp.int32, (q, LANES), 1) < SSM_HEAD_DIM
    crows = lambda c: slice(c * q, (c + 1) * q)
    b_of = lambda u, g: u[:, D_SSM + g * SSM_STATE:D_SSM + (g + 1) * SSM_STATE]
    c_of = lambda u, g: u[:, D_SSM + (SSM_GROUPS + g) * SSM_STATE:D_SSM + (SSM_GROUPS + g + 1) * SSM_STATE]
    sgroups = range(SSM_GROUPS)
    gsl = lambda g: slice(g * gl, (g + 1) * gl)
    chunks_of = lambda h: range(h * cps, (h + 1) * cps)
    v = {}

    def level1(h):
        for c in chunks_of(h):
            u = u_buf[crows(c), :]
            dtp = dt_buf[crows(c), :]
            dta = dtp * a2_ref[...]
            v["dtp", c] = dtp
            v["cum3", c] = jnp.dot(tril, jnp.concatenate(_split3(dta), axis=1), preferred_element_type=f32)
            v["acs_row", c] = _dot_f32_by_01(dta.T[0:SUBLANES], triu)
            for g in sgroups:
                v["cb", c, g] = lax.dot_general(c_of(u, g).astype(bf16), b_of(u, g).astype(bf16), (((1,), (1,)), ((), ())),
                                                preferred_element_type=f32)

    def level2(h):
        for c in chunks_of(h):
            cum3 = v["cum3", c]
            cum = cum3[:, 0:LANES] + cum3[:, LANES:2 * LANES] + cum3[:, 2 * LANES:3 * LANES]
            v["both", c] = jnp.dot(jnp.concatenate(_split3(jnp.concatenate([v["dtp", c], cum], axis=0)), axis=1), e3_ref[...],
                                   preferred_element_type=f32)

    def level3(h):
        for c in chunks_of(h):
            u = u_buf[crows(c), :]
            dt_exp, acs_exp, acs_row = v["both", c][0:q], v["both", c][q:2 * q], v["acs_row", c]
            xdt = u[:, 0:D_SSM] * dt_exp
            xdt_b = xdt.astype(bf16)
            parts = []
            for g in sgroups:
                for pair in range(hpg // 2):
                    mats = []
                    for hh in range(2):
                        hd = g * hpg + pair * 2 + hh
                        col = acs_exp[:, hd * SSM_HEAD_DIM:hd * SSM_HEAD_DIM + 1]
                        diff = jnp.where(causal, col - acs_row[hd:hd + 1, :], -jnp.inf)
                        mats.append((v["cb", c, g] * jnp.exp2(diff)).astype(bf16))
                    c0 = (g * hpg + pair * 2) * SSM_HEAD_DIM
                    xp = xdt_b[:, c0:c0 + LANES]
                    rhs = jnp.concatenate([jnp.where(lo, xp, jnp.zeros_like(xp)), jnp.where(lo, jnp.zeros_like(xp), xp)],
                                          axis=0)
                    parts.append(jnp.dot(jnp.concatenate(mats, axis=1), rhs, preferred_element_type=f32))
            v["ydiag", c] = jnp.concatenate(parts, axis=1)
            acs_last = acs_exp[q - 1:q, :]
            xw = (xdt * jnp.exp2(acs_last - acs_exp)).astype(bf16)
            v["s_new", c] = jnp.concatenate(
                [jnp.dot(b_of(u, g).T.astype(bf16), xw[:, gsl(g)], preferred_element_type=f32) for g in sgroups], axis=1)
            v["acs", c] = acs_exp

    def level4():
        state = st[...]
        for c in range(n_sub * cps):
            u = u_buf[crows(c), :]
            state_b = state.astype(bf16)
            v["yoff", c] = jnp.concatenate(
                [jnp.dot(c_of(u, g).astype(bf16), state_b[:, gsl(g)], preferred_element_type=f32) for g in sgroups], axis=1)
            state = state * jnp.exp2(v["acs", c][q - 1:q, :]) + v["s_new", c]
        st[...] = state
        state_ref[0] = state

    def level5():
        for c in range(n_sub * cps):
            xs = u_buf[crows(c), 0:D_SSM]
            y = v["ydiag", c] + v["yoff", c] * jnp.exp2(v["acs", c])
            y = (y + dskip_ref[...] * xs) * _silu(z_buf[crows(c), :])
            outs = []
            for g in sgroups:
                yg = y[:, gsl(g)]
                outs.append(yg * lax.rsqrt(jnp.mean(yg * yg, -1, keepdims=True) + RMS_EPS))
            yc_ref[crows(c), :] = (jnp.concatenate(outs, axis=1) * norm_ref[...]).astype(yc_ref.dtype)

    project(0)
    project(1)
    for s in range(n_sub + 3):
        if s < n_sub:
            tail(s)
        if s + 2 < n_sub:
            project(s + 2)
        for level, lag in ((level1, 1), (level2, 2), (level3, 3)):
            if 0 <= s - lag < n_sub:
                level(s - lag)
    carry_history()
    level4()
    level5()


def _inproj_ssd_prompt(x, layer, batch, seq, g, w, pbd, pscale, cw, cb, dtb, cos, sin, cosq, sinq, a2_pad, e3_mat, dskip, norm):
    m = x.shape[0]
    tm = INPROJ_TILE
    tps = seq // tm
    row = lambda wd: pl.BlockSpec((tm, wd), lambda i: (i, 0))
    slab = pl.BlockSpec((2, tm, LANES), lambda i: (0, i, 0))
    tab = pl.BlockSpec((tm, LANES), lambda i: (i % tps, 0))
    per_seq = lambda r, c: pl.BlockSpec((1, r, c), lambda i: (i // tps, 0, 0))
    out_shape = (
        jax.ShapeDtypeStruct((m, D_POOL), bf16),
        jax.ShapeDtypeStruct((2, m, LANES), f32),
        jax.ShapeDtypeStruct((2, m, LANES), f32),
        jax.ShapeDtypeStruct((2, m, LANES), f32),
        jax.ShapeDtypeStruct((m, D_SSM), bf16),
        jax.ShapeDtypeStruct((batch, POOL_HALO, D_POOL), f32),
        jax.ShapeDtypeStruct((batch, CONV_HALO, D_CONV), f32),
        jax.ShapeDtypeStruct((batch, SSM_STATE, D_SSM), f32),
    )
    out_specs = (row(D_POOL), slab, slab, slab, row(D_SSM), per_seq(POOL_HALO, D_POOL), per_seq(CONV_HALO, D_CONV),
                 per_seq(SSM_STATE, D_SSM))
    consts = [pbd, pscale, cw, cb, dtb]
    ssd_consts = [a2_pad, e3_mat, dskip, norm]
    in_specs = ([row(D_MODEL), _layer_spec(g, layer), _layer_spec(w, layer)] + [_layer_spec(c, layer) for c in consts]
                + [tab, tab, tab, tab]
                + [_layer_spec(a2_pad, layer), _const_spec(e3_mat.shape), _layer_spec(dskip, layer), _layer_spec(norm, layer)])
    return pl.pallas_call(
        functools.partial(_inproj_ssd_kernel, tps),
        grid=(m // tm,),
        in_specs=in_specs,
        out_specs=out_specs,
        out_shape=out_shape,
        scratch_shapes=[pltpu.VMEM((POOL_HALO + tm, D_POOL), f32), pltpu.VMEM((CONV_HALO + tm, D_CONV), f32),
                        pltpu.VMEM((tm, D_CONV), f32), pltpu.VMEM((tm, D_SSM), f32), pltpu.VMEM((tm, LANES), f32),
                        pltpu.VMEM((SSM_STATE, D_SSM), f32)],
        compiler_params=_params(1),
        name="inproj_ssd",
    )(x, g, w, *consts, cos, sin, cosq, sinq, *ssd_consts)


def _attn_kernel(has_prev, q_ref, kp_ref, kc_ref, vp_ref, vc_ref, *rest):
    if has_prev:
        rest = rest[2:]
    o_ref, kwin_ref, vwin_ref, acc, mrun, lrun = rest
    t = pl.program_id(2)
    blk = N_BACK

    @pl.when(t == pl.num_programs(2) - 1)
    def _():
        kwin_ref[0] = kc_ref[0, 0, ATT_TILE - ATT_WIN:ATT_TILE, :].T
        vwin_ref[0] = vc_ref[0, 0, ATT_TILE - ATT_WIN:ATT_TILE, :].T

    qi = lax.broadcasted_iota(jnp.int32, (2 * blk, 2 * blk), 0) % blk
    kj = lax.broadcasted_iota(jnp.int32, (2 * blk, 2 * blk), 1)
    dist = blk + qi - kj
    band = (dist >= 0) & (dist <= N_BACK)
    bias_full = jnp.where(band, 0.0, -jnp.inf).astype(f32)
    bias_cur = jnp.where(band & (kj >= blk), 0.0, -jnp.inf).astype(f32)
    bias_first = jnp.where(t > 0, bias_full, bias_cur)
    head0 = lax.broadcasted_iota(jnp.int32, (blk, LANES), 1) < HEAD_DIM
    head0_k = lax.broadcasted_iota(jnp.int32, (2 * blk, LANES), 1) < HEAD_DIM
    ones_blk = jnp.concatenate([jnp.where(head0_k, 1.0, 0.0), jnp.where(head0_k, 0.0, 1.0)], axis=0).astype(bf16)
    n_units = ATT_TILE // blk

    def strided(ref, start, n, d):
        return ref[0, 0, pl.ds(start, n, stride=d) if d > 1 else pl.ds(start, n), :]

    def group(d, units):
        first = d == DILATIONS[0]
        last = d == DILATIONS[-1]
        span = blk * d
        rows, scores, vcats = [], [], []
        for u in units:
            sb, r = divmod(u, d)
            qstart = sb * span + r
            rows.append(pl.ds(qstart, blk, stride=d) if d > 1 else pl.ds(qstart, blk))
            qb = strided(q_ref, qstart, blk, d).astype(bf16)
            if sb > 0:
                kf = strided(kc_ref, qstart - span, 2 * blk, d)
                vf = strided(vc_ref, qstart - span, 2 * blk, d)
                bias = bias_full
            else:
                kf = jnp.concatenate([strided(kp_ref, ATT_TILE - span + r, blk, d), strided(kc_ref, r, blk, d)], axis=0)
                vf = jnp.concatenate([strided(vp_ref, ATT_TILE - span + r, blk, d), strided(vc_ref, r, blk, d)], axis=0)
                bias = bias_first
            kb = kf.astype(bf16)
            vb = vf.astype(bf16)
            qcat = jnp.concatenate([jnp.where(head0, qb, jnp.zeros_like(qb)), jnp.where(head0, jnp.zeros_like(qb), qb)], axis=0)
            scores.append(lax.dot_general(qcat, kb, (((1,), (1,)), ((), ())), preferred_element_type=f32) + bias)
            vcat = jnp.concatenate([jnp.where(head0_k, vb, jnp.zeros_like(vb)), jnp.where(head0_k, jnp.zeros_like(vb), vb)],
                                   axis=0)
            vcats.append(jnp.concatenate([vcat, ones_blk], axis=1))
        stats = []
        for sc in scores:
            m2 = jnp.max(sc, axis=1, keepdims=True)
            pb = jnp.exp2(sc - m2).astype(bf16)
            stats.append((jnp.concatenate([pb[0:blk], pb[blk:2 * blk]], axis=1), jnp.where(head0, m2[0:blk], m2[blk:2 * blk])))
        pvs = [jnp.dot(pcat, vcat, preferred_element_type=f32) for (pcat, _), vcat in zip(stats, vcats)]
        for qrows, pvl, (_, m_e) in zip(rows, pvs, stats):
            pv, l_e = pvl[:, 0:LANES], pvl[:, LANES:2 * LANES]
            if first:
                acc[qrows, :] = pv
                mrun[qrows, :] = m_e
                lrun[qrows, :] = l_e
            else:
                m_old = mrun[qrows, :]
                m_new = jnp.maximum(m_old, m_e)
                a = jnp.exp2(m_old - m_new)
                b = jnp.exp2(m_e - m_new)
                acc_new = acc[qrows, :] * a + pv * b
                l_new = lrun[qrows, :] * a + l_e * b
                if last:
                    o_ref[0, 0, qrows, :] = (acc_new / l_new).astype(o_ref.dtype)
                else:
                    acc[qrows, :] = acc_new
                    lrun[qrows, :] = l_new
                    mrun[qrows, :] = m_new

    for d in DILATIONS:
        for g0 in range(0, n_units, ATT_UNROLL):
            group(d, range(g0, g0 + ATT_UNROLL))


def _attn_prompt(layer, depth, q, k, v, batch, seq, prev):
    assert ATT_TILE % ATT_WIN == 0
    nt = seq // ATT_TILE
    q4 = q.reshape(2, batch, seq, LANES)
    k4 = k.reshape(2, batch, seq, LANES)
    v4 = v.reshape(2, batch, seq, LANES)
    cur = pl.BlockSpec((1, 1, ATT_TILE, LANES), lambda b, s, t: (s, b, t, 0))
    before = pl.BlockSpec((1, 1, ATT_TILE, LANES), lambda b, s, t: (s, b, jnp.maximum(t - 1, 0), 0))
    win = pl.BlockSpec((None, 1, LANES, ATT_WIN), lambda b, s, t: (layer, b, s, 0))
    win_shape = jax.ShapeDtypeStruct((depth, batch, D_ATT, ATT_WIN), f32)
    args, in_specs, aliases = [q4, k4, k4, v4, v4], [cur, before, cur, before, cur], {}
    if prev is not None:
        in_specs += [pl.BlockSpec(memory_space=pl.ANY)] * 2
        aliases = {5: 1, 6: 2}
        args += list(prev)
    out, kwin, vwin = pl.pallas_call(
        functools.partial(_attn_kernel, prev is not None),
        grid=(batch, 2, nt),
        in_specs=in_specs,
        out_specs=(cur, win, win),
        out_shape=(jax.ShapeDtypeStruct((2, batch, seq, LANES), bf16), win_shape, win_shape),
        scratch_shapes=[pltpu.VMEM((ATT_TILE, LANES), f32), pltpu.VMEM((ATT_TILE, LANES), f32),
                        pltpu.VMEM((ATT_TILE, LANES), f32)],
        input_output_aliases=aliases,
        compiler_params=_params(3),
        name="attn",
    )(*args)
    return out.reshape(2, batch * seq, LANES), (kwin, vwin)


def _sample_pre_kernel(x_ref, g_ref, w_ref, cpool_ref, pbd_ref, pscale_ref, cconv_ref, cw_ref, cb_ref, dtb_ref,
                       cos_ref, sin_ref, ya_ref, q_ref, k_ref, v_ref, z_ref, u_ref, dt_ref, pnew_ref, cnew_ref):
    xn = _rms(x_ref[...], g_ref[...]).astype(bf16)
    proj = jnp.dot(xn, w_ref[...], preferred_element_type=f32)

    xa = proj[:, 0:D_POOL]
    back = lambda i: cpool_ref[POOL_BUF - i]
    s2 = xa + back(1)
    s4 = s2 + back(2) + back(3)
    s8 = s4 + back(4) + back(5) + back(6) + back(7)
    s16 = s8
    for i in range(8, 16):
        s16 = s16 + back(i)
    cnt = jnp.minimum(_pool_window(xa.shape), PAST_LEN + 1).astype(f32)
    dmean = _pool_select(s2, s4, s8, s16) / cnt - xa
    ya = jnp.dot(dmean.astype(bf16), pbd_ref[...], preferred_element_type=f32) * pscale_ref[...]
    ya_ref[...] = ya.astype(ya_ref.dtype)
    for i in range(POOL_BUF - 1):
        pnew_ref[i] = cpool_ref[i + 1]
    pnew_ref[POOL_BUF - 1] = xa

    cos = cos_ref[...]
    sin = sin_ref[...]
    for s in range(2):
        sl = slice(s * LANES, (s + 1) * LANES)
        q_ref[:, sl] = _rope_slab(proj[:, OFF_Q + s * LANES:OFF_Q + (s + 1) * LANES], cos, sin)
        k_ref[:, sl] = _rope_slab(proj[:, OFF_K + s * LANES:OFF_K + (s + 1) * LANES], cos, sin)
    v_ref[...] = proj[:, OFF_V:OFF_V + D_ATT]
    z_ref[...] = proj[:, OFF_Z:OFF_Z + D_SSM]

    xbc = proj[:, OFF_XBC:OFF_XBC + D_CONV]
    conv = cb_ref[...] + xbc * cw_ref[CONV_WIDTH - 1:CONV_WIDTH, :]
    for j in range(1, CONV_WIDTH):
        conv = conv + cconv_ref[CONV_WIDTH - 1 - j] * cw_ref[CONV_WIDTH - 1 - j:CONV_WIDTH - j, :]
    u_ref[...] = _silu(conv)
    for j in range(CONV_WIDTH - 2):
        cnew_ref[j] = cconv_ref[j + 1]
    cnew_ref[CONV_WIDTH - 2] = xbc

    dt_ref[...] = _softplus(proj[:, OFF_DT:OFF_DT + LANES] + dtb_ref[...])


def _sample_pre(x, layer, g, w, cpool, pbd, pscale, cconv, cw, cb, dtb, cos, sin):
    n = x.shape[0]
    args = [x, g, w, cpool, pbd, pscale, cconv, cw, cb, dtb, cos, sin]
    out_shape = (
        jax.ShapeDtypeStruct((n, D_POOL), bf16),
        jax.ShapeDtypeStruct((n, D_ATT), f32),
        jax.ShapeDtypeStruct((n, D_ATT), f32),
        jax.ShapeDtypeStruct((n, D_ATT), f32),
        jax.ShapeDtypeStruct((n, D_SSM), f32),
        jax.ShapeDtypeStruct((n, D_CONV), f32),
        jax.ShapeDtypeStruct((n, LANES), f32),
        jax.ShapeDtypeStruct((POOL_BUF, n, D_POOL), f32),
        jax.ShapeDtypeStruct((CONV_WIDTH - 1, n, D_CONV), f32),
    )
    full = lambda s: pl.BlockSpec(s.shape, lambda i, nd=len(s.shape): (0,) * nd)
    in_specs = [full(x)] + [_layer_spec(a, layer) for a in args[1:-2]] + [full(cos), full(sin)]
    return pl.pallas_call(
        _sample_pre_kernel,
        grid=(1,),
        in_specs=in_specs,
        out_specs=tuple(full(s) for s in out_shape),
        out_shape=out_shape,
        compiler_params=_params(1),
        name="sample_pre",
    )(*args)


def _sample_ssd_kernel(u_ref, dt_ref, z_ref, st_ref, aexp_ref, e_ref, dskip_ref, norm_ref, *rest):
    y_ref, stnew_ref = rest[-2:]
    n = u_ref.shape[0]
    gl = D_SSM // SSM_GROUPS
    u = u_ref[...]
    xs = u[:, 0:D_SSM]
    dt_exp = _dot_f32_by_01(dt_ref[...], e_ref[...])
    pad = jnp.zeros((LANES - n, D_SSM), f32)
    dec_t = jnp.concatenate([jnp.exp(dt_exp * aexp_ref[...]), pad], axis=0).T
    dtx_t = jnp.concatenate([dt_exp * xs, pad], axis=0).T
    lane = lax.broadcasted_iota(jnp.int32, (D_SSM, LANES), 1)
    y_t = jnp.zeros((D_SSM, LANES), f32)
    for i in range(n):
        dec = dec_t[:, i:i + 1]
        dtx = dtx_t[:, i:i + 1]
        ycols = []
        for g in range(SSM_GROUPS):
            rows = slice(g * gl, (g + 1) * gl)
            b_row = u[i:i + 1, D_SSM + g * SSM_STATE:D_SSM + (g + 1) * SSM_STATE]
            c_row = u[i:i + 1, D_SSM + (SSM_GROUPS + g) * SSM_STATE:D_SSM + (SSM_GROUPS + g + 1) * SSM_STATE]
            h_new = dec[rows] * st_ref[i, rows, :] + dtx[rows] * b_row
            stnew_ref[i, rows, :] = h_new
            ycols.append(jnp.sum(h_new * c_row, axis=1, keepdims=True))
        y_t = jnp.where(lane == i, jnp.concatenate(ycols, axis=0), y_t)
    y = y_t.T[0:n]
    y = (y + dskip_ref[...] * xs) * _silu(z_ref[...])
    outs = []
    for g in range(SSM_GROUPS):
        yg = y[:, g * gl:(g + 1) * gl]
        outs.append(yg * lax.rsqrt(jnp.mean(yg * yg, -1, keepdims=True) + RMS_EPS))
    y_ref[...] = (jnp.concatenate(outs, axis=1) * norm_ref[...]).astype(y_ref.dtype)


def _sample_ssd(layer, u, dt, z, state, prev, a_exp, e_mat, dskip, norm):
    n = u.shape[0]
    full = lambda s: pl.BlockSpec(s.shape, lambda i, nd=len(s.shape): (0,) * nd)
    of_layer = pl.BlockSpec((None,) + state.shape[1:], lambda i: (layer, 0, 0, 0))
    args = [u, dt, z, state, a_exp, e_mat, dskip, norm]
    in_specs = [full(u), full(dt), full(z), of_layer, _layer_spec(a_exp, layer), full(e_mat), _layer_spec(dskip, layer),
                _layer_spec(norm, layer)]
    aliases = {}
    if prev is not None:
        in_specs.append(pl.BlockSpec(memory_space=pl.ANY))
        aliases = {len(args): 1}
        args.append(prev)
    y_shape = jax.ShapeDtypeStruct((n, D_SSM), bf16)
    return pl.pallas_call(
        _sample_ssd_kernel,
        grid=(1,),
        in_specs=in_specs,
        out_specs=(full(y_shape), of_layer),
        out_shape=(y_shape, jax.ShapeDtypeStruct(state.shape, f32)),
        input_output_aliases=aliases,
        compiler_params=_params(1),
        name="sample_ssd",
    )(*args)


def _sample_attn_kernel(has_prev, q_ref, kn_ref, vn_ref, knc_ref, vnc_ref, kc_ref, vc_ref, *rest):
    if has_prev:
        rest = rest[2:]
    y_ref, ks_ref, vs_ref = rest
    wb = kc_ref.shape[3]
    kc = kc_ref[0, 0]
    vc = vc_ref[0, 0]
    q = q_ref[0]
    kn = kn_ref[0]
    vn = vn_ref[0]
    scale = 1.0 / math.sqrt(HEAD_DIM)

    own = (lax.broadcasted_iota(jnp.int32, (SUBLANES, D_ATT), 1) // HEAD_DIM) == lax.broadcasted_iota(
        jnp.int32, (SUBLANES, D_ATT), 0)
    qbd = jnp.where(own, jnp.broadcast_to(q, (SUBLANES, D_ATT)), 0.0)
    s_all = jnp.dot(qbd.astype(bf16), kc.astype(bf16), preferred_element_type=f32) * scale
    s_new = jnp.sum(jnp.where(own, jnp.broadcast_to(q * kn, (SUBLANES, D_ATT)), 0.0), axis=1, keepdims=True) * scale
    dist = wb - lax.broadcasted_iota(jnp.int32, (SUBLANES, wb), 1)
    ms, ps, ls, es = [], [], [], []
    for d in DILATIONS:
        valid = ((dist % d) == 0) & (dist <= N_BACK * d)
        s = jnp.where(valid, s_all, -jnp.inf)
        m = jnp.maximum(jnp.max(s, axis=1, keepdims=True), s_new)
        p = jnp.exp(s - m)
        e_new = jnp.exp(s_new - m)
        ms.append(m)
        ps.append(p)
        es.append(e_new)
        ls.append(jnp.sum(p, axis=1, keepdims=True) + e_new)
    m_all = jnp.maximum(jnp.maximum(ms[0], ms[1]), ms[2])
    p_all = jnp.zeros_like(ps[0])
    w_new = jnp.zeros_like(s_new)
    l_all = jnp.zeros_like(s_new)
    for m, p, e_new, l in zip(ms, ps, es, ls):
        c = jnp.exp(m - m_all)
        p_all = p_all + c * p
        w_new = w_new + c * e_new
        l_all = l_all + c * l
    o_full = lax.dot_general(p_all.astype(bf16), vc.astype(bf16), (((1,), (1,)), ((), ())), preferred_element_type=f32)
    to_row = lambda t: jnp.sum(jnp.where(own, t, 0.0), axis=0, keepdims=True)
    o_row = to_row(o_full) + to_row(jnp.broadcast_to(w_new, (SUBLANES, D_ATT))) * vn
    y_ref[0] = o_row / to_row(jnp.broadcast_to(l_all, (SUBLANES, D_ATT)))

    newest = lax.broadcasted_iota(jnp.int32, (D_ATT, wb), 1) == wb - 1
    ks_ref[0, 0] = jnp.where(newest, knc_ref[0], pltpu.roll(kc, wb - 1, 1))
    vs_ref[0, 0] = jnp.where(newest, vnc_ref[0], pltpu.roll(vc, wb - 1, 1))


def _sample_attn(layer, q, kn, vn, cache_k, cache_v, prev):
    depth, n, da, wb = cache_k.shape
    rowv = pl.BlockSpec((1, 1, da), lambda i: (i, 0, 0))
    colv = pl.BlockSpec((1, da, 1), lambda i: (i, 0, 0))
    big = pl.BlockSpec((1, 1, da, wb), lambda i: (layer, i, 0, 0))
    in_specs = [rowv, rowv, rowv, colv, colv, big, big]
    args = [q.reshape(n, 1, da), kn.reshape(n, 1, da), vn.reshape(n, 1, da), kn.reshape(n, da, 1), vn.reshape(n, da, 1),
            cache_k, cache_v]
    aliases = {}
    if prev is not None:
        in_specs += [pl.BlockSpec(memory_space=pl.ANY)] * 2
        aliases = {len(args): 1, len(args) + 1: 2}
        args += list(prev)
    return pl.pallas_call(
        functools.partial(_sample_attn_kernel, prev is not None),
        grid=(n,),
        in_specs=in_specs,
        out_specs=(rowv, big, big),
        out_shape=(jax.ShapeDtypeStruct((n, 1, da), f32), jax.ShapeDtypeStruct(cache_k.shape, f32),
                   jax.ShapeDtypeStruct(cache_v.shape, f32)),
        input_output_aliases=aliases,
        compiler_params=_params(1),
        name="sample_attn",
    )(*args)


def _rope_tables(pos):
    half = HEAD_DIM // 2
    inv = ROPE_THETA ** (-jnp.arange(half, dtype=f32) / half)
    ang = pos.astype(f32)[:, None] * inv[None]
    cos = jnp.tile(jnp.cos(ang), (1, LANES // half))
    sin = jnp.sin(ang)
    sin_signed = jnp.tile(jnp.concatenate([-sin, sin], axis=1), (1, LANES // HEAD_DIM))
    return cos, sin_signed


def kernel(x_prompt, x_sample, cache_pool, cache_k, cache_v, state_conv, state_ssm, ffn1_norm, ffn1_w_gate, ffn1_w_up,
           ffn1_w_down, mix_norm, w_in, pool_w, pool_scale, conv_w, conv_b, dt_bias, a_log, d_skip, ssm_norm, w_out,
           ffn2_norm, ffn2_w_gate, ffn2_w_up, ffn2_w_down, final_norm):
    batch, seq, _ = x_prompt.shape
    n_dec = x_sample.shape[0]
    depth = w_in.shape[0]
    wb = cache_k.shape[2]
    assert all(seq % tile == 0 for tile in (ATT_TILE, ROW_TILE, INPROJ_TILE))
    assert x_sample.shape[1] == 1 and wb == ATT_WIN

    yp = x_prompt.reshape(batch * seq, D_MODEL)
    ys = x_sample.reshape(n_dec, D_MODEL)
    cos_p, sin_p = _rope_tables(jnp.arange(seq))
    cos_s, sin_s = _rope_tables(jnp.full((1,), PAST_LEN))
    e_mat = (jnp.arange(LANES)[:, None] == (jnp.arange(D_SSM)[None, :] // SSM_HEAD_DIM)).astype(bf16)
    e3_mat = jnp.tile(e_mat, (3, 1))
    ck_t = jnp.transpose(cache_k, (0, 1, 3, 4, 2)).reshape(depth, n_dec, D_ATT, wb)
    cv_t = jnp.transpose(cache_v, (0, 1, 3, 4, 2)).reshape(depth, n_dec, D_ATT, wb)
    row2 = lambda a: a.reshape(1, -1)
    row3 = lambda a: a.reshape(depth, 1, -1)

    f1 = (row3(ffn1_norm), ffn1_w_gate.astype(bf16), ffn1_w_up.astype(bf16), ffn1_w_down.astype(bf16))
    f2 = (row3(ffn2_norm), ffn2_w_gate.astype(bf16), ffn2_w_up.astype(bf16), ffn2_w_down.astype(bf16))
    w_all = jnp.pad(w_in, ((0, 0), (0, 0), (0, N_PROJ - D_IN_PROJ))).astype(bf16)
    wo_all = w_out.astype(bf16)
    g_mix = row3(mix_norm)
    n_pool = len(POOL_WINDOWS)
    pbd = (pool_w[:, :, :, None, :] * jnp.eye(n_pool, dtype=f32)[None, :, None, :, None]).reshape(
        depth, D_POOL, D_POOL).astype(bf16)
    pscale = row3(pool_scale)
    cb = row3(conv_b)
    head_pad = ((0, 0), (0, LANES - N_SSM_HEADS))
    dtb = row3(jnp.pad(dt_bias, head_pad))
    a_neg = -jnp.exp(a_log.astype(f32))
    a2_pad = row3(jnp.pad(a_neg * math.log2(math.e), head_pad))
    a_exp = row3(jnp.repeat(a_neg, SSM_HEAD_DIM, axis=1))
    dskip = row3(jnp.repeat(d_skip, SSM_HEAD_DIM, axis=1))
    norm = row3(ssm_norm)
    cpool = jnp.transpose(cache_pool, (0, 2, 1, 3))
    cconv = jnp.transpose(state_conv, (0, 2, 1, 3))
    cosq_p, sinq_p = cos_p * ATT_SCALE, sin_p * ATT_SCALE

    st_p = [[] for _ in range(3)]
    st_s = [[] for _ in range(2)]
    kv_p = kv_s = ssm_s = None
    for i in range(depth):
        last = i == depth - 1
        fin = row2(final_norm) if last else None

        yp = _ffn(yp, i, *f1)
        ya, q, k, v, yc, pst, cst, sst = _inproj_ssd_prompt(yp, i, batch, seq, g_mix, w_all, pbd, pscale, conv_w, cb, dtb,
                                                            cos_p, sin_p, cosq_p, sinq_p, a2_pad, e3_mat, dskip, norm)
        yb, kv_p = _attn_prompt(i, depth, q, k, v, batch, seq, kv_p)
        yp = _ffn(yp, i, *f2, mix=(ya, yb, yc, wo_all), final_g=fin)

        st_p[0].append(pst[:, POOL_HALO - POOL_BUF:])
        st_p[1].append(cst[:, CONV_HALO - (CONV_WIDTH - 1):])
        st_p[2].append(jnp.transpose(sst.reshape(batch, SSM_STATE, N_SSM_HEADS, SSM_HEAD_DIM), (0, 2, 3, 1)))

        ys = _ffn(ys, i, *f1)
        ya_s, q_s, k_s, v_s, z_s, u_s, dt_s, pnew, cnew = _sample_pre(ys, i, g_mix, w_all, cpool, pbd, pscale, cconv,
                                                                    conv_w, cb, dtb, cos_s, sin_s)
        yb_s, ks_buf, vs_buf = _sample_attn(i, q_s, k_s, v_s, ck_t, cv_t, kv_s)
        kv_s = (ks_buf, vs_buf)
        yc_s, ssm_s = _sample_ssd(i, u_s, dt_s, z_s, state_ssm.reshape(depth, n_dec, D_SSM, SSM_STATE), ssm_s, a_exp, e_mat,
                                  dskip, norm)
        yb_slab = jnp.transpose(yb_s.reshape(n_dec, 2, LANES), (1, 0, 2)).astype(bf16)
        ys = _ffn(ys, i, *f2, mix=(ya_s, yb_slab, yc_s, wo_all), final_g=fin)
        st_s[0].append(jnp.transpose(pnew, (1, 0, 2)))
        st_s[1].append(jnp.transpose(cnew, (1, 0, 2)))

    y_prompt = yp.reshape(batch, seq, D_MODEL)
    y_sample = ys.reshape(n_dec, 1, D_MODEL)
    pool_p, conv_p, ssm_p = (jnp.stack(s, 0) for s in st_p)
    pool_s, conv_s = (jnp.stack(s, 0) for s in st_s)
    ssm_s = ssm_s.reshape(state_ssm.shape)
    from_t = lambda t: jnp.transpose(t.reshape(depth, -1, N_ATT_HEADS, HEAD_DIM, wb), (0, 1, 4, 2, 3))
    k_p, v_p = from_t(kv_p[0]), from_t(kv_p[1])
    return (y_prompt, y_sample, pool_p, pool_s, k_p, from_t(kv_s[0]), v_p, from_t(kv_s[1]), conv_p, conv_s, ssm_p, ssm_s)
```
